```python
import math
import jax
import jax.numpy as jnp
from jax import lax
import numpy as np

D_MODEL = 2048
BATCH = 8
SEQ = 2048
DEPTH = 2

GRID_W = 64
CTX_LEN = 256
HEAD_DIM = 64
ROPE_THETA = 10000.0
NORM_EPS = 1e-6
N_MIXERS = 4
GROUP_W = D_MODEL // N_MIXERS

GLA_DV = 64
GLA_DK = GLA_DV // 2
GLA_HEADS = GROUP_W // GLA_DV
GLA_GATE_RANK = 16
GLA_GATE_TEMP = 16.0
GLA_CHUNK = 16
SWA_HEADS = GROUP_W // HEAD_DIM
SWA_KV_HEADS = SWA_HEADS // 4
SWA_WINDOW = 128
SWA_BLOCK = 128
HYENA_CH = GROUP_W
HYENA_ORDER = 2
HYENA_BANDS = 16
HYENA_EMB = 1 + 2 * HYENA_BANDS
HYENA_HIDDEN = 64
HYENA_TARGET = 1e-2
HYENA_MIN_DECAY = math.log(1.0 / HYENA_TARGET) / 1.5
HYENA_MAX_DECAY = math.log(1.0 / HYENA_TARGET) / 0.3
DIFF_QK_DIM = HEAD_DIM
DIFF_V_DIM = 2 * HEAD_DIM
DIFF_HEADS = GROUP_W // DIFF_V_DIM
DIFF_BLOCK = 128
N_EXPERTS = 16
N_GROUPS = 4
EXPERTS_PER_GROUP = N_EXPERTS // N_GROUPS
TOP_K = 2
D_EXPERT = 1024

COL_SIZES = (GLA_HEADS * GLA_DK, GLA_HEADS * GLA_DK, GLA_HEADS * GLA_DV, GLA_HEADS * GLA_DV, 2 * GLA_GATE_RANK,
             SWA_HEADS * HEAD_DIM, SWA_KV_HEADS * HEAD_DIM, SWA_KV_HEADS * HEAD_DIM,
             (HYENA_ORDER + 1) * HYENA_CH,
             DIFF_HEADS * 2 * DIFF_QK_DIM, DIFF_HEADS * 2 * DIFF_QK_DIM, DIFF_HEADS * DIFF_V_DIM)
D_IN = sum(COL_SIZES)
COL_OFFSETS = tuple(int(o) for o in np.cumsum(COL_SIZES)[:-1])
D_MIX = GLA_HEADS * GLA_DV + SWA_HEADS * HEAD_DIM + HYENA_CH + DIFF_HEADS * DIFF_V_DIM

kernel_name = "hybrid_parallel_heads_moe_dit"

F32 = jnp.float32


def rmsnorm(x, w):
    xf = x.astype(F32)
    y = xf * lax.rsqrt(jnp.mean(xf * xf, axis=-1, keepdims=True) + NORM_EPS)
    return (y * w.astype(F32)).astype(x.dtype)


def rope_tables(row_ids, col_ids):
    n = HEAD_DIM // 4
    inv = ROPE_THETA ** (-jnp.arange(n, dtype=F32) / n)
    ar = row_ids[:, None] * inv
    ac = col_ids[:, None] * inv
    return (jnp.cos(ar), jnp.sin(ar), jnp.cos(ac), jnp.sin(ac))


def rope_2d(x, rope):
    cos_r, sin_r, cos_c, sin_c = rope
    shp = (1, x.shape[1]) + (1,) * (x.ndim - 3) + (cos_r.shape[-1],)

    def rot(xh, cos, sin):
        cos = cos.reshape(shp).astype(xh.dtype)
        sin = sin.reshape(shp).astype(xh.dtype)
        x1, x2 = jnp.split(xh, 2, axis=-1)
        return jnp.concatenate([x1 * cos - x2 * sin, x2 * cos + x1 * sin], axis=-1)

    xr, xcol = jnp.split(x, 2, axis=-1)
    return jnp.concatenate([rot(xr, cos_r, sin_r), rot(xcol, cos_c, sin_c)], axis=-1)


def gla_scan(q, k, v, log_a, s0, with_output):
    B, L, H, dk = k.shape
    dv = v.shape[-1]
    C = GLA_CHUNK
    n = L // C
    r = lambda t: t.reshape(B, n, C, H, t.shape[-1]).astype(F32)
    q, k, v, log_a = r(q), r(k), r(v), r(log_a)
    b = jnp.cumsum(log_a, axis=2)
    b_last = b[:, :, -1]
    chunk_kv = jnp.einsum('bnchd,bnchv->bnhdv', k * jnp.exp(b_last[:, :, None] - b), v)
    chunk_decay = jnp.exp(b_last)

    def step(s, inp):
        dec, kv = inp
        return dec[..., None] * s + kv, s

    s_final, s_start = lax.scan(step, s0, (jnp.moveaxis(chunk_decay, 1, 0), jnp.moveaxis(chunk_kv, 1, 0)))
    if not with_output:
        return None, s_final
    s_start = jnp.moveaxis(s_start, 0, 1)
    o_inter = jnp.einsum('bnchd,bnhdv->bnchv', q * jnp.exp(b), s_start)
    tri = jnp.tril(jnp.ones((C, C), dtype=bool))
    diff = b[:, :, :, None] - b[:, :, None, :]
    decay = jnp.exp(jnp.where(tri[None, None, :, :, None, None], diff, -jnp.inf))
    attn = jnp.einsum('bnthd,bnshd,bntshd->bnhts', q, k, decay)
    o_intra = jnp.einsum('bnhts,bnshv->bnthv', attn, v)
    return (o_inter + o_intra).reshape(B, L, H, dv), s_final


def gla_mixer(cols_c, cols_l, p, update_ctx):
    def prep(cols):
        q, k, v, g, a = cols
        B, L, _ = q.shape
        q = q.reshape(B, L, GLA_HEADS, GLA_DK) * (GLA_DK ** -0.5)
        k = k.reshape(B, L, GLA_HEADS, GLA_DK)
        v = v.reshape(B, L, GLA_HEADS, GLA_DV)
        z = jnp.einsum('bldr,drk->bldk', a.reshape(B, L, 2, GLA_GATE_RANK), p['gla_gate_w']) + p['gla_gate_b']
        la = (jax.nn.log_sigmoid(z.astype(F32)) / GLA_GATE_TEMP).reshape(B, L, 2, GLA_HEADS, GLA_DK)
        return q, k, v, g, la[:, :, 0], la[:, :, 1]

    qc, kc, vc, gc, lfc, lbc = prep(cols_c)
    ql, kl, vl, gl, lfl, lbl = prep(cols_l)
    B = ql.shape[0]
    s0 = jnp.zeros((B, GLA_HEADS, GLA_DK, GLA_DV), F32)
    fl = lambda t: jnp.flip(t, axis=1)
    oc_f, s_f = gla_scan(qc, kc, vc, lfc, s0, update_ctx)
    oc_b, s_b = gla_scan(fl(qc), fl(kc), fl(vc), fl(lbc), s0, update_ctx)
    ol_f, _ = gla_scan(ql, kl, vl, lfl, s_f, True)
    ol_b, _ = gla_scan(fl(ql), fl(kl), fl(vl), fl(lbl), s_b, True)

    def finish(o, g):
        o = rmsnorm(o.astype(g.dtype), p['gla_norm_w'])
        return o.reshape(g.shape) * jax.nn.silu(g)

    out_l = finish(ol_f + fl(ol_b), gl)
    out_c = finish(oc_f + fl(oc_b), gc) if update_ctx else None
    return out_c, out_l


def swa_latent(q, k, v, kc, vc, sink):
    B, S, Hq, hd = q.shape
    Hkv = k.shape[2]
    G = Hq // Hkv
    W = SWA_BLOCK
    nb = S // W
    Lc = kc.shape[1]
    scale = hd ** -0.5
    qb = q.reshape(B, nb, W, Hkv, G, hd)

    def band(t):
        tp = jnp.pad(t, ((0, 0), (W, W), (0, 0), (0, 0))).reshape(B, nb + 2, W, Hkv, hd)
        return jnp.concatenate([tp[:, :-2], tp[:, 1:-1], tp[:, 2:]], axis=2)

    kb, vb = band(k), band(v)
    s_loc = jnp.einsum('bnqhgd,bnkhd->bnhgqk', qb, kb).astype(F32) * scale
    blk = jnp.arange(nb)[:, None] * W
    qpos = blk + jnp.arange(W)[None, :]
    kpos = blk - W + jnp.arange(3 * W)[None, :]
    valid = ((jnp.abs(qpos[:, :, None] - kpos[:, None, :]) <= SWA_WINDOW)
             & (kpos >= 0)[:, None, :] & (kpos < S)[:, None, :])
    s_loc = jnp.where(valid[None, :, None, None], s_loc, -jnp.inf)
    s_ctx = jnp.einsum('bnqhgd,bkhd->bnhgqk', qb, kc).astype(F32) * scale
    s_sink = jnp.broadcast_to(sink.astype(F32).reshape(1, 1, Hkv, G, 1, 1), s_ctx.shape[:-1] + (1,))
    prob = jax.nn.softmax(jnp.concatenate([s_sink, s_ctx, s_loc], axis=-1), axis=-1).astype(v.dtype)
    o = (jnp.einsum('bnhgqk,bkhd->bnqhgd', prob[..., 1:1 + Lc], vc)
         + jnp.einsum('bnhgqk,bnkhd->bnqhgd', prob[..., 1 + Lc:], vb))
    return o.reshape(B, S, Hq * hd)


def swa_context(q, k, v, sink):
    B, L, Hq, hd = q.shape
    Hkv = k.shape[2]
    G = Hq // Hkv
    s = jnp.einsum('bqhgd,bkhd->bhgqk', q.reshape(B, L, Hkv, G, hd), k).astype(F32) * (hd ** -0.5)
    s_sink = jnp.broadcast_to(sink.astype(F32).reshape(1, Hkv, G, 1, 1), s.shape[:-1] + (1,))
    prob = jax.nn.softmax(jnp.concatenate([s_sink, s], axis=-1), axis=-1)[..., 1:].astype(v.dtype)
    return jnp.einsum('bhgqk,bkhd->bqhgd', prob, v).reshape(B, L, Hq * hd)


def swa_mixer(cols_c, cols_l, p, rope, update_ctx):
    qc, kc, vc = cols_c
    ql, kl, vl = cols_l
    heads = lambda t, n: t.reshape(t.shape[0], t.shape[1], n, HEAD_DIM)
    qn = lambda t: rmsnorm(heads(t, SWA_HEADS), p['swa_q_norm_w'])
    kn = lambda t: rmsnorm(heads(t, SWA_KV_HEADS), p['swa_k_norm_w'])
    kc_, vc_ = kn(kc), heads(vc, SWA_KV_HEADS)
    out_l = swa_latent(rope_2d(qn(ql), rope), rope_2d(kn(kl), rope), heads(vl, SWA_KV_HEADS),
                       kc_, vc_, p['swa_sink'])
    out_c = swa_context(qn(qc), kc_, vc_, p['swa_sink']) if update_ctx else None
    return out_c, out_l


def short_conv(u, w, b):
    up = jnp.pad(u, ((0, 0), (1, 1), (0, 0)))
    return up[:, :-2] * w[0] + up[:, 1:-1] * w[1] + up[:, 2:] * w[2] + b


def hyena_filter_spectrum(L, p):
    t = jnp.linspace(0.0, 1.0, L, dtype=F32)[:, None]
    w = (2.0 * math.pi / L) * jnp.arange(L, dtype=F32)[:, None]
    bands = jnp.linspace(1e-4, HYENA_BANDS - 1, HYENA_BANDS, dtype=F32)
    z = jnp.concatenate([t, jnp.cos(w * bands), -jnp.sin(w * bands)], axis=-1)
    fr = p['hyena_ffn_freq'].astype(F32)
    h = jnp.sin(fr[0] * (z @ p['hyena_ffn_w1'].astype(F32) + p['hyena_ffn_b1'].astype(F32)))
    h = jnp.sin(fr[1] * (h @ p['hyena_ffn_w2'].astype(F32) + p['hyena_ffn_b2'].astype(F32)))
    h = (h @ p['hyena_ffn_w3'].astype(F32)).reshape(L, HYENA_ORDER, 2, HYENA_CH)
    deltas = jnp.linspace(HYENA_MIN_DECAY, HYENA_MAX_DECAY, HYENA_CH, dtype=F32)
    h = h * jnp.exp(-t * deltas)[:, None, None, :]
    h = h / jnp.sum(jnp.abs(h), axis=(0, 2), keepdims=True)
    full = jnp.concatenate([h[:, :, 0], jnp.zeros((1, HYENA_ORDER, HYENA_CH), F32),
                            jnp.flip(h[1:, :, 1], axis=0)], axis=0)
    return jnp.fft.rfft(full, axis=0)


def fft_conv(u, spec):
    L = u.shape[1]
    U = jnp.fft.rfft(u.astype(F32), n=2 * L, axis=1)
    return jnp.fft.irfft(U * spec[None], n=2 * L, axis=1)[:, :L].astype(u.dtype)


def hyena_mixer(u_c, u_l, p, update_ctx):
    def run(u):
        z = short_conv(u, p['hyena_conv_w'], p['hyena_conv_b'])
        x1, x2, y = jnp.split(z, 3, axis=-1)
        spec = hyena_filter_spectrum(u.shape[1], p)
        for o, gate in enumerate((x1, x2)):
            y = gate * (fft_conv(y, spec[:, o]) + y * p['hyena_bias'][o])
        return y

    out_l = run(u_l)
    out_c = run(u_c) if update_ctx else None
    return out_c, out_l


def diff_attend(q, k, v, lam):
    s = jnp.einsum('bqhmd,bkhmd->bhmqk', q, k).astype(F32) * (q.shape[-1] ** -0.5)
    prob = jax.nn.softmax(s, axis=-1)
    wgt = prob[:, :, 0] - lam * prob[:, :, 1]
    return jnp.einsum('bhqk,bkhe->bqhe', wgt.astype(v.dtype), v)


def diff_attend_blocks(q, k, v, lam):
    B, S, H, M, d = q.shape
    nb = S // DIFF_BLOCK
    qb = jnp.moveaxis(q.reshape(B, nb, DIFF_BLOCK, H, M, d), 1, 0)
    o = lax.map(lambda qq: diff_attend(qq, k, v, lam), qb)
    return jnp.moveaxis(o, 0, 1).reshape(B, S, H, -1)


def diff_mixer(cols_c, cols_l, p, rope, lam_init, update_ctx):
    dl = p['diff_lambda'].astype(F32)
    lam = jnp.exp(jnp.sum(dl[0] * dl[1])) - jnp.exp(jnp.sum(dl[2] * dl[3])) + lam_init
    qk = lambda t: t.reshape(t.shape[0], t.shape[1], DIFF_HEADS, 2, DIFF_QK_DIM)
    vh = lambda t: t.reshape(t.shape[0], t.shape[1], DIFF_HEADS, DIFF_V_DIM)
    qn = lambda t: rmsnorm(qk(t), p['diff_q_norm_w'])
    kn = lambda t: rmsnorm(qk(t), p['diff_k_norm_w'])
    qc, kc, vc = cols_c
    ql, kl, vl = cols_l
    kc_, vc_ = kn(kc), vh(vc)
    k_all = jnp.concatenate([kc_, rope_2d(kn(kl), rope)], axis=1)
    v_all = jnp.concatenate([vc_, vh(vl)], axis=1)

    def post(o):
        o = rmsnorm(o, p['diff_subln_w']) * (1.0 - lam_init)
        return o.reshape(o.shape[0], o.shape[1], DIFF_HEADS * DIFF_V_DIM)

    out_l = post(diff_attend_blocks(rope_2d(qn(ql), rope), k_all, v_all, lam))
    out_c = post(diff_attend(qn(qc), kc_, vc_, lam)) if update_ctx else None
    return out_c, out_l


def moe(h, router_w, router_bias, w_gate, w_up, w_down):
    Bh, Lh, D = h.shape
    t = h.reshape(-1, D)
    probs = jax.nn.softmax((t @ router_w).astype(F32), axis=-1)
    sel = (probs + router_bias.astype(F32)).reshape(-1, N_GROUPS, EXPERTS_PER_GROUP)
    group = jnp.argmax(jnp.max(sel, axis=-1), axis=-1)
    sel_g = jnp.take_along_axis(sel, group[:, None, None], axis=1)[:, 0]
    _, idx = lax.top_k(sel_g, TOP_K)
    expert = group[:, None] * EXPERTS_PER_GROUP + idx
    wts = jnp.take_along_axis(probs, expert, axis=-1)
    wts = wts / jnp.sum(wts, axis=-1, keepdims=True)
    comb = jnp.sum(jax.nn.one_hot(expert, N_EXPERTS, dtype=F32) * wts[..., None], axis=1).astype(t.dtype)
    y = jnp.zeros_like(t)
    for e in range(N_EXPERTS):
        he = jax.nn.silu(t @ w_gate[e]) * (t @ w_up[e])
        y = y + comb[:, e:e + 1] * (he @ w_down[e])
    return y.reshape(Bh, Lh, D)


def hybrid_layer(x, xc, c, c_ctx, p, router_w, router_bias, rope, lam_init, update_ctx):
    Lc = xc.shape[1]
    mod_l = jnp.split(jax.nn.silu(c) @ p['ada_w'] + p['ada_b'], 6, axis=-1)
    mod_c = jnp.split(jax.nn.silu(c_ctx) @ p['ada_w'] + p['ada_b'], 6, axis=-1)
    sh_l, sc_l, ga_l, shf_l, scf_l, gf_l = [m[:, None, :] for m in mod_l]
    sh_c, sc_c, ga_c, shf_c, scf_c, gf_c = mod_c

    h_l = rmsnorm(x, p['norm1_w']) * (1.0 + sc_l) + sh_l
    h_c = rmsnorm(xc, p['norm1_w']) * (1.0 + sc_c) + sh_c
    u = jnp.concatenate([h_c, h_l], axis=1) @ p['w_in']
    cols_c = jnp.split(u[:, :Lc], COL_OFFSETS, axis=-1)
    cols_l = jnp.split(u[:, Lc:], COL_OFFSETS, axis=-1)

    a_c, a_l = gla_mixer(cols_c[0:5], cols_l[0:5], p, update_ctx)
    b_c, b_l = swa_mixer(cols_c[5:8], cols_l[5:8], p, rope, update_ctx)
    y_c, y_l = hyena_mixer(cols_c[8], cols_l[8], p, update_ctx)
    d_c, d_l = diff_mixer(cols_c[9:12], cols_l[9:12], p, rope, lam_init, update_ctx)

    x = x + ga_l * (jnp.concatenate([a_l, b_l, y_l, d_l], axis=-1) @ p['w_out'])
    f_l = rmsnorm(x, p['norm2_w']) * (1.0 + scf_l) + shf_l
    if update_ctx:
        xc = xc + ga_c * (jnp.concatenate([a_c, b_c, y_c, d_c], axis=-1) @ p['w_out'])
        f_c = rmsnorm(xc, p['norm2_w']) * (1.0 + scf_c) + shf_c
        f = moe(jnp.concatenate([f_c, f_l], axis=1), router_w, router_bias,
                p['expert_w_gate'], p['expert_w_up'], p['expert_w_down'])
        xc = xc + gf_c * f[:, :Lc]
        x = x + gf_l * f[:, Lc:]
    else:
        x = x + gf_l * moe(f_l, router_w, router_bias, p['expert_w_gate'], p['expert_w_up'], p['expert_w_down'])
    return x, xc


def setup_inputs(seed: int = 0) -> dict:
    key = jax.random.key(seed)
    ks = iter(jax.random.split(key, 48))
    nrm = lambda shape, scale: jax.random.normal(next(ks), shape, F32) * scale
    gain = lambda shape: 1.0 + nrm(shape, 0.02)
    D = D_MODEL
    return {
        "x": nrm((BATCH, SEQ, D), 1.0),
        "c": nrm((BATCH, D), 1.0),
        "ctx": nrm((BATCH, CTX_LEN, D), 1.0),
        "c_ctx": nrm((D,), 1.0),
        "norm1_w": gain((DEPTH, D)),
        "norm2_w": gain((DEPTH, D)),
        "ada_w": nrm((DEPTH, D, 6 * D), 0.5 * D ** -0.5),
        "ada_b": nrm((DEPTH, 6 * D), 0.01),
        "w_in": nrm((DEPTH, D, D_IN), D ** -0.5),
        "w_out": nrm((DEPTH, D_MIX, D), D_MIX ** -0.5),
        "gla_gate_w": nrm((DEPTH, 2, GLA_GATE_RANK, GLA_HEADS * GLA_DK), GLA_GATE_RANK ** -0.5),
        "gla_gate_b": nrm((DEPTH, 2, GLA_HEADS * GLA_DK), 0.1),
        "gla_norm_w": gain((DEPTH, GLA_DV)),
        "swa_q_norm_w": gain((DEPTH, HEAD_DIM)),
        "swa_k_norm_w": gain((DEPTH, HEAD_DIM)),
        "swa_sink": nrm((DEPTH, SWA_HEADS), 0.5),
        "hyena_conv_w": nrm((DEPTH, 3, (HYENA_ORDER + 1) * HYENA_CH), 3 ** -0.5),
        "hyena_conv_b": nrm((DEPTH, (HYENA_ORDER + 1) * HYENA_CH), 0.01),
        "hyena_ffn_w1": nrm((DEPTH, HYENA_EMB, HYENA_HIDDEN), HYENA_EMB ** -0.5),
        "hyena_ffn_b1": nrm((DEPTH, HYENA_HIDDEN), 0.1),
        "hyena_ffn_w2": nrm((DEPTH, HYENA_HIDDEN, HYENA_HIDDEN), HYENA_HIDDEN ** -0.5),
        "hyena_ffn_b2": nrm((DEPTH, HYENA_HIDDEN), 0.1),
        "hyena_ffn_w3": nrm((DEPTH, HYENA_HIDDEN, HYENA_ORDER * 2 * HYENA_CH), HYENA_HIDDEN ** -0.5),
        "hyena_ffn_freq": 1.0 + nrm((DEPTH, 2, HYENA_HIDDEN), 0.1),
        "hyena_bias": nrm((DEPTH, HYENA_ORDER, HYENA_CH), 1.0),
        "diff_q_norm_w": gain((DEPTH, DIFF_QK_DIM)),
        "diff_k_norm_w": gain((DEPTH, DIFF_QK_DIM)),
        "diff_lambda": nrm((DEPTH, 4, DIFF_QK_DIM), 0.1),
        "diff_subln_w": gain((DEPTH, DIFF_V_DIM)),
        "router_w": nrm((D, N_EXPERTS), D ** -0.5),
        "router_bias": nrm((N_EXPERTS,), 0.01),
        "expert_w_gate": nrm((DEPTH, N_EXPERTS, D, D_EXPERT), D ** -0.5),
        "expert_w_up": nrm((DEPTH, N_EXPERTS, D, D_EXPERT), D ** -0.5),
        "expert_w_down": nrm((DEPTH, N_EXPERTS, D_EXPERT, D), D_EXPERT ** -0.5),
    }


def reference(x, c, ctx, c_ctx, norm1_w, norm2_w, ada_w, ada_b, w_in, w_out, gla_gate_w, gla_gate_b, gla_norm_w,
              swa_q_norm_w, swa_k_norm_w, swa_sink, hyena_conv_w, hyena_conv_b, hyena_ffn_w1, hyena_ffn_b1,
              hyena_ffn_w2, hyena_ffn_b2, hyena_ffn_w3, hyena_ffn_freq, hyena_bias, diff_q_norm_w, diff_k_norm_w,
              diff_lambda, diff_subln_w, router_w, router_bias, expert_w_gate, expert_w_up, expert_w_down):
    S = x.shape[1]
    rows = S // GRID_W
    row_ids = jnp.repeat(jnp.arange(rows), GRID_W).astype(F32)
    col_ids = jnp.tile(jnp.arange(GRID_W), rows).astype(F32)
    rope = rope_tables(row_ids, col_ids)
    xc = ctx
    for l in range(DEPTH):
        p = dict(norm1_w=norm1_w[l], norm2_w=norm2_w[l], ada_w=ada_w[l], ada_b=ada_b[l], w_in=w_in[l],
                 w_out=w_out[l], gla_gate_w=gla_gate_w[l], gla_gate_b=gla_gate_b[l], gla_norm_w=gla_norm_w[l],
                 swa_q_norm_w=swa_q_norm_w[l], swa_k_norm_w=swa_k_norm_w[l], swa_sink=swa_sink[l],
                 hyena_conv_w=hyena_conv_w[l], hyena_conv_b=hyena_conv_b[l], hyena_ffn_w1=hyena_ffn_w1[l],
                 hyena_ffn_b1=hyena_ffn_b1[l], hyena_ffn_w2=hyena_ffn_w2[l], hyena_ffn_b2=hyena_ffn_b2[l],
                 hyena_ffn_w3=hyena_ffn_w3[l], hyena_ffn_freq=hyena_ffn_freq[l], hyena_bias=hyena_bias[l],
                 diff_q_norm_w=diff_q_norm_w[l], diff_k_norm_w=diff_k_norm_w[l], diff_lambda=diff_lambda[l],
                 diff_subln_w=diff_subln_w[l], expert_w_gate=expert_w_gate[l], expert_w_up=expert_w_up[l],
                 expert_w_down=expert_w_down[l])
        lam_init = 0.8 - 0.6 * math.exp(-0.3 * l)
        x, xc = hybrid_layer(x, xc, c, c_ctx, p, router_w, router_bias, rope, lam_init, l < DEPTH - 1)
    return x
```

```python
import functools
import math

import numpy as np
import jax
import jax.numpy as jnp
from jax import lax
from jax.experimental import pallas as pl
from jax.experimental.pallas import tpu as pltpu

F32 = jnp.float32
BF16 = jnp.bfloat16

D = 2048
B = 8
S = 2048
LC = 256
L = LC + S
DEPTH = 2
GRID_W = 64
HEAD_DIM = 64
ROPE_THETA = 10000.0
EPS = 1e-6
GROUP_W = 512

GLA_DV = 64
GLA_DK = 32
GLA_HEADS = 8
GLA_RANK = 16
GLA_TEMP = 16.0
GLA_CHUNK = 16
SWA_HEADS = 8
SWA_KV = 2
SWA_WINDOW = 128
HY_CH = 512
HY_ORDER = 2
HY_BANDS = 16
HY_TARGET = 1e-2
HY_MIN_DECAY = math.log(1.0 / HY_TARGET) / 1.5
HY_MAX_DECAY = math.log(1.0 / HY_TARGET) / 0.3
DIFF_HEADS = 4
N_EXPERTS = 16
N_GROUPS = 4
EPG = 4
TOP_K = 2
D_EXPERT = 1024

REF_COLS = dict(gla_q=(0, 256), gla_k=(256, 256), gla_v=(512, 512), gla_g=(1024, 512), gla_a=(1536, 32),
                swa_q=(1568, 512), swa_k=(2080, 128), swa_v=(2208, 128), hy=(2336, 1536),
                diff_q=(3872, 512), diff_k=(4384, 512), diff_v=(4896, 512))
U_ORDER = ("hy", "gla_v", "gla_g", "swa_q", "diff_q", "diff_k", "diff_v", "gla_q", "gla_k", "swa_k", "swa_v", "gla_a")
U_OFF = {}
_o = 0
for _n in U_ORDER:
    U_OFF[_n] = _o
    _o += max(REF_COLS[_n][1], 128)
U_USED = _o
TN_IN = 512
U_W = -(-U_USED // TN_IN) * TN_IN

VMEM_LIMIT = 56 * 1024 * 1024


def _cparams(sem):
    return pltpu.CompilerParams(dimension_semantics=sem, vmem_limit_bytes=VMEM_LIMIT)


def _ada_kernel(c_ref, w_ref, b_ref, o_ref):
    c = c_ref[...]
    a = c / (1.0 + jnp.exp(-c))
    o_ref[0] = jnp.dot(a, w_ref[0], preferred_element_type=F32, precision=lax.Precision.HIGHEST) + b_ref[0]


def ada_mod(cc, ada_w, ada_b):
    tn = 1024
    return pl.pallas_call(
        _ada_kernel,
        grid=(DEPTH, 6 * D // tn),
        in_specs=[pl.BlockSpec((16, D), lambda l, j: (0, 0)),
                  pl.BlockSpec((1, D, tn), lambda l, j: (l, 0, j)),
                  pl.BlockSpec((1, 1, tn), lambda l, j: (l, 0, j))],
        out_specs=pl.BlockSpec((1, 16, tn), lambda l, j: (l, 0, j)),
        out_shape=jax.ShapeDtypeStruct((DEPTH, 16, 6 * D), F32),
        compiler_params=_cparams(("parallel", "parallel")),
    )(cc, ada_w, ada_b.reshape(DEPTH, 1, 6 * D))


TM_IN = 768


def _inproj_kernel(x_ref, nw_ref, scl_ref, shl_ref, scc_ref, shc_ref, w_ref, o_ref, h_ref):
    i = pl.program_id(1)

    @pl.when(pl.program_id(2) == 0)
    def _():
        x = x_ref[0]
        ms = jnp.mean(x * x, axis=-1, keepdims=True)
        y = x * lax.rsqrt(ms + EPS) * nw_ref[...]
        row = i * TM_IN + lax.broadcasted_iota(jnp.int32, (TM_IN, 1), 0)
        is_ctx = row < LC
        sc = jnp.where(is_ctx, scc_ref[...], scl_ref[0])
        sh = jnp.where(is_ctx, shc_ref[...], shl_ref[0])
        h_ref[...] = (y * (1.0 + sc) + sh).astype(BF16)

    o_ref[0] = jnp.dot(h_ref[...], w_ref[...], preferred_element_type=F32)


def in_proj(xall, nw, ml, mc, w_in_p):
    return pl.pallas_call(
        _inproj_kernel,
        grid=(B, L // TM_IN, U_W // TN_IN),
        in_specs=[pl.BlockSpec((1, TM_IN, D), lambda b, i, j: (b, i, 0)),
                  pl.BlockSpec((1, D), lambda b, i, j: (0, 0)),
                  pl.BlockSpec((1, 1, D), lambda b, i, j: (b, 0, 1)),
                  pl.BlockSpec((1, 1, D), lambda b, i, j: (b, 0, 0)),
                  pl.BlockSpec((1, D), lambda b, i, j: (0, 1)),
                  pl.BlockSpec((1, D), lambda b, i, j: (0, 0)),
                  pl.BlockSpec((D, TN_IN), lambda b, i, j: (0, j))],
        out_specs=pl.BlockSpec((1, TM_IN, TN_IN), lambda b, i, j: (b, i, j)),
        out_shape=jax.ShapeDtypeStruct((B, L, U_W), F32),
        scratch_shapes=[pltpu.VMEM((TM_IN, D), BF16)],
        compiler_params=_cparams(("parallel", "parallel", "arbitrary")),
    )(xall, nw, ml, ml, mc, mc, w_in_p)


def _norm_rope(x, w, cos, sin):
    R, W = x.shape
    lane = lax.broadcasted_iota(jnp.int32, (R, 128), 1)
    left = lane < HEAD_DIM
    first = (lane & 31) < 16
    outs = []
    for c in range(W // 128):
        xc = x[:, c * 128:(c + 1) * 128]
        sq = xc * xc
        sl = jnp.sum(jnp.where(left, sq, 0.0), axis=-1, keepdims=True)
        sr = jnp.sum(jnp.where(left, 0.0, sq), axis=-1, keepdims=True)
        inv = jnp.where(left, lax.rsqrt(sl * (1.0 / HEAD_DIM) + EPS), lax.rsqrt(sr * (1.0 / HEAD_DIM) + EPS))
        y = xc * inv * w
        rot = jnp.where(first, pltpu.roll(y, 112, 1), pltpu.roll(y, 16, 1))
        outs.append(y * cos + rot * sin)
    return outs[0] if len(outs) == 1 else jnp.concatenate(outs, axis=-1)


def rope_tables128():
    n = HEAD_DIM // 4
    rows = S // GRID_W
    row_ids = jnp.repeat(jnp.arange(rows), GRID_W).astype(F32)
    col_ids = jnp.tile(jnp.arange(GRID_W), rows).astype(F32)
    inv = ROPE_THETA ** (-jnp.arange(n, dtype=F32) / n)
    ar = row_ids[:, None] * inv
    ac = col_ids[:, None] * inv
    cos64 = jnp.concatenate([jnp.cos(ar), jnp.cos(ar), jnp.cos(ac), jnp.cos(ac)], axis=-1)
    sin64 = jnp.concatenate([-jnp.sin(ar), jnp.sin(ar), -jnp.sin(ac), jnp.sin(ac)], axis=-1)
    cos = jnp.concatenate([jnp.ones((LC, 64), F32), cos64], axis=0)
    sin = jnp.concatenate([jnp.zeros((LC, 64), F32), sin64], axis=0)
    return jnp.tile(cos, (1, 2)), jnp.tile(sin, (1, 2))


SWA_BAND = 3 * SWA_WINDOW


def _swa_kernel(sink_ref, q_ref, k_ref, v_ref, qw_ref, kw_ref, cos_ref, sin_ref, o_ref, kn_ref, vb_ref, *, n_off):
    n = pl.program_id(1) + n_off

    @pl.when(pl.program_id(1) == 0)
    def _():
        kn_ref[...] = _norm_rope(k_ref[0], kw_ref[...], cos_ref[...], sin_ref[...]).astype(BF16)
        vb_ref[...] = v_ref[0].astype(BF16)

    r0 = pl.multiple_of(n * 128, 128)
    q = _norm_rope(q_ref[0], qw_ref[...], cos_ref[pl.ds(r0, 128), :], sin_ref[pl.ds(r0, 128), :])
    q = (q * (HEAD_DIM ** -0.5)).astype(BF16)

    nb = n - LC // 128
    is_lat = nb >= 0
    lstart = jnp.clip((nb - 1) * 128, 0, S - SWA_BAND)
    start = pl.multiple_of(LC + lstart, 128)
    kk = jnp.concatenate([kn_ref[0:LC, :], kn_ref[pl.ds(start, SWA_BAND), :]], axis=0)
    vv = jnp.concatenate([vb_ref[0:LC, :], vb_ref[pl.ds(start, SWA_BAND), :]], axis=0)
    nk = LC + SWA_BAND
    row = lax.broadcasted_iota(jnp.int32, (512, nk), 0) & 127
    col = lax.broadcasted_iota(jnp.int32, (512, nk), 1)
    qpos = nb * 128 + row
    kpos = lstart + col - LC
    valid = (col < LC) | ((jnp.abs(qpos - kpos) <= SWA_WINDOW) & is_lat)
    rowi = lax.broadcasted_iota(jnp.int32, (512, 1), 0)
    outs = [None] * SWA_HEADS
    for hk in range(SWA_KV):
        qs = jnp.concatenate([q[:, (hk * 4 + g) * 64:(hk * 4 + g + 1) * 64] for g in range(4)], axis=0)
        s = lax.dot_general(qs, kk[:, hk * 64:(hk + 1) * 64], (((1,), (1,)), ((), ())),
                            preferred_element_type=F32)
        s = jnp.where(valid, s, -jnp.inf)
        sk = jnp.where(rowi < 128, sink_ref[hk * 4],
                       jnp.where(rowi < 256, sink_ref[hk * 4 + 1],
                                 jnp.where(rowi < 384, sink_ref[hk * 4 + 2], sink_ref[hk * 4 + 3])))
        m = jnp.maximum(jnp.max(s, axis=-1, keepdims=True), sk)
        e = jnp.exp(s - m)
        den = jnp.sum(e, axis=-1, keepdims=True) + jnp.exp(sk - m)
        p = (e / den).astype(BF16)
        o = jnp.dot(p, vv[:, hk * 64:(hk + 1) * 64], preferred_element_type=F32)
        for g in range(4):
            outs[hk * 4 + g] = o[g * 128:(g + 1) * 128]
    o_ref[0] = jnp.concatenate(outs, axis=-1).astype(o_ref.dtype)


def swa_mixer(u, sink, qw, kw, cos, sin, with_ctx):
    n_off = 0 if with_ctx else LC // 128
    nblk = L // 128 - n_off
    qc, kc, vc = U_OFF["swa_q"] // 512, U_OFF["swa_k"] // 128, U_OFF["swa_v"] // 128
    return pl.pallas_call(
        functools.partial(_swa_kernel, n_off=n_off),
        grid_spec=pltpu.PrefetchScalarGridSpec(
            num_scalar_prefetch=1,
            grid=(B, nblk),
            in_specs=[pl.BlockSpec((1, 128, 512), lambda b, n, s: (b, n + n_off, qc)),
                      pl.BlockSpec((1, L, 128), lambda b, n, s: (b, 0, kc)),
                      pl.BlockSpec((1, L, 128), lambda b, n, s: (b, 0, vc)),
                      pl.BlockSpec((1, 128), lambda b, n, s: (0, 0)),
                      pl.BlockSpec((1, 128), lambda b, n, s: (0, 0)),
                      pl.BlockSpec((L, 128), lambda b, n, s: (0, 0)),
                      pl.BlockSpec((L, 128), lambda b, n, s: (0, 0))],
            out_specs=pl.BlockSpec((1, 128, 512), lambda b, n, s: (b, n, 0)),
            scratch_shapes=[pltpu.VMEM((L, 128), BF16), pltpu.VMEM((L, 128), BF16)]),
        out_shape=jax.ShapeDtypeStruct((B, nblk * 128, GROUP_W), BF16),
        compiler_params=_cparams(("parallel", "arbitrary")),
    )(sink, u, u, u, jnp.tile(qw, 2)[None], jnp.tile(kw, 2)[None], cos, sin)


TQ_DIFF = 256


def _diff_kernel(q_ref, k_ref, v_ref, qw_ref, kw_ref, cos_ref, sin_ref, dl_ref, sw_ref, o_ref, kn_ref, vb_ref,
                 *, j_off, lam_init):
    j = pl.program_id(2) + j_off

    @pl.when(pl.program_id(2) == 0)
    def _():
        kn_ref[...] = _norm_rope(k_ref[0], kw_ref[...], cos_ref[...], sin_ref[...]).astype(BF16)
        vb_ref[...] = v_ref[0].astype(BF16)

    dl = dl_ref[...]
    lam = (jnp.exp(jnp.sum(dl[0:1] * dl[1:2], axis=-1, keepdims=True))
           - jnp.exp(jnp.sum(dl[2:3] * dl[3:4], axis=-1, keepdims=True)) + lam_init)
    r0 = pl.multiple_of(j * TQ_DIFF, TQ_DIFF)
    q = _norm_rope(q_ref[0], qw_ref[...], cos_ref[pl.ds(r0, TQ_DIFF), :], sin_ref[pl.ds(r0, TQ_DIFF), :])
    q = (q * (HEAD_DIM ** -0.5)).astype(BF16)
    col = lax.broadcasted_iota(jnp.int32, (TQ_DIFF, L), 1)
    valid = (col < LC) | (j > 0)
    kn = kn_ref[...]
    ps = []
    for m in range(2):
        s = lax.dot_general(q[:, m * 64:(m + 1) * 64], kn[:, m * 64:(m + 1) * 64], (((1,), (1,)), ((), ())),
                            preferred_element_type=F32)
        s = jnp.where(valid, s, -jnp.inf)
        e = jnp.exp(s - jnp.max(s, axis=-1, keepdims=True))
        ps.append(e / jnp.sum(e, axis=-1, keepdims=True))
    wgt = (ps[0] - lam * ps[1]).astype(BF16)
    o = jnp.dot(wgt, vb_ref[...], preferred_element_type=F32)
    ms = jnp.mean(o * o, axis=-1, keepdims=True)
    o = o * lax.rsqrt(ms + EPS) * sw_ref[...] * (1.0 - lam_init)
    o_ref[0] = o.astype(o_ref.dtype)


def diff_mixer(u, qw, kw, cos, sin, dlam, subw, lam_init, with_ctx):
    j_off = 0 if with_ctx else LC // TQ_DIFF
    nblk = L // TQ_DIFF - j_off
    qc, kc, vc = U_OFF["diff_q"] // 128, U_OFF["diff_k"] // 128, U_OFF["diff_v"] // 128
    return pl.pallas_call(
        functools.partial(_diff_kernel, j_off=j_off, lam_init=lam_init),
        grid=(B, DIFF_HEADS, nblk),
        in_specs=[pl.BlockSpec((1, TQ_DIFF, 128), lambda b, h, j: (b, j + j_off, qc + h)),
                  pl.BlockSpec((1, L, 128), lambda b, h, j: (b, 0, kc + h)),
                  pl.BlockSpec((1, L, 128), lambda b, h, j: (b, 0, vc + h)),
                  pl.BlockSpec((1, 128), lambda b, h, j: (0, 0)),
                  pl.BlockSpec((1, 128), lambda b, h, j: (0, 0)),
                  pl.BlockSpec((L, 128), lambda b, h, j: (0, 0)),
                  pl.BlockSpec((L, 128), lambda b, h, j: (0, 0)),
                  pl.BlockSpec((4, 64), lambda b, h, j: (0, 0)),
                  pl.BlockSpec((1, 128), lambda b, h, j: (0, 0))],
        out_specs=pl.BlockSpec((1, TQ_DIFF, 128), lambda b, h, j: (b, j, h)),
        out_shape=jax.ShapeDtypeStruct((B, nblk * TQ_DIFF, GROUP_W), BF16),
        scratch_shapes=[pltpu.VMEM((L, 128), BF16), pltpu.VMEM((L, 128), BF16)],
        compiler_params=_cparams(("parallel", "parallel", "arbitrary")),
    )(u, u, u, jnp.tile(qw, 2)[None], jnp.tile(kw, 2)[None], cos, sin, dlam, subw[None])


TM_OUT = 256


def _outproj_kernel(x_ref, a_ref, b_ref, y_ref, d_ref, w_ref, nw_ref, ml_ref, mc_ref, rw_ref,
                    xo_ref, f_ref, lg_ref, *, i_off):
    is_ctx = (pl.program_id(1) + i_off) == 0

    def mod(k):
        return jnp.where(is_ctx, mc_ref[:, k * D:(k + 1) * D], ml_ref[0, :, k * D:(k + 1) * D])

    acc = jnp.dot(a_ref[0], w_ref[0:512, :], preferred_element_type=F32)
    acc += jnp.dot(b_ref[0], w_ref[512:1024, :], preferred_element_type=F32)
    acc += jnp.dot(y_ref[0], w_ref[1024:1536, :], preferred_element_type=F32)
    acc += jnp.dot(d_ref[0], w_ref[1536:2048, :], preferred_element_type=F32)
    xn = x_ref[0] + mod(2) * acc
    xo_ref[0] = xn
    ms = jnp.mean(xn * xn, axis=-1, keepdims=True)
    f = (xn * lax.rsqrt(ms + EPS) * nw_ref[...]) * (1.0 + mod(4)) + mod(3)
    f_ref[0] = f
    lg_ref[0] = jnp.dot(f, rw_ref[...], preferred_element_type=F32, precision=lax.Precision.HIGHEST)


def out_proj(xall, mixes, w_out_b, nw, ml, mc, rw_pad, with_ctx):
    i_off = 0 if with_ctx else LC // TM_OUT
    nblk = L // TM_OUT - i_off
    rows = lambda b, i: (b, i + i_off, 0)
    return pl.pallas_call(
        functools.partial(_outproj_kernel, i_off=i_off),
        grid=(B, nblk),
        in_specs=[pl.BlockSpec((1, TM_OUT, D), rows)]
                 + [pl.BlockSpec((1, TM_OUT, GROUP_W), lambda b, i: (b, i, 0))] * 4
                 + [pl.BlockSpec((D, D), lambda b, i: (0, 0)),
                    pl.BlockSpec((1, D), lambda b, i: (0, 0)),
                    pl.BlockSpec((1, 1, 6 * D), lambda b, i: (b, 0, 0)),
                    pl.BlockSpec((1, 6 * D), lambda b, i: (0, 0)),
                    pl.BlockSpec((D, 128), lambda b, i: (0, 0))],
        out_specs=[pl.BlockSpec((1, TM_OUT, D), lambda b, i: (b, i, 0)),
                   pl.BlockSpec((1, TM_OUT, D), lambda b, i: (b, i, 0)),
                   pl.BlockSpec((1, TM_OUT, 128), lambda b, i: (b, i, 0))],
        out_shape=[jax.ShapeDtypeStruct((B, nblk * TM_OUT, D), F32),
                   jax.ShapeDtypeStruct((B, nblk * TM_OUT, D), F32),
                   jax.ShapeDtypeStruct((B, nblk * TM_OUT, 128), F32)],
        compiler_params=_cparams(("parallel", "parallel")),
    )(xall, *mixes, w_out_b, nw, ml, mc, rw_pad)


TM_MOE = 256
R_GATHER = 256


def _gather_kernel(src_ref, f_hbm, o_ref, buf, sem):
    t = pl.program_id(0)

    def issue(r, carry):
        idx = src_ref[t * R_GATHER + r]
        pltpu.make_async_copy(f_hbm.at[pl.ds(idx, 1)], buf.at[pl.ds(r, 1)], sem).start()
        return carry

    lax.fori_loop(0, R_GATHER, issue, 0)

    def drain(r, carry):
        pltpu.make_async_copy(f_hbm.at[pl.ds(0, 1)], buf.at[pl.ds(r, 1)], sem).wait()
        return carry

    lax.fori_loop(0, R_GATHER, drain, 0)
    o_ref[...] = buf[...].astype(BF16)


def gather_rows(src, f2d, nrows):
    return pl.pallas_call(
        _gather_kernel,
        grid_spec=pltpu.PrefetchScalarGridSpec(
            num_scalar_prefetch=1,
            grid=(nrows // R_GATHER,),
            in_specs=[pl.BlockSpec(memory_space=pl.ANY)],
            out_specs=pl.BlockSpec((R_GATHER, D), lambda t, s: (t, 0)),
            scratch_shapes=[pltpu.VMEM((R_GATHER, D), F32), pltpu.SemaphoreType.DMA(())]),
        out_shape=jax.ShapeDtypeStruct((nrows, D), BF16),
        compiler_params=_cparams(("arbitrary",)),
    )(src, f2d)


def _moe_kernel(te_ref, tv_ref, x_ref, wg_ref, wu_ref, wd_ref, rw_ref, o_ref):
    @pl.when(tv_ref[pl.program_id(0)] == 0)
    def _():
        o_ref[...] = jnp.zeros_like(o_ref)

    @pl.when(tv_ref[pl.program_id(0)] > 0)
    def _():
        x = x_ref[...]
        g = jnp.dot(x, wg_ref[0], preferred_element_type=F32)
        u = jnp.dot(x, wu_ref[0], preferred_element_type=F32)
        h = (g / (1.0 + jnp.exp(-g)) * u).astype(BF16)
        y = jnp.dot(h, wd_ref[0], preferred_element_type=F32)
        o_ref[...] = y * rw_ref[...]


def moe_experts(tile_e, tile_v, xs, wg, wu, wd, roww):
    nrows = xs.shape[0]
    return pl.pallas_call(
        _moe_kernel,
        grid_spec=pltpu.PrefetchScalarGridSpec(
            num_scalar_prefetch=2,
            grid=(nrows // TM_MOE,),
            in_specs=[pl.BlockSpec((TM_MOE, D), lambda t, te, tv: (t, 0)),
                      pl.BlockSpec((1, D, D_EXPERT), lambda t, te, tv: (te[t], 0, 0)),
                      pl.BlockSpec((1, D, D_EXPERT), lambda t, te, tv: (te[t], 0, 0)),
                      pl.BlockSpec((1, D_EXPERT, D), lambda t, te, tv: (te[t], 0, 0)),
                      pl.BlockSpec((TM_MOE, 1), lambda t, te, tv: (t, 0))],
            out_specs=pl.BlockSpec((TM_MOE, D), lambda t, te, tv: (t, 0))),
        out_shape=jax.ShapeDtypeStruct((nrows, D), F32),
        compiler_params=_cparams(("arbitrary",)),
    )(tile_e, tile_v, xs, wg, wu, wd, roww)


R_COMB = 256


def _combine_kernel(pos_ref, x_ref, ml_ref, mc_ref, y_hbm, o_ref, buf0, buf1, sem, *, i_off, rows_per_b):
    b = pl.program_id(0)
    i = pl.program_id(1)
    is_ctx = (i + i_off) == 0
    base = (b * rows_per_b + i * R_COMB) * 2

    def issue(r, carry):
        p0 = pos_ref[base + 2 * r]
        p1 = pos_ref[base + 2 * r + 1]
        pltpu.make_async_copy(y_hbm.at[pl.ds(p0, 1)], buf0.at[pl.ds(r, 1)], sem.at[0]).start()
        pltpu.make_async_copy(y_hbm.at[pl.ds(p1, 1)], buf1.at[pl.ds(r, 1)], sem.at[1]).start()
        return carry

    lax.fori_loop(0, R_COMB, issue, 0)

    def drain(r, carry):
        pltpu.make_async_copy(y_hbm.at[pl.ds(0, 1)], buf0.at[pl.ds(r, 1)], sem.at[0]).wait()
        pltpu.make_async_copy(y_hbm.at[pl.ds(0, 1)], buf1.at[pl.ds(r, 1)], sem.at[1]).wait()
        return carry

    lax.fori_loop(0, R_COMB, drain, 0)
    gf = jnp.where(is_ctx, mc_ref[:, 5 * D:6 * D], ml_ref[0, :, 5 * D:6 * D])
    o_ref[0] = x_ref[0] + gf * (buf0[...] + buf1[...])


def moe_combine(pos, xall, ml, mc, y_sorted, with_ctx):
    i_off = 0 if with_ctx else LC // R_COMB
    rows_out = xall.shape[1]
    nblk = rows_out // R_COMB
    return pl.pallas_call(
        functools.partial(_combine_kernel, i_off=i_off, rows_per_b=rows_out),
        grid_spec=pltpu.PrefetchScalarGridSpec(
            num_scalar_prefetch=1,
            grid=(B, nblk),
            in_specs=[pl.BlockSpec((1, R_COMB, D), lambda b, i, p: (b, i, 0)),
                      pl.BlockSpec((1, 1, 6 * D), lambda b, i, p: (b, 0, 0)),
                      pl.BlockSpec((1, 6 * D), lambda b, i, p: (0, 0)),
                      pl.BlockSpec(memory_space=pl.ANY)],
            out_specs=pl.BlockSpec((1, R_COMB, D), lambda b, i, p: (b, i, 0)),
            scratch_shapes=[pltpu.VMEM((R_COMB, D), F32), pltpu.VMEM((R_COMB, D), F32),
                            pltpu.SemaphoreType.DMA((2,))]),
        out_shape=jax.ShapeDtypeStruct((B, rows_out, D), F32),
        compiler_params=_cparams(("arbitrary", "arbitrary")),
    )(pos, xall, ml, mc, y_sorted)


def route(logits, router_bias):
    T = logits.shape[0]
    probs = jax.nn.softmax(logits, axis=-1)
    sel = (probs + router_bias.astype(F32)).reshape(T, N_GROUPS, EPG)
    group = jnp.argmax(jnp.max(sel, axis=-1), axis=-1)
    sel_g = jnp.take_along_axis(sel, group[:, None, None], axis=1)[:, 0]
    _, idx = lax.top_k(sel_g, TOP_K)
    expert = (group[:, None] * EPG + idx).astype(jnp.int32)
    wts = jnp.take_along_axis(probs, expert, axis=-1)
    wts = wts / jnp.sum(wts, axis=-1, keepdims=True)
    flat_e = expert.reshape(-1)
    onehot = (flat_e[:, None] == jnp.arange(N_EXPERTS, dtype=jnp.int32)[None, :]).astype(jnp.int32)
    csum = jnp.cumsum(onehot, axis=0)
    rank = jnp.sum(csum * onehot, axis=-1) - 1
    counts = csum[-1]
    ptiles = (counts + TM_MOE - 1) // TM_MOE
    tend = jnp.cumsum(ptiles)
    tstart = tend - ptiles
    dest = (tstart[flat_e] * TM_MOE + rank).astype(jnp.int32)
    ntiles = -(-(TOP_K * T) // TM_MOE) + N_EXPERTS
    nrows = ntiles * TM_MOE
    tid = jnp.arange(ntiles, dtype=jnp.int32)
    tile_e = jnp.minimum(jnp.searchsorted(tend, tid, side="right"), N_EXPERTS - 1).astype(jnp.int32)
    tile_v = (tid < tend[-1]).astype(jnp.int32)
    src = jnp.zeros((nrows,), jnp.int32).at[dest].set(jnp.repeat(jnp.arange(T, dtype=jnp.int32), TOP_K))
    roww = jnp.zeros((nrows, 1), F32).at[dest, 0].set(wts.reshape(-1))
    return src, roww, dest, tile_e, tile_v


def _rmsnorm(x, w):
    xf = x.astype(F32)
    y = xf * lax.rsqrt(jnp.mean(xf * xf, axis=-1, keepdims=True) + EPS)
    return (y * w.astype(F32)).astype(x.dtype)


def _gla_scan(q, k, v, log_a, s0, with_output):
    Bq, Lq, H, dk = k.shape
    C = GLA_CHUNK
    n = Lq // C
    r = lambda t: t.reshape(Bq, n, C, H, t.shape[-1]).astype(F32)
    q, k, v, log_a = r(q), r(k), r(v), r(log_a)
    b = jnp.cumsum(log_a, axis=2)
    b_last = b[:, :, -1]
    chunk_kv = jnp.einsum('bnchd,bnchv->bnhdv', k * jnp.exp(b_last[:, :, None] - b), v)
    chunk_decay = jnp.exp(b_last)

    def step(s, inp):
        dec, kv = inp
        return dec[..., None] * s + kv, s

    s_final, s_start = lax.scan(step, s0, (jnp.moveaxis(chunk_decay, 1, 0), jnp.moveaxis(chunk_kv, 1, 0)))
    if not with_output:
        return None, s_final
    s_start = jnp.moveaxis(s_start, 0, 1)
    o_inter = jnp.einsum('bnchd,bnhdv->bnchv', q * jnp.exp(b), s_start)
    tri = jnp.tril(jnp.ones((C, C), dtype=bool))
    diff = b[:, :, :, None] - b[:, :, None, :]
    decay = jnp.exp(jnp.where(tri[None, None, :, :, None, None], diff, -jnp.inf))
    attn = jnp.einsum('bnthd,bnshd,bntshd->bnhts', q, k, decay)
    o_intra = jnp.einsum('bnhts,bnshv->bnthv', attn, v)
    return (o_inter + o_intra).reshape(Bq, Lq, H, -1), s_final


def gla_mixer_jnp(u, gate_w, gate_b, norm_w, with_ctx):
    def col(name):
        o = U_OFF[name]
        return u[:, :, o:o + REF_COLS[name][1]]

    def prep(q, k, v, g, a):
        Bq, Lq, _ = q.shape
        q = q.reshape(Bq, Lq, GLA_HEADS, GLA_DK) * (GLA_DK ** -0.5)
        k = k.reshape(Bq, Lq, GLA_HEADS, GLA_DK)
        v = v.reshape(Bq, Lq, GLA_HEADS, GLA_DV)
        z = jnp.einsum('bldr,drk->bldk', a.reshape(Bq, Lq, 2, GLA_RANK), gate_w) + gate_b
        la = (jax.nn.log_sigmoid(z.astype(F32)) / GLA_TEMP).reshape(Bq, Lq, 2, GLA_HEADS, GLA_DK)
        return q, k, v, g, la[:, :, 0], la[:, :, 1]

    cols = [col(n) for n in ("gla_q", "gla_k", "gla_v", "gla_g", "gla_a")]
    qc, kc, vc, gc, lfc, lbc = prep(*[t[:, :LC] for t in cols])
    ql, kl, vl, gl, lfl, lbl = prep(*[t[:, LC:] for t in cols])
    s0 = jnp.zeros((B, GLA_HEADS, GLA_DK, GLA_DV), F32)
    fl = lambda t: jnp.flip(t, axis=1)
    oc_f, s_f = _gla_scan(qc, kc, vc, lfc, s0, with_ctx)
    oc_b, s_b = _gla_scan(fl(qc), fl(kc), fl(vc), fl(lbc), s0, with_ctx)
    ol_f, _ = _gla_scan(ql, kl, vl, lfl, s_f, True)
    ol_b, _ = _gla_scan(fl(ql), fl(kl), fl(vl), fl(lbl), s_b, True)

    def finish(o, g):
        o = _rmsnorm(o.astype(g.dtype), norm_w)
        return o.reshape(g.shape) * jax.nn.silu(g)

    out_l = finish(ol_f + fl(ol_b), gl)
    if not with_ctx:
        return out_l.astype(BF16)
    return jnp.concatenate([finish(oc_f + fl(oc_b), gc), out_l], axis=1).astype(BF16)


def _hyena_spectrum(Lh, w1, b1, w2, b2, w3, fr):
    t = jnp.linspace(0.0, 1.0, Lh, dtype=F32)[:, None]
    w = (2.0 * math.pi / Lh) * jnp.arange(Lh, dtype=F32)[:, None]
    bands = jnp.linspace(1e-4, HY_BANDS - 1, HY_BANDS, dtype=F32)
    z = jnp.concatenate([t, jnp.cos(w * bands), -jnp.sin(w * bands)], axis=-1)
    h = jnp.sin(fr[0] * (z @ w1 + b1))
    h = jnp.sin(fr[1] * (h @ w2 + b2))
    h = (h @ w3).reshape(Lh, HY_ORDER, 2, HY_CH)
    deltas = jnp.linspace(HY_MIN_DECAY, HY_MAX_DECAY, HY_CH, dtype=F32)
    h = h * jnp.exp(-t * deltas)[:, None, None, :]
    h = h / jnp.sum(jnp.abs(h), axis=(0, 2), keepdims=True)
    full = jnp.concatenate([h[:, :, 0], jnp.zeros((1, HY_ORDER, HY_CH), F32), jnp.flip(h[1:, :, 1], axis=0)], axis=0)
    return jnp.fft.rfft(full, axis=0)


def hyena_mixer_jnp(u, conv_w, conv_b, w1, b1, w2, b2, w3, fr, bias, with_ctx):
    o = U_OFF["hy"]
    uh = u[:, :, o:o + 3 * HY_CH]

    def run(uu):
        Lh = uu.shape[1]
        up = jnp.pad(uu, ((0, 0), (1, 1), (0, 0)))
        z = up[:, :-2] * conv_w[0] + up[:, 1:-1] * conv_w[1] + up[:, 2:] * conv_w[2] + conv_b
        x1, x2, y = jnp.split(z, 3, axis=-1)
        spec = _hyena_spectrum(Lh, w1, b1, w2, b2, w3, fr)
        for oo, gate in enumerate((x1, x2)):
            U = jnp.fft.rfft(y.astype(F32), n=2 * Lh, axis=1)
            cv = jnp.fft.irfft(U * spec[None, :, oo], n=2 * Lh, axis=1)[:, :Lh]
            y = gate * (cv + y * bias[oo])
        return y

    out_l = run(uh[:, LC:])
    if not with_ctx:
        return out_l.astype(BF16)
    return jnp.concatenate([run(uh[:, :LC]), out_l], axis=1).astype(BF16)


def _permute_w_in(w):
    parts = []
    for n in U_ORDER:
        o, wd = REF_COLS[n]
        p = w[:, o:o + wd]
        if wd < 128:
            p = jnp.pad(p, ((0, 0), (0, 128 - wd)))
        parts.append(p)
    parts.append(jnp.zeros((w.shape[0], U_W - U_USED), w.dtype))
    return jnp.concatenate(parts, axis=1).astype(BF16)


def kernel(x, c, ctx, c_ctx, norm1_w, norm2_w, ada_w, ada_b, w_in, w_out, gla_gate_w, gla_gate_b, gla_norm_w,
           swa_q_norm_w, swa_k_norm_w, swa_sink, hyena_conv_w, hyena_conv_b, hyena_ffn_w1, hyena_ffn_b1,
           hyena_ffn_w2, hyena_ffn_b2, hyena_ffn_w3, hyena_ffn_freq, hyena_bias, diff_q_norm_w, diff_k_norm_w,
           diff_lambda, diff_subln_w, router_w, router_bias, expert_w_gate, expert_w_up, expert_w_down):
    assert x.shape == (B, S, D) and ctx.shape == (B, LC, D)
    cc = jnp.zeros((16, D), F32).at[:B].set(c).at[B].set(c_ctx)
    mods = ada_mod(cc, ada_w, ada_b)
    cos, sin = rope_tables128()
    rw_pad = jnp.pad(router_w, ((0, 0), (0, 128 - N_EXPERTS)))
    xall = jnp.concatenate([ctx, x], axis=1)

    for l in range(DEPTH):
        with_ctx = l < DEPTH - 1
        lam_init = 0.8 - 0.6 * math.exp(-0.3 * l)
        ml = mods[l, :B].reshape(B, 1, 6 * D)
        mc = mods[l, B:B + 1]
        u = in_proj(xall, norm1_w[l][None], ml, mc, _permute_w_in(w_in[l]))
        mix_a = gla_mixer_jnp(u, gla_gate_w[l], gla_gate_b[l], gla_norm_w[l], with_ctx)
        mix_b = swa_mixer(u, swa_sink[l], swa_q_norm_w[l], swa_k_norm_w[l], cos, sin, with_ctx)
        mix_y = hyena_mixer_jnp(u, hyena_conv_w[l], hyena_conv_b[l], hyena_ffn_w1[l], hyena_ffn_b1[l],
                                hyena_ffn_w2[l], hyena_ffn_b2[l], hyena_ffn_w3[l], hyena_ffn_freq[l],
                                hyena_bias[l], with_ctx)
        mix_d = diff_mixer(u, diff_q_norm_w[l], diff_k_norm_w[l], cos, sin, diff_lambda[l], diff_subln_w[l],
                           lam_init, with_ctx)
        xall, f, logits = out_proj(xall, (mix_a, mix_b, mix_y, mix_d), w_out[l].astype(BF16), norm2_w[l][None],
                                   ml, mc, rw_pad, with_ctx)
        T = B * f.shape[1]
        src, roww, dest, tile_e, tile_v = route(logits.reshape(T, 128)[:, :N_EXPERTS], router_bias)
        xs = gather_rows(src, f.reshape(T, D), src.shape[0])
        ys = moe_experts(tile_e, tile_v, xs, expert_w_gate[l].astype(BF16), expert_w_up[l].astype(BF16),
                         expert_w_down[l].astype(BF16), roww)
        xall = moe_combine(dest, xall, ml, mc, ys, with_ctx)
    return xall
```

```python
import functools
import math

import numpy as np
import jax
import jax.numpy as jnp
from jax import lax
from jax.experimental import pallas as pl
from jax.experimental.pallas import tpu as pltpu

F32 = jnp.float32
BF16 = jnp.bfloat16

D = 2048
B = 8
S = 2048
LC = 256
L = LC + S
DEPTH = 2
GRID_W = 64
HEAD_DIM = 64
ROPE_THETA = 10000.0
EPS = 1e-6
GROUP_W = 512

GLA_DV = 64
GLA_DK = 32
GLA_HEADS = 8
GLA_RANK = 16
GLA_TEMP = 16.0
GLA_CHUNK = 16
SWA_HEADS = 8
SWA_KV = 2
SWA_WINDOW = 128
HY_CH = 512
HY_ORDER = 2
HY_BANDS = 16
HY_TARGET = 1e-2
HY_MIN_DECAY = math.log(1.0 / HY_TARGET) / 1.5
HY_MAX_DECAY = math.log(1.0 / HY_TARGET) / 0.3
DIFF_HEADS = 4
N_EXPERTS = 16
N_GROUPS = 4
EPG = 4
TOP_K = 2
D_EXPERT = 1024

REF_COLS = dict(gla_q=(0, 256), gla_k=(256, 256), gla_v=(512, 512), gla_g=(1024, 512), gla_a=(1536, 32),
                swa_q=(1568, 512), swa_k=(2080, 128), swa_v=(2208, 128), hy=(2336, 1536),
                diff_q=(3872, 512), diff_k=(4384, 512), diff_v=(4896, 512))
U_ORDER = ("hy", "gla_v", "gla_g", "swa_q", "diff_q", "diff_k", "diff_v", "gla_q", "gla_k", "swa_k", "swa_v", "gla_a")
U_OFF = {}
_o = 0
for _n in U_ORDER:
    U_OFF[_n] = _o
    _o += max(REF_COLS[_n][1], 128)
U_USED = _o
TN_IN = 512
U_W = -(-U_USED // TN_IN) * TN_IN

VMEM_LIMIT = 56 * 1024 * 1024


def _cparams(sem):
    return pltpu.CompilerParams(dimension_semantics=sem, vmem_limit_bytes=VMEM_LIMIT)


def _ada_kernel(c_ref, w_ref, b_ref, o_ref):
    c = c_ref[...]
    a = c / (1.0 + jnp.exp(-c))
    o_ref[0] = jnp.dot(a, w_ref[0], preferred_element_type=F32, precision=lax.Precision.HIGHEST) + b_ref[0]


def ada_mod(cc, ada_w, ada_b):
    tn = 1024
    return pl.pallas_call(
        _ada_kernel,
        grid=(DEPTH, 6 * D // tn),
        in_specs=[pl.BlockSpec((16, D), lambda l, j: (0, 0)),
                  pl.BlockSpec((1, D, tn), lambda l, j: (l, 0, j)),
                  pl.BlockSpec((1, 1, tn), lambda l, j: (l, 0, j))],
        out_specs=pl.BlockSpec((1, 16, tn), lambda l, j: (l, 0, j)),
        out_shape=jax.ShapeDtypeStruct((DEPTH, 16, 6 * D), F32),
        compiler_params=_cparams(("parallel", "parallel")),
    )(cc, ada_w, ada_b.reshape(DEPTH, 1, 6 * D))


TM_IN = 768


def _inproj_kernel(x_ref, nw_ref, scl_ref, shl_ref, scc_ref, shc_ref, w_ref, o_ref, h_ref):
    i = pl.program_id(1)

    @pl.when(pl.program_id(2) == 0)
    def _():
        x = x_ref[0]
        ms = jnp.mean(x * x, axis=-1, keepdims=True)
        y = x * lax.rsqrt(ms + EPS) * nw_ref[...]
        row = i * TM_IN + lax.broadcasted_iota(jnp.int32, (TM_IN, 1), 0)
        is_ctx = row < LC
        sc = jnp.where(is_ctx, scc_ref[...], scl_ref[0])
        sh = jnp.where(is_ctx, shc_ref[...], shl_ref[0])
        h_ref[...] = (y * (1.0 + sc) + sh).astype(BF16)

    o_ref[0] = jnp.dot(h_ref[...], w_ref[...], preferred_element_type=F32)


def in_proj(xall, nw, ml, mc, w_in_p):
    return pl.pallas_call(
        _inproj_kernel,
        grid=(B, L // TM_IN, U_W // TN_IN),
        in_specs=[pl.BlockSpec((1, TM_IN, D), lambda b, i, j: (b, i, 0)),
                  pl.BlockSpec((1, D), lambda b, i, j: (0, 0)),
                  pl.BlockSpec((1, 1, D), lambda b, i, j: (b, 0, 1)),
                  pl.BlockSpec((1, 1, D), lambda b, i, j: (b, 0, 0)),
                  pl.BlockSpec((1, D), lambda b, i, j: (0, 1)),
                  pl.BlockSpec((1, D), lambda b, i, j: (0, 0)),
                  pl.BlockSpec((D, TN_IN), lambda b, i, j: (0, j))],
        out_specs=pl.BlockSpec((1, TM_IN, TN_IN), lambda b, i, j: (b, i, j)),
        out_shape=jax.ShapeDtypeStruct((B, L, U_W), F32),
        scratch_shapes=[pltpu.VMEM((TM_IN, D), BF16)],
        compiler_params=_cparams(("parallel", "parallel", "arbitrary")),
    )(xall, nw, ml, ml, mc, mc, w_in_p)


def _norm_rope(x, w, cos, sin):
    R, W = x.shape
    lane = lax.broadcasted_iota(jnp.int32, (R, 128), 1)
    left = lane < HEAD_DIM
    first = (lane & 31) < 16
    outs = []
    for c in range(W // 128):
        xc = x[:, c * 128:(c + 1) * 128]
        sq = xc * xc
        sl = jnp.sum(jnp.where(left, sq, 0.0), axis=-1, keepdims=True)
        sr = jnp.sum(jnp.where(left, 0.0, sq), axis=-1, keepdims=True)
        inv = jnp.where(left, lax.rsqrt(sl * (1.0 / HEAD_DIM) + EPS), lax.rsqrt(sr * (1.0 / HEAD_DIM) + EPS))
        y = xc * inv * w
        rot = jnp.where(first, pltpu.roll(y, 112, 1), pltpu.roll(y, 16, 1))
        outs.append(y * cos + rot * sin)
    return outs[0] if len(outs) == 1 else jnp.concatenate(outs, axis=-1)


def rope_tables128():
    n = HEAD_DIM // 4
    rows = S // GRID_W
    row_ids = jnp.repeat(jnp.arange(rows), GRID_W).astype(F32)
    col_ids = jnp.tile(jnp.arange(GRID_W), rows).astype(F32)
    inv = ROPE_THETA ** (-jnp.arange(n, dtype=F32) / n)
    ar = row_ids[:, None] * inv
    ac = col_ids[:, None] * inv
    cos64 = jnp.concatenate([jnp.cos(ar), jnp.cos(ar), jnp.cos(ac), jnp.cos(ac)], axis=-1)
    sin64 = jnp.concatenate([-jnp.sin(ar), jnp.sin(ar), -jnp.sin(ac), jnp.sin(ac)], axis=-1)
    cos = jnp.concatenate([jnp.ones((LC, 64), F32), cos64], axis=0)
    sin = jnp.concatenate([jnp.zeros((LC, 64), F32), sin64], axis=0)
    return jnp.tile(cos, (1, 2)), jnp.tile(sin, (1, 2))


SWA_BAND = 3 * SWA_WINDOW


def _swa_kernel(sink_ref, q_ref, k_ref, v_ref, qw_ref, kw_ref, cos_ref, sin_ref, o_ref, kn_ref, vb_ref, *, n_off):
    n = pl.program_id(1) + n_off

    @pl.when(pl.program_id(1) == 0)
    def _():
        kn_ref[...] = _norm_rope(k_ref[0], kw_ref[...], cos_ref[...], sin_ref[...]).astype(BF16)
        vb_ref[...] = v_ref[0].astype(BF16)

    r0 = pl.multiple_of(n * 128, 128)
    q = _norm_rope(q_ref[0], qw_ref[...], cos_ref[pl.ds(r0, 128), :], sin_ref[pl.ds(r0, 128), :])
    q = (q * (HEAD_DIM ** -0.5)).astype(BF16)

    nb = n - LC // 128
    is_lat = nb >= 0
    lstart = jnp.clip((nb - 1) * 128, 0, S - SWA_BAND)
    start = pl.multiple_of(LC + lstart, 128)
    kk = jnp.concatenate([kn_ref[0:LC, :], kn_ref[pl.ds(start, SWA_BAND), :]], axis=0)
    vv = jnp.concatenate([vb_ref[0:LC, :], vb_ref[pl.ds(start, SWA_BAND), :]], axis=0)
    nk = LC + SWA_BAND
    row = lax.broadcasted_iota(jnp.int32, (512, nk), 0) & 127
    col = lax.broadcasted_iota(jnp.int32, (512, nk), 1)
    qpos = nb * 128 + row
    kpos = lstart + col - LC
    valid = (col < LC) | ((jnp.abs(qpos - kpos) <= SWA_WINDOW) & is_lat)
    rowi = lax.broadcasted_iota(jnp.int32, (512, 1), 0)
    outs = [None] * SWA_HEADS
    for hk in range(SWA_KV):
        qs = jnp.concatenate([q[:, (hk * 4 + g) * 64:(hk * 4 + g + 1) * 64] for g in range(4)], axis=0)
        s = lax.dot_general(qs, kk[:, hk * 64:(hk + 1) * 64], (((1,), (1,)), ((), ())),
                            preferred_element_type=F32)
        s = jnp.where(valid, s, -jnp.inf)
        sk = jnp.where(rowi < 128, sink_ref[hk * 4],
                       jnp.where(rowi < 256, sink_ref[hk * 4 + 1],
                                 jnp.where(rowi < 384, sink_ref[hk * 4 + 2], sink_ref[hk * 4 + 3])))
        m = jnp.maximum(jnp.max(s, axis=-1, keepdims=True), sk)
        e = jnp.exp(s - m)
        den = jnp.sum(e, axis=-1, keepdims=True) + jnp.exp(sk - m)
        p = (e / den).astype(BF16)
        o = jnp.dot(p, vv[:, hk * 64:(hk + 1) * 64], preferred_element_type=F32)
        for g in range(4):
            outs[hk * 4 + g] = o[g * 128:(g + 1) * 128]
    o_ref[0] = jnp.concatenate(outs, axis=-1).astype(o_ref.dtype)


def swa_mixer(u, sink, qw, kw, cos, sin, with_ctx):
    n_off = 0 if with_ctx else LC // 128
    nblk = L // 128 - n_off
    qc, kc, vc = U_OFF["swa_q"] // 512, U_OFF["swa_k"] // 128, U_OFF["swa_v"] // 128
    return pl.pallas_call(
        functools.partial(_swa_kernel, n_off=n_off),
        grid_spec=pltpu.PrefetchScalarGridSpec(
            num_scalar_prefetch=1,
            grid=(B, nblk),
            in_specs=[pl.BlockSpec((1, 128, 512), lambda b, n, s: (b, n + n_off, qc)),
                      pl.BlockSpec((1, L, 128), lambda b, n, s: (b, 0, kc)),
                      pl.BlockSpec((1, L, 128), lambda b, n, s: (b, 0, vc)),
                      pl.BlockSpec((1, 128), lambda b, n, s: (0, 0)),
                      pl.BlockSpec((1, 128), lambda b, n, s: (0, 0)),
                      pl.BlockSpec((L, 128), lambda b, n, s: (0, 0)),
                      pl.BlockSpec((L, 128), lambda b, n, s: (0, 0))],
            out_specs=pl.BlockSpec((1, 128, 512), lambda b, n, s: (b, n, 0)),
            scratch_shapes=[pltpu.VMEM((L, 128), BF16), pltpu.VMEM((L, 128), BF16)]),
        out_shape=jax.ShapeDtypeStruct((B, nblk * 128, GROUP_W), BF16),
        compiler_params=_cparams(("parallel", "arbitrary")),
    )(sink, u, u, u, jnp.tile(qw, 2)[None], jnp.tile(kw, 2)[None], cos, sin)


TQ_DIFF = 256


def _diff_kernel(q_ref, k_ref, v_ref, qw_ref, kw_ref, cos_ref, sin_ref, dl_ref, sw_ref, o_ref, kn_ref, vb_ref,
                 *, j_off, lam_init):
    j = pl.program_id(2) + j_off

    @pl.when(pl.program_id(2) == 0)
    def _():
        kn_ref[...] = _norm_rope(k_ref[0], kw_ref[...], cos_ref[...], sin_ref[...]).astype(BF16)
        vb_ref[...] = v_ref[0].astype(BF16)

    dl = dl_ref[...]
    lam = (jnp.exp(jnp.sum(dl[0:1] * dl[1:2], axis=-1, keepdims=True))
           - jnp.exp(jnp.sum(dl[2:3] * dl[3:4], axis=-1, keepdims=True)) + lam_init)
    r0 = pl.multiple_of(j * TQ_DIFF, TQ_DIFF)
    q = _norm_rope(q_ref[0], qw_ref[...], cos_ref[pl.ds(r0, TQ_DIFF), :], sin_ref[pl.ds(r0, TQ_DIFF), :])
    q = (q * (HEAD_DIM ** -0.5)).astype(BF16)
    col = lax.broadcasted_iota(jnp.int32, (TQ_DIFF, L), 1)
    valid = (col < LC) | (j > 0)
    kn = kn_ref[...]
    ps = []
    for m in range(2):
        s = lax.dot_general(q[:, m * 64:(m + 1) * 64], kn[:, m * 64:(m + 1) * 64], (((1,), (1,)), ((), ())),
                            preferred_element_type=F32)
        s = jnp.where(valid, s, -jnp.inf)
        e = jnp.exp(s - jnp.max(s, axis=-1, keepdims=True))
        ps.append(e / jnp.sum(e, axis=-1, keepdims=True))
    wgt = (ps[0] - lam * ps[1]).astype(BF16)
    o = jnp.dot(wgt, vb_ref[...], preferred_element_type=F32)
    ms = jnp.mean(o * o, axis=-1, keepdims=True)
    o = o * lax.rsqrt(ms + EPS) * sw_ref[...] * (1.0 - lam_init)
    o_ref[0] = o.astype(o_ref.dtype)


def diff_mixer(u, qw, kw, cos, sin, dlam, subw, lam_init, with_ctx):
    j_off = 0 if with_ctx else LC // TQ_DIFF
    nblk = L // TQ_DIFF - j_off
    qc, kc, vc = U_OFF["diff_q"] // 128, U_OFF["diff_k"] // 128, U_OFF["diff_v"] // 128
    return pl.pallas_call(
        functools.partial(_diff_kernel, j_off=j_off, lam_init=lam_init),
        grid=(B, DIFF_HEADS, nblk),
        in_specs=[pl.BlockSpec((1, TQ_DIFF, 128), lambda b, h, j: (b, j + j_off, qc + h)),
                  pl.BlockSpec((1, L, 128), lambda b, h, j: (b, 0, kc + h)),
                  pl.BlockSpec((1, L, 128), lambda b, h, j: (b, 0, vc + h)),
                  pl.BlockSpec((1, 128), lambda b, h, j: (0, 0)),
                  pl.BlockSpec((1, 128), lambda b, h, j: (0, 0)),
                  pl.BlockSpec((L, 128), lambda b, h, j: (0, 0)),
                  pl.BlockSpec((L, 128), lambda b, h, j: (0, 0)),
                  pl.BlockSpec((4, 64), lambda b, h, j: (0, 0)),
                  pl.BlockSpec((1, 128), lambda b, h, j: (0, 0))],
        out_specs=pl.BlockSpec((1, TQ_DIFF, 128), lambda b, h, j: (b, j, h)),
        out_shape=jax.ShapeDtypeStruct((B, nblk * TQ_DIFF, GROUP_W), BF16),
        scratch_shapes=[pltpu.VMEM((L, 128), BF16), pltpu.VMEM((L, 128), BF16)],
        compiler_params=_cparams(("parallel", "parallel", "arbitrary")),
    )(u, u, u, jnp.tile(qw, 2)[None], jnp.tile(kw, 2)[None], cos, sin, dlam, subw[None])


TM_OUT = 256


def _outproj_kernel(x_ref, a_ref, b_ref, y_ref, d_ref, w_ref, nw_ref, ml_ref, mc_ref, rw_ref,
                    xo_ref, f_ref, lg_ref, *, i_off):
    is_ctx = (pl.program_id(1) + i_off) == 0

    def mod(k):
        return jnp.where(is_ctx, mc_ref[:, k * D:(k + 1) * D], ml_ref[0, :, k * D:(k + 1) * D])

    acc = jnp.dot(a_ref[0], w_ref[0:512, :], preferred_element_type=F32)
    acc += jnp.dot(b_ref[0], w_ref[512:1024, :], preferred_element_type=F32)
    acc += jnp.dot(y_ref[0], w_ref[1024:1536, :], preferred_element_type=F32)
    acc += jnp.dot(d_ref[0], w_ref[1536:2048, :], preferred_element_type=F32)
    xn = x_ref[0] + mod(2) * acc
    xo_ref[0] = xn
    ms = jnp.mean(xn * xn, axis=-1, keepdims=True)
    f = (xn * lax.rsqrt(ms + EPS) * nw_ref[...]) * (1.0 + mod(4)) + mod(3)
    f_ref[0] = f
    lg_ref[0] = jnp.dot(f, rw_ref[...], preferred_element_type=F32, precision=lax.Precision.HIGHEST)


def out_proj(xall, mixes, w_out_b, nw, ml, mc, rw_pad, with_ctx):
    i_off = 0 if with_ctx else LC // TM_OUT
    nblk = L // TM_OUT - i_off
    rows = lambda b, i: (b, i + i_off, 0)
    return pl.pallas_call(
        functools.partial(_outproj_kernel, i_off=i_off),
        grid=(B, nblk),
        in_specs=[pl.BlockSpec((1, TM_OUT, D), rows)]
                 + [pl.BlockSpec((1, TM_OUT, GROUP_W), lambda b, i: (b, i, 0))] * 4
                 + [pl.BlockSpec((D, D), lambda b, i: (0, 0)),
                    pl.BlockSpec((1, D), lambda b, i: (0, 0)),
                    pl.BlockSpec((1, 1, 6 * D), lambda b, i: (b, 0, 0)),
                    pl.BlockSpec((1, 6 * D), lambda b, i: (0, 0)),
                    pl.BlockSpec((D, 128), lambda b, i: (0, 0))],
        out_specs=[pl.BlockSpec((1, TM_OUT, D), lambda b, i: (b, i, 0)),
                   pl.BlockSpec((1, TM_OUT, D), lambda b, i: (b, i, 0)),
                   pl.BlockSpec((1, TM_OUT, 128), lambda b, i: (b, i, 0))],
        out_shape=[jax.ShapeDtypeStruct((B, nblk * TM_OUT, D), F32),
                   jax.ShapeDtypeStruct((B, nblk * TM_OUT, D), F32),
                   jax.ShapeDtypeStruct((B, nblk * TM_OUT, 128), F32)],
        compiler_params=_cparams(("parallel", "parallel")),
    )(xall, *mixes, w_out_b, nw, ml, mc, rw_pad)


TM_MOE = 256
R_GATHER = 256


def _gather_kernel(src_ref, f_hbm, o_ref, buf, sem):
    t = pl.program_id(0)

    def issue(r, carry):
        idx = src_ref[t * R_GATHER + r]
        pltpu.make_async_copy(f_hbm.at[pl.ds(idx, 1)], buf.at[pl.ds(r, 1)], sem).start()
        return carry

    lax.fori_loop(0, R_GATHER, issue, 0)

    def drain(r, carry):
        pltpu.make_async_copy(f_hbm.at[pl.ds(0, 1)], buf.at[pl.ds(r, 1)], sem).wait()
        return carry

    lax.fori_loop(0, R_GATHER, drain, 0)
    o_ref[...] = buf[...].astype(BF16)


def gather_rows(src, f2d, nrows):
    return pl.pallas_call(
        _gather_kernel,
        grid_spec=pltpu.PrefetchScalarGridSpec(
            num_scalar_prefetch=1,
            grid=(nrows // R_GATHER,),
            in_specs=[pl.BlockSpec(memory_space=pl.ANY)],
            out_specs=pl.BlockSpec((R_GATHER, D), lambda t, s: (t, 0)),
            scratch_shapes=[pltpu.VMEM((R_GATHER, D), F32), pltpu.SemaphoreType.DMA(())]),
        out_shape=jax.ShapeDtypeStruct((nrows, D), BF16),
        compiler_params=_cparams(("arbitrary",)),
    )(src, f2d)


def _moe_kernel(te_ref, tv_ref, x_ref, wg_ref, wu_ref, wd_ref, rw_ref, o_ref):
    @pl.when(tv_ref[pl.program_id(0)] == 0)
    def _():
        o_ref[...] = jnp.zeros_like(o_ref)

    @pl.when(tv_ref[pl.program_id(0)] > 0)
    def _():
        x = x_ref[...]
        g = jnp.dot(x, wg_ref[0], preferred_element_type=F32)
        u = jnp.dot(x, wu_ref[0], preferred_element_type=F32)
        h = (g / (1.0 + jnp.exp(-g)) * u).astype(BF16)
        y = jnp.dot(h, wd_ref[0], preferred_element_type=F32)
        o_ref[...] = y * rw_ref[...]


def moe_experts(tile_e, tile_v, xs, wg, wu, wd, roww):
    nrows = xs.shape[0]
    return pl.pallas_call(
        _moe_kernel,
        grid_spec=pltpu.PrefetchScalarGridSpec(
            num_scalar_prefetch=2,
            grid=(nrows // TM_MOE,),
            in_specs=[pl.BlockSpec((TM_MOE, D), lambda t, te, tv: (t, 0)),
                      pl.BlockSpec((1, D, D_EXPERT), lambda t, te, tv: (te[t], 0, 0)),
                      pl.BlockSpec((1, D, D_EXPERT), lambda t, te, tv: (te[t], 0, 0)),
                      pl.BlockSpec((1, D_EXPERT, D), lambda t, te, tv: (te[t], 0, 0)),
                      pl.BlockSpec((TM_MOE, 1), lambda t, te, tv: (t, 0))],
            out_specs=pl.BlockSpec((TM_MOE, D), lambda t, te, tv: (t, 0))),
        out_shape=jax.ShapeDtypeStruct((nrows, D), F32),
        compiler_params=_cparams(("arbitrary",)),
    )(tile_e, tile_v, xs, wg, wu, wd, roww)


R_COMB = 256


def _combine_kernel(pos_ref, x_ref, ml_ref, mc_ref, y_hbm, o_ref, buf0, buf1, sem, *, i_off, rows_per_b):
    b = pl.program_id(0)
    i = pl.program_id(1)
    is_ctx = (i + i_off) == 0
    base = (b * rows_per_b + i * R_COMB) * 2

    def issue(r, carry):
        p0 = pos_ref[base + 2 * r]
        p1 = pos_ref[base + 2 * r + 1]
        pltpu.make_async_copy(y_hbm.at[pl.ds(p0, 1)], buf0.at[pl.ds(r, 1)], sem.at[0]).start()
        pltpu.make_async_copy(y_hbm.at[pl.ds(p1, 1)], buf1.at[pl.ds(r, 1)], sem.at[1]).start()
        return carry

    lax.fori_loop(0, R_COMB, issue, 0)

    def drain(r, carry):
        pltpu.make_async_copy(y_hbm.at[pl.ds(0, 1)], buf0.at[pl.ds(r, 1)], sem.at[0]).wait()
        pltpu.make_async_copy(y_hbm.at[pl.ds(0, 1)], buf1.at[pl.ds(r, 1)], sem.at[1]).wait()
        return carry

    lax.fori_loop(0, R_COMB, drain, 0)
    gf = jnp.where(is_ctx, mc_ref[:, 5 * D:6 * D], ml_ref[0, :, 5 * D:6 * D])
    o_ref[0] = x_ref[0] + gf * (buf0[...] + buf1[...])


def moe_combine(pos, xall, ml, mc, y_sorted, with_ctx):
    i_off = 0 if with_ctx else LC // R_COMB
    rows_out = xall.shape[1]
    nblk = rows_out // R_COMB
    return pl.pallas_call(
        functools.partial(_combine_kernel, i_off=i_off, rows_per_b=rows_out),
        grid_spec=pltpu.PrefetchScalarGridSpec(
            num_scalar_prefetch=1,
            grid=(B, nblk),
            in_specs=[pl.BlockSpec((1, R_COMB, D), lambda b, i, p: (b, i, 0)),
                      pl.BlockSpec((1, 1, 6 * D), lambda b, i, p: (b, 0, 0)),
                      pl.BlockSpec((1, 6 * D), lambda b, i, p: (0, 0)),
                      pl.BlockSpec(memory_space=pl.ANY)],
            out_specs=pl.BlockSpec((1, R_COMB, D), lambda b, i, p: (b, i, 0)),
            scratch_shapes=[pltpu.VMEM((R_COMB, D), F32), pltpu.VMEM((R_COMB, D), F32),
                            pltpu.SemaphoreType.DMA((2,))]),
        out_shape=jax.ShapeDtypeStruct((B, rows_out, D), F32),
        compiler_params=_cparams(("arbitrary", "arbitrary")),
    )(pos, xall, ml, mc, y_sorted)


def route(logits, router_bias):
    T = logits.shape[0]
    probs = jax.nn.softmax(logits, axis=-1)
    sel = (probs + router_bias.astype(F32)).reshape(T, N_GROUPS, EPG)
    group = jnp.argmax(jnp.max(sel, axis=-1), axis=-1)
    sel_g = jnp.take_along_axis(sel, group[:, None, None], axis=1)[:, 0]
    _, idx = lax.top_k(sel_g, TOP_K)
    expert = (group[:, None] * EPG + idx).astype(jnp.int32)
    wts = jnp.take_along_axis(probs, expert, axis=-1)
    wts = wts / jnp.sum(wts, axis=-1, keepdims=True)
    flat_e = expert.reshape(-1)
    onehot = (flat_e[:, None] == jnp.arange(N_EXPERTS, dtype=jnp.int32)[None, :]).astype(jnp.int32)
    csum = jnp.cumsum(onehot, axis=0)
    rank = jnp.sum(csum * onehot, axis=-1) - 1
    counts = csum[-1]
    ptiles = (counts + TM_MOE - 1) // TM_MOE
    tend = jnp.cumsum(ptiles)
    tstart = tend - ptiles
    dest = (tstart[flat_e] * TM_MOE + rank).astype(jnp.int32)
    ntiles = -(-(TOP_K * T) // TM_MOE) + N_EXPERTS
    nrows = ntiles * TM_MOE
    tid = jnp.arange(ntiles, dtype=jnp.int32)
    tile_e = jnp.minimum(jnp.searchsorted(tend, tid, side="right"), N_EXPERTS - 1).astype(jnp.int32)
    tile_v = (tid < tend[-1]).astype(jnp.int32)
    src = jnp.zeros((nrows,), jnp.int32).at[dest].set(jnp.repeat(jnp.arange(T, dtype=jnp.int32), TOP_K))
    roww = jnp.zeros((nrows, 1), F32).at[dest, 0].set(wts.reshape(-1))
    return src, roww, dest, tile_e, tile_v


GLA_QW = GLA_HEADS * GLA_DK
N_BLK = L // GLA_CHUNK
N_BLK_C = LC // GLA_CHUNK


def _log_sigmoid(z):
    return jnp.minimum(z, 0.0) - jnp.log1p(jnp.exp(-jnp.abs(z)))


def _head_rms(x, w):
    R, W = x.shape
    lane = lax.broadcasted_iota(jnp.int32, (R, 128), 1)
    left = lane < 64
    outs = []
    for c in range(W // 128):
        xc = x[:, c * 128:(c + 1) * 128]
        sq = xc * xc
        sl = jnp.sum(jnp.where(left, sq, 0.0), axis=-1, keepdims=True)
        sr = jnp.sum(jnp.where(left, 0.0, sq), axis=-1, keepdims=True)
        inv = jnp.where(left, lax.rsqrt(sl * (1.0 / 64) + EPS), lax.rsqrt(sr * (1.0 / 64) + EPS))
        outs.append(xc * inv * w)
    return jnp.concatenate(outs, axis=-1)


def _gla_kernel(q_ref, k_ref, v_ref, g_ref, a_ref, wf_ref, wb_ref, gb_ref, nw_ref, ind_ref, bdm_ref, o_ref,
                lf_ref, lb_ref, acc_ref, sf_ref, sb_ref, *, row_off):
    hi = lax.Precision.HIGHEST
    a = a_ref[0]
    lf_ref[...] = _log_sigmoid(jnp.dot(a, wf_ref[...], preferred_element_type=F32, precision=hi)
                               + gb_ref[0:1, :]) * (1.0 / GLA_TEMP)
    lb_ref[...] = _log_sigmoid(jnp.dot(a, wb_ref[...], preferred_element_type=F32, precision=hi)
                               + gb_ref[1:2, :]) * (1.0 / GLA_TEMP)
    acc_ref[...] = jnp.zeros_like(acc_ref)
    sf_ref[...] = jnp.zeros_like(sf_ref)
    sb_ref[...] = jnp.zeros_like(sb_ref)

    C = GLA_CHUNK
    r16 = lax.broadcasted_iota(jnp.int32, (C, C), 0)
    c16 = lax.broadcasted_iota(jnp.int32, (C, C), 1)
    rowi = lax.broadcasted_iota(jnp.int32, (C, GLA_QW), 0)

    def block(r0, la_ref, s_ref, forward):
        tri = ((r16 >= c16) if forward else (r16 <= c16)).astype(F32)
        q = q_ref[0, pl.ds(r0, C), :] * (GLA_DK ** -0.5)
        k = k_ref[0, pl.ds(r0, C), :]
        v = v_ref[0, pl.ds(r0, C), :]
        la = la_ref[pl.ds(r0, C), :]
        b = jnp.dot(tri, la, preferred_element_type=F32, precision=hi)
        btot = jnp.sum(la, axis=0, keepdims=True)
        st = s_ref[...]
        o = lax.dot_general((q * jnp.exp(b)).astype(BF16), st.astype(BF16), (((1,), (1,)), ((), ())),
                            preferred_element_type=F32)
        ps = []
        for s in range(C):
            seen = (rowi >= s) if forward else (rowi <= s)
            e = jnp.exp(jnp.where(seen, b - b[s:s + 1, :], -jnp.inf))
            ps.append(q * k[s:s + 1, :] * e)
        pm = jnp.concatenate(ps, axis=0).astype(BF16)
        rm = jnp.dot(pm, ind_ref[...], preferred_element_type=F32)
        for s in range(C):
            o = o + rm[s * C:(s + 1) * C, :] * v[s:s + 1, :]
        acc_ref[pl.ds(r0, C), :] += o
        kd = (k * jnp.exp(btot - b)).astype(BF16)
        upd = lax.dot_general(v.astype(BF16), kd, (((0,), (0,)), ((), ())), preferred_element_type=F32)
        s_ref[...] = st * jnp.exp(btot) + upd * bdm_ref[...]

    def body(i, carry):
        block(pl.multiple_of(i * C, C), lf_ref, sf_ref, True)
        jb = jnp.where(i < N_BLK_C, N_BLK_C - 1 - i, N_BLK + N_BLK_C - 1 - i)
        block(pl.multiple_of(jb * C, C), lb_ref, sb_ref, False)
        return carry

    lax.fori_loop(0, N_BLK, body, 0)
    o = _head_rms(acc_ref[row_off:, :], nw_ref[...])
    g = g_ref[0, row_off:, :]
    o_ref[0] = (o * (g / (1.0 + jnp.exp(-g)))).astype(o_ref.dtype)


def gla_mixer(u, gate_w, gate_b, norm_w, with_ctx):
    row_off = 0 if with_ctx else LC
    qc, kc = U_OFF["gla_q"] // 256, U_OFF["gla_k"] // 256
    vc, gc, ac = U_OFF["gla_v"] // 512, U_OFF["gla_g"] // 512, U_OFF["gla_a"] // 128
    wf = jnp.zeros((128, GLA_QW), F32).at[0:GLA_RANK].set(gate_w[0])
    wb = jnp.zeros((128, GLA_QW), F32).at[GLA_RANK:2 * GLA_RANK].set(gate_w[1])
    hd = np.arange(GLA_QW)[:, None] // GLA_DK == np.arange(GROUP_W)[None, :] // GLA_DV
    ind = jnp.asarray(hd, BF16)
    bdm = jnp.asarray(hd.T, F32)
    full = lambda *shape: pl.BlockSpec(shape, lambda b: (0,) * len(shape))
    return pl.pallas_call(
        functools.partial(_gla_kernel, row_off=row_off),
        grid=(B,),
        in_specs=[pl.BlockSpec((1, L, 256), lambda b: (b, 0, qc)),
                  pl.BlockSpec((1, L, 256), lambda b: (b, 0, kc)),
                  pl.BlockSpec((1, L, 512), lambda b: (b, 0, vc)),
                  pl.BlockSpec((1, L, 512), lambda b: (b, 0, gc)),
                  pl.BlockSpec((1, L, 128), lambda b: (b, 0, ac)),
                  full(128, GLA_QW), full(128, GLA_QW), full(2, GLA_QW), full(1, 128),
                  full(GLA_QW, GROUP_W), full(GROUP_W, GLA_QW)],
        out_specs=pl.BlockSpec((1, L - row_off, GROUP_W), lambda b: (b, 0, 0)),
        out_shape=jax.ShapeDtypeStruct((B, L - row_off, GROUP_W), BF16),
        scratch_shapes=[pltpu.VMEM((L, GLA_QW), F32), pltpu.VMEM((L, GLA_QW), F32), pltpu.VMEM((L, GROUP_W), F32),
                        pltpu.VMEM((GROUP_W, GLA_QW), F32), pltpu.VMEM((GROUP_W, GLA_QW), F32)],
        compiler_params=_cparams(("parallel",)),
    )(u, u, u, u, u, wf, wb, gate_b, jnp.tile(norm_w, 2)[None], ind, bdm)


HY_CT = 256
HY_TK = 512


@functools.lru_cache(maxsize=None)
def _dft_consts(Lh):
    k = np.arange(Lh, dtype=np.int64)
    ph = (np.outer(k, k) % (2 * Lh)).astype(np.float64) * (np.pi / Lh)
    sgn = (1.0 - 2.0 * (k % 2)).astype(np.float32)[:, None]
    return np.cos(ph).astype(np.float32), np.sin(ph).astype(np.float32), sgn


def _hyena_features(Lh):
    t = jnp.linspace(0.0, 1.0, Lh, dtype=F32)[:, None]
    w = (2.0 * math.pi / Lh) * jnp.arange(Lh, dtype=F32)[:, None]
    bands = jnp.linspace(1e-4, HY_BANDS - 1, HY_BANDS, dtype=F32)
    z = jnp.concatenate([t, jnp.cos(w * bands), -jnp.sin(w * bands)], axis=-1)
    return jnp.pad(z, ((0, 0), (0, 128 - z.shape[1])))


def _hy_filter_kernel(z_ref, w1_ref, b1_ref, w2_ref, b2_ref, fr_ref, w3f_ref, w3b_ref, dl_ref, c_ref, s_ref, sgn_ref,
                      hc_ref, hs_ref, hn_ref, *, Lh):
    hi = lax.Precision.HIGHEST
    z = z_ref[...]
    h = jnp.sin(fr_ref[0:1, :] * (jnp.dot(z, w1_ref[...], preferred_element_type=F32, precision=hi) + b1_ref[...]))
    h = jnp.sin(fr_ref[1:2, :] * (jnp.dot(h, w2_ref[...], preferred_element_type=F32, precision=hi) + b2_ref[...]))
    dec = jnp.exp(-(z[:, 0:1] * dl_ref[...]))
    hf = jnp.dot(h, w3f_ref[...], preferred_element_type=F32, precision=hi) * dec
    hb = jnp.dot(h, w3b_ref[...], preferred_element_type=F32, precision=hi) * dec
    nrm = jnp.sum(jnp.abs(hf), axis=0, keepdims=True) + jnp.sum(jnp.abs(hb), axis=0, keepdims=True)
    row = lax.broadcasted_iota(jnp.int32, (Lh, 1), 0)
    hf = hf / nrm
    hb = jnp.where(row == 0, 0.0, hb / nrm)
    wk = jnp.where(row == 0, 0.5 / Lh, 1.0 / Lh)

    def project(m_ref, x):
        xh = x.astype(BF16)
        xl = (x - xh.astype(F32)).astype(BF16)
        return (jnp.dot(m_ref[...], xh, preferred_element_type=F32)
                + jnp.dot(m_ref[...], xl, preferred_element_type=F32))

    am = hf + hb
    hc_ref[0] = project(c_ref, am) * wk
    hs_ref[0] = project(s_ref, hf - hb) * wk
    hn_ref[0] = jnp.sum(am * sgn_ref[...], axis=0, keepdims=True) * (0.5 / Lh)


def hyena_spectrum(Lh, w1, b1, w2, b2, w3, fr, cmat, smat, sgn):
    z = _hyena_features(Lh)
    w1p = jnp.pad(w1, ((0, 128 - w1.shape[0]), (0, 0)))
    deltas = jnp.linspace(HY_MIN_DECAY, HY_MAX_DECAY, HY_CH, dtype=F32)[None]
    nct = HY_CH // HY_CT
    full = lambda *shape: pl.BlockSpec(shape, lambda o, c: (0,) * len(shape))
    return pl.pallas_call(
        functools.partial(_hy_filter_kernel, Lh=Lh),
        grid=(HY_ORDER, nct),
        in_specs=[full(Lh, 128), full(128, 64), full(1, 64), full(64, 64), full(1, 64), full(2, 64),
                  pl.BlockSpec((64, HY_CT), lambda o, c: (0, o * 2 * nct + c)),
                  pl.BlockSpec((64, HY_CT), lambda o, c: (0, o * 2 * nct + nct + c)),
                  pl.BlockSpec((1, HY_CT), lambda o, c: (0, c)),
                  full(Lh, Lh), full(Lh, Lh), full(Lh, 1)],
        out_specs=[pl.BlockSpec((1, Lh, HY_CT), lambda o, c: (o, 0, c)),
                   pl.BlockSpec((1, Lh, HY_CT), lambda o, c: (o, 0, c)),
                   pl.BlockSpec((1, 1, HY_CT), lambda o, c: (o, 0, c))],
        out_shape=[jax.ShapeDtypeStruct((HY_ORDER, Lh, HY_CH), F32),
                   jax.ShapeDtypeStruct((HY_ORDER, Lh, HY_CH), F32),
                   jax.ShapeDtypeStruct((HY_ORDER, 1, HY_CH), F32)],
        compiler_params=_cparams(("arbitrary", "arbitrary")),
    )(z, w1p, b1[None], w2, b2[None], fr, w3, w3, deltas, cmat, smat, sgn)


def _hyena_seq(row0, Lh, x1_ref, x2_ref, y_ref, cw_refs, cb_refs, bias_ref, c_ref, s_ref, hc_ref, hs_ref, hn_ref, *,
               scratch):
    r = lax.broadcasted_iota(jnp.int32, (Lh, 1), 0)

    def sconv(u_ref, w_ref, b_ref):
        u = u_ref[0, row0:row0 + Lh, :]
        up = jnp.where(r == 0, 0.0, pltpu.roll(u, 1, 0))
        dn = jnp.where(r == Lh - 1, 0.0, pltpu.roll(u, Lh - 1, 0))
        return up * w_ref[0:1, :] + u * w_ref[1:2, :] + dn * w_ref[2:3, :] + b_ref[...]

    gate_refs = (x1_ref, x2_ref)
    ys_ref, yb_ref, cv_ref = scratch
    rows = slice(0, Lh)
    ys_ref[rows, :] = sconv(y_ref, cw_refs[2], cb_refs[2])
    sgn = (1 - 2 * (r & 1)).astype(F32)
    tk = min(HY_TK, Lh)
    for o in range(HY_ORDER):
        y = ys_ref[rows, :]
        yb_ref[rows, :] = y.astype(BF16)
        cv_ref[rows, :] = sgn * (jnp.sum(y * sgn, axis=0, keepdims=True) * hn_ref[o])
        for m in range(Lh // tk):
            fs = slice(m * tk, (m + 1) * tk)
            yc = jnp.dot(c_ref[fs, :], yb_ref[rows, :], preferred_element_type=F32)
            ysn = jnp.dot(s_ref[fs, :], yb_ref[rows, :], preferred_element_type=F32)
            hc = hc_ref[o, fs, :]
            hs = hs_ref[o, fs, :]
            pc = (yc * hc - ysn * hs).astype(BF16)
            ps = (yc * hs + ysn * hc).astype(BF16)
            cv_ref[rows, :] += (jnp.dot(c_ref[:, fs], pc, preferred_element_type=F32)
                                + jnp.dot(s_ref[:, fs], ps, preferred_element_type=F32))
        ys_ref[rows, :] = (sconv(gate_refs[o], cw_refs[o], cb_refs[o])
                           * (cv_ref[rows, :] + ys_ref[rows, :] * bias_ref[o:o + 1, :]))
    return ys_ref[rows, :]


def _hyena_kernel(x1_ref, x2_ref, y_ref, w1_ref, w2_ref, w3_ref, b1_ref, b2_ref, b3_ref, bias_ref, *rest, with_ctx):
    nmat = 10 if with_ctx else 5
    o_ref = rest[nmat]
    scratch = rest[nmat + 1:]
    seqs = [(LC, S, rest[0:5])]
    if with_ctx:
        seqs.append((0, LC, rest[5:10]))
    for row0, Lh, mats in seqs:
        y = _hyena_seq(row0, Lh, x1_ref, x2_ref, y_ref, (w1_ref, w2_ref, w3_ref), (b1_ref, b2_ref, b3_ref),
                       bias_ref, *mats, scratch=scratch)
        out0 = row0 if with_ctx else 0
        o_ref[0, out0:out0 + Lh, :] = y.astype(o_ref.dtype)


def hyena_mixer(u, conv_w, conv_b, w1, b1, w2, b2, w3, fr, bias, with_ctx):
    nct = HY_CH // HY_CT
    hc0 = U_OFF["hy"] // HY_CT
    once = pl.Buffered(1)
    consts = []
    specs = []
    for Lh in ((S, LC) if with_ctx else (S,)):
        cm, sm, sgn = _dft_consts(Lh)
        cm, sm, sgn = jnp.asarray(cm, BF16), jnp.asarray(sm, BF16), jnp.asarray(sgn)
        hc, hs, hn = hyena_spectrum(Lh, w1, b1, w2, b2, w3, fr, cm, sm, sgn)
        consts += [cm, sm, hc, hs, hn]
        specs += [pl.BlockSpec((Lh, Lh), lambda c, b: (0, 0), pipeline_mode=once),
                  pl.BlockSpec((Lh, Lh), lambda c, b: (0, 0), pipeline_mode=once),
                  pl.BlockSpec((HY_ORDER, Lh, HY_CT), lambda c, b: (0, 0, c), pipeline_mode=once),
                  pl.BlockSpec((HY_ORDER, Lh, HY_CT), lambda c, b: (0, 0, c), pipeline_mode=once),
                  pl.BlockSpec((HY_ORDER, 1, HY_CT), lambda c, b: (0, 0, c), pipeline_mode=once)]
    rows_out = L if with_ctx else S
    ublk = lambda j: pl.BlockSpec((1, L, HY_CT), lambda c, b: (b, 0, hc0 + j * nct + c), pipeline_mode=once)
    wblk = lambda j: pl.BlockSpec((3, HY_CT), lambda c, b: (0, j * nct + c))
    bblk = lambda j: pl.BlockSpec((1, HY_CT), lambda c, b: (0, j * nct + c))
    return pl.pallas_call(
        functools.partial(_hyena_kernel, with_ctx=with_ctx),
        grid=(nct, B),
        in_specs=[ublk(0), ublk(1), ublk(2), wblk(0), wblk(1), wblk(2), bblk(0), bblk(1), bblk(2),
                  pl.BlockSpec((HY_ORDER, HY_CT), lambda c, b: (0, c))] + specs,
        out_specs=pl.BlockSpec((1, rows_out, HY_CT), lambda c, b: (b, 0, c)),
        out_shape=jax.ShapeDtypeStruct((B, rows_out, HY_CH), BF16),
        scratch_shapes=[pltpu.VMEM((S, HY_CT), F32), pltpu.VMEM((S, HY_CT), BF16), pltpu.VMEM((S, HY_CT), F32)],
        compiler_params=_cparams(("arbitrary", "arbitrary")),
    )(u, u, u, conv_w, conv_w, conv_w, conv_b[None], conv_b[None], conv_b[None], bias, *consts)


def _permute_w_in(w):
    parts = []
    for n in U_ORDER:
        o, wd = REF_COLS[n]
        p = w[:, o:o + wd]
        if wd < 128:
            p = jnp.pad(p, ((0, 0), (0, 128 - wd)))
        parts.append(p)
    parts.append(jnp.zeros((w.shape[0], U_W - U_USED), w.dtype))
    return jnp.concatenate(parts, axis=1).astype(BF16)


def kernel(x, c, ctx, c_ctx, norm1_w, norm2_w, ada_w, ada_b, w_in, w_out, gla_gate_w, gla_gate_b, gla_norm_w,
           swa_q_norm_w, swa_k_norm_w, swa_sink, hyena_conv_w, hyena_conv_b, hyena_ffn_w1, hyena_ffn_b1,
           hyena_ffn_w2, hyena_ffn_b2, hyena_ffn_w3, hyena_ffn_freq, hyena_bias, diff_q_norm_w, diff_k_norm_w,
           diff_lambda, diff_subln_w, router_w, router_bias, expert_w_gate, expert_w_up, expert_w_down):
    assert x.shape == (B, S, D) and ctx.shape == (B, LC, D)
    cc = jnp.zeros((16, D), F32).at[:B].set(c).at[B].set(c_ctx)
    mods = ada_mod(cc, ada_w, ada_b)
    cos, sin = rope_tables128()
    rw_pad = jnp.pad(router_w, ((0, 0), (0, 128 - N_EXPERTS)))
    xall = jnp.concatenate([ctx, x], axis=1)

    for l in range(DEPTH):
        with_ctx = l < DEPTH - 1
        lam_init = 0.8 - 0.6 * math.exp(-0.3 * l)
        ml = mods[l, :B].reshape(B, 1, 6 * D)
        mc = mods[l, B:B + 1]
        u = in_proj(xall, norm1_w[l][None], ml, mc, _permute_w_in(w_in[l]))
        mix_a = gla_mixer(u, gla_gate_w[l], gla_gate_b[l], gla_norm_w[l], with_ctx)
        mix_b = swa_mixer(u, swa_sink[l], swa_q_norm_w[l], swa_k_norm_w[l], cos, sin, with_ctx)
        mix_y = hyena_mixer(u, hyena_conv_w[l], hyena_conv_b[l], hyena_ffn_w1[l], hyena_ffn_b1[l],
                            hyena_ffn_w2[l], hyena_ffn_b2[l], hyena_ffn_w3[l], hyena_ffn_freq[l],
                            hyena_bias[l], with_ctx)
        mix_d = diff_mixer(u, diff_q_norm_w[l], diff_k_norm_w[l], cos, sin, diff_lambda[l], diff_subln_w[l],
                           lam_init, with_ctx)
        xall, f, logits = out_proj(xall, (mix_a, mix_b, mix_y, mix_d), w_out[l].astype(BF16), norm2_w[l][None],
                                   ml, mc, rw_pad, with_ctx)
        T = B * f.shape[1]
        src, roww, dest, tile_e, tile_v = route(logits.reshape(T, 128)[:, :N_EXPERTS], router_bias)
        xs = gather_rows(src, f.reshape(T, D), src.shape[0])
        ys = moe_experts(tile_e, tile_v, xs, expert_w_gate[l].astype(BF16), expert_w_up[l].astype(BF16),
                         expert_w_down[l].astype(BF16), roww)
        xall = moe_combine(dest, xall, ml, mc, ys, with_ctx)
    return xall
```

```python
import functools
import math

import numpy as np
import jax
import jax.numpy as jnp
from jax import lax
from jax.experimental import pallas as pl
from jax.experimental.pallas import tpu as pltpu

F32 = jnp.float32
BF16 = jnp.bfloat16

D = 2048
B = 8
S = 2048
LC = 256
L = LC + S
DEPTH = 2
GRID_W = 64
HEAD_DIM = 64
ROPE_THETA = 10000.0
EPS = 1e-6
GROUP_W = 512

GLA_DV = 64
GLA_DK = 32
GLA_HEADS = 8
GLA_RANK = 16
GLA_TEMP = 16.0
GLA_CHUNK = 16
SWA_HEADS = 8
SWA_KV = 2
SWA_WINDOW = 128
HY_CH = 512
HY_ORDER = 2
HY_BANDS = 16
HY_TARGET = 1e-2
HY_MIN_DECAY = math.log(1.0 / HY_TARGET) / 1.5
HY_MAX_DECAY = math.log(1.0 / HY_TARGET) / 0.3
DIFF_HEADS = 4
N_EXPERTS = 16
N_GROUPS = 4
EPG = 4
TOP_K = 2
D_EXPERT = 1024

REF_COLS = dict(gla_q=(0, 256), gla_k=(256, 256), gla_v=(512, 512), gla_g=(1024, 512), gla_a=(1536, 32),
                swa_q=(1568, 512), swa_k=(2080, 128), swa_v=(2208, 128), hy=(2336, 1536),
                diff_q=(3872, 512), diff_k=(4384, 512), diff_v=(4896, 512))
U_ORDER = ("hy", "gla_v", "gla_g", "swa_q", "diff_q", "diff_k", "diff_v", "gla_q", "gla_k", "swa_k", "swa_v", "gla_a")
U_OFF = {}
_o = 0
for _n in U_ORDER:
    U_OFF[_n] = _o
    _o += max(REF_COLS[_n][1], 128)
U_USED = _o
TN_IN = 512
U_W = -(-U_USED // TN_IN) * TN_IN

VMEM_LIMIT = 56 * 1024 * 1024


def _cparams(sem):
    return pltpu.CompilerParams(dimension_semantics=sem, vmem_limit_bytes=VMEM_LIMIT)


def _ada_kernel(c_ref, w_ref, b_ref, o_ref):
    c = c_ref[...]
    a = c / (1.0 + jnp.exp(-c))
    o_ref[0] = jnp.dot(a, w_ref[0], preferred_element_type=F32, precision=lax.Precision.HIGHEST) + b_ref[0]


def ada_mod(cc, ada_w, ada_b):
    tn = 1024
    return pl.pallas_call(
        _ada_kernel,
        grid=(DEPTH, 6 * D // tn),
        in_specs=[pl.BlockSpec((16, D), lambda l, j: (0, 0)),
                  pl.BlockSpec((1, D, tn), lambda l, j: (l, 0, j)),
                  pl.BlockSpec((1, 1, tn), lambda l, j: (l, 0, j))],
        out_specs=pl.BlockSpec((1, 16, tn), lambda l, j: (l, 0, j)),
        out_shape=jax.ShapeDtypeStruct((DEPTH, 16, 6 * D), F32),
        compiler_params=_cparams(("parallel", "parallel")),
    )(cc, ada_w, ada_b.reshape(DEPTH, 1, 6 * D))


TM_IN = 1152
CH_IN = 32


def _inproj_kernel(x_ref, nw_ref, scl_ref, shl_ref, scc_ref, shc_ref, w_ref, o_ref, h_ref):
    i = pl.program_id(1)

    @pl.when(pl.program_id(2) == 0)
    def _():
        def chunk(c, carry):
            r0 = pl.multiple_of(c * CH_IN, CH_IN)
            x = x_ref[0, pl.ds(r0, CH_IN), :]
            ms = jnp.mean(x * x, axis=-1, keepdims=True)
            y = x * lax.rsqrt(ms + EPS) * nw_ref[...]
            row = i * TM_IN + r0 + lax.broadcasted_iota(jnp.int32, (CH_IN, 1), 0)
            is_ctx = row < LC
            sc = jnp.where(is_ctx, scc_ref[...], scl_ref[0])
            sh = jnp.where(is_ctx, shc_ref[...], shl_ref[0])
            h_ref[pl.ds(r0, CH_IN), :] = (y * (1.0 + sc) + sh).astype(BF16)
            return carry

        lax.fori_loop(0, TM_IN // CH_IN, chunk, 0)

    o_ref[0] = jnp.dot(h_ref[...], w_ref[...], preferred_element_type=F32).astype(o_ref.dtype)


def in_proj(xall, nw, ml, mc, w_in_p):
    return pl.pallas_call(
        _inproj_kernel,
        grid=(B, L // TM_IN, U_W // TN_IN),
        in_specs=[pl.BlockSpec((1, TM_IN, D), lambda b, i, j: (b, i, 0)),
                  pl.BlockSpec((1, D), lambda b, i, j: (0, 0)),
                  pl.BlockSpec((1, 1, D), lambda b, i, j: (b, 0, 1)),
                  pl.BlockSpec((1, 1, D), lambda b, i, j: (b, 0, 0)),
                  pl.BlockSpec((1, D), lambda b, i, j: (0, 1)),
                  pl.BlockSpec((1, D), lambda b, i, j: (0, 0)),
                  pl.BlockSpec((D, TN_IN), lambda b, i, j: (0, j))],
        out_specs=pl.BlockSpec((1, TM_IN, TN_IN), lambda b, i, j: (b, i, j)),
        out_shape=jax.ShapeDtypeStruct((B, L, U_W), BF16),
        scratch_shapes=[pltpu.VMEM((TM_IN, D), BF16)],
        compiler_params=_cparams(("parallel", "parallel", "arbitrary")),
    )(xall, nw, ml, ml, mc, mc, w_in_p)


def _norm_rope(x, w, cos, sin):
    R, W = x.shape
    lane = lax.broadcasted_iota(jnp.int32, (R, 128), 1)
    left = lane < HEAD_DIM
    first = (lane & 31) < 16
    outs = []
    for c in range(W // 128):
        xc = x[:, c * 128:(c + 1) * 128]
        sq = xc * xc
        sl = jnp.sum(jnp.where(left, sq, 0.0), axis=-1, keepdims=True)
        sr = jnp.sum(jnp.where(left, 0.0, sq), axis=-1, keepdims=True)
        inv = jnp.where(left, lax.rsqrt(sl * (1.0 / HEAD_DIM) + EPS), lax.rsqrt(sr * (1.0 / HEAD_DIM) + EPS))
        y = xc * inv * w
        rot = jnp.where(first, pltpu.roll(y, 112, 1), pltpu.roll(y, 16, 1))
        outs.append(y * cos + rot * sin)
    return outs[0] if len(outs) == 1 else jnp.concatenate(outs, axis=-1)


def rope_tables128():
    n = HEAD_DIM // 4
    rows = S // GRID_W
    row_ids = jnp.repeat(jnp.arange(rows), GRID_W).astype(F32)
    col_ids = jnp.tile(jnp.arange(GRID_W), rows).astype(F32)
    inv = ROPE_THETA ** (-jnp.arange(n, dtype=F32) / n)
    ar = row_ids[:, None] * inv
    ac = col_ids[:, None] * inv
    cos64 = jnp.concatenate([jnp.cos(ar), jnp.cos(ar), jnp.cos(ac), jnp.cos(ac)], axis=-1)
    sin64 = jnp.concatenate([-jnp.sin(ar), jnp.sin(ar), -jnp.sin(ac), jnp.sin(ac)], axis=-1)
    cos = jnp.concatenate([jnp.ones((LC, 64), F32), cos64], axis=0)
    sin = jnp.concatenate([jnp.zeros((LC, 64), F32), sin64], axis=0)
    return jnp.tile(cos, (1, 2)), jnp.tile(sin, (1, 2))


SWA_BAND = 3 * SWA_WINDOW


def _swa_kernel(sink_ref, q_ref, k_ref, v_ref, qw_ref, kw_ref, cos_ref, sin_ref, o_ref, kn_ref, *, n_off):
    n = pl.program_id(1) + n_off

    @pl.when(pl.program_id(1) == 0)
    def _():
        kn_ref[...] = _norm_rope(k_ref[0].astype(F32), kw_ref[...], cos_ref[...], sin_ref[...]).astype(BF16)

    r0 = pl.multiple_of(n * 128, 128)
    q = _norm_rope(q_ref[0].astype(F32), qw_ref[...], cos_ref[pl.ds(r0, 128), :], sin_ref[pl.ds(r0, 128), :])
    q = (q * (HEAD_DIM ** -0.5)).astype(BF16)

    nb = n - LC // 128
    is_lat = nb >= 0
    lstart = jnp.clip((nb - 1) * 128, 0, S - SWA_BAND)
    start = pl.multiple_of(LC + lstart, 128)
    kk = jnp.concatenate([kn_ref[0:LC, :], kn_ref[pl.ds(start, SWA_BAND), :]], axis=0)
    vv = jnp.concatenate([v_ref[0, 0:LC, :], v_ref[0, pl.ds(start, SWA_BAND), :]], axis=0)
    nk = LC + SWA_BAND
    row = lax.broadcasted_iota(jnp.int32, (512, nk), 0) & 127
    col = lax.broadcasted_iota(jnp.int32, (512, nk), 1)
    qpos = nb * 128 + row
    kpos = lstart + col - LC
    valid = (col < LC) | ((jnp.abs(qpos - kpos) <= SWA_WINDOW) & is_lat)
    rowi = lax.broadcasted_iota(jnp.int32, (512, 1), 0)
    outs = [None] * SWA_HEADS
    for hk in range(SWA_KV):
        qs = jnp.concatenate([q[:, (hk * 4 + g) * 64:(hk * 4 + g + 1) * 64] for g in range(4)], axis=0)
        s = lax.dot_general(qs, kk[:, hk * 64:(hk + 1) * 64], (((1,), (1,)), ((), ())),
                            preferred_element_type=F32)
        s = jnp.where(valid, s, -jnp.inf)
        sk = jnp.where(rowi < 128, sink_ref[hk * 4],
                       jnp.where(rowi < 256, sink_ref[hk * 4 + 1],
                                 jnp.where(rowi < 384, sink_ref[hk * 4 + 2], sink_ref[hk * 4 + 3])))
        m = jnp.maximum(jnp.max(s, axis=-1, keepdims=True), sk)
        e = jnp.exp(s - m)
        den = jnp.sum(e, axis=-1, keepdims=True) + jnp.exp(sk - m)
        o = jnp.dot(e.astype(BF16), vv[:, hk * 64:(hk + 1) * 64], preferred_element_type=F32) / den
        for g in range(4):
            outs[hk * 4 + g] = o[g * 128:(g + 1) * 128]
    o_ref[0] = jnp.concatenate(outs, axis=-1).astype(o_ref.dtype)


def swa_mixer(u, sink, qw, kw, cos, sin, with_ctx):
    n_off = 0 if with_ctx else LC // 128
    nblk = L // 128 - n_off
    qc, kc, vc = U_OFF["swa_q"] // 512, U_OFF["swa_k"] // 128, U_OFF["swa_v"] // 128
    return pl.pallas_call(
        functools.partial(_swa_kernel, n_off=n_off),
        grid_spec=pltpu.PrefetchScalarGridSpec(
            num_scalar_prefetch=1,
            grid=(B, nblk),
            in_specs=[pl.BlockSpec((1, 128, 512), lambda b, n, s: (b, n + n_off, qc)),
                      pl.BlockSpec((1, L, 128), lambda b, n, s: (b, 0, kc)),
                      pl.BlockSpec((1, L, 128), lambda b, n, s: (b, 0, vc)),
                      pl.BlockSpec((1, 128), lambda b, n, s: (0, 0)),
                      pl.BlockSpec((1, 128), lambda b, n, s: (0, 0)),
                      pl.BlockSpec((L, 128), lambda b, n, s: (0, 0)),
                      pl.BlockSpec((L, 128), lambda b, n, s: (0, 0))],
            out_specs=pl.BlockSpec((1, 128, 512), lambda b, n, s: (b, n, 0)),
            scratch_shapes=[pltpu.VMEM((L, 128), BF16)]),
        out_shape=jax.ShapeDtypeStruct((B, nblk * 128, GROUP_W), BF16),
        compiler_params=_cparams(("parallel", "arbitrary")),
    )(sink, u, u, u, jnp.tile(qw, 2)[None], jnp.tile(kw, 2)[None], cos, sin)


TQ_DIFF = 256


def _diff_kernel(q_ref, k_ref, v_ref, qw_ref, kw_ref, cos_ref, sin_ref, dl_ref, sw_ref, o_ref, kn_ref,
                 *, j_off, lam_init):
    j = pl.program_id(2) + j_off

    @pl.when(pl.program_id(2) == 0)
    def _():
        kn_ref[...] = _norm_rope(k_ref[0].astype(F32), kw_ref[...], cos_ref[...], sin_ref[...]).astype(BF16)

    dl = dl_ref[...]
    lam = (jnp.exp(jnp.sum(dl[0:1] * dl[1:2], axis=-1, keepdims=True))
           - jnp.exp(jnp.sum(dl[2:3] * dl[3:4], axis=-1, keepdims=True)) + lam_init)
    r0 = pl.multiple_of(j * TQ_DIFF, TQ_DIFF)
    q = _norm_rope(q_ref[0].astype(F32), qw_ref[...], cos_ref[pl.ds(r0, TQ_DIFF), :], sin_ref[pl.ds(r0, TQ_DIFF), :])
    q = (q * (HEAD_DIM ** -0.5)).astype(BF16)

    def attend(nk):
        os_ = []
        for m in range(2):
            s = lax.dot_general(q[:, m * 64:(m + 1) * 64], kn_ref[0:nk, m * 64:(m + 1) * 64],
                                (((1,), (1,)), ((), ())), preferred_element_type=F32)
            e = jnp.exp(s - jnp.max(s, axis=-1, keepdims=True))
            den = jnp.sum(e, axis=-1, keepdims=True)
            os_.append(jnp.dot(e.astype(BF16), v_ref[0, 0:nk, :], preferred_element_type=F32) / den)
        o = os_[0] - lam * os_[1]
        ms = jnp.mean(o * o, axis=-1, keepdims=True)
        o = o * lax.rsqrt(ms + EPS) * sw_ref[...] * (1.0 - lam_init)
        o_ref[0] = o.astype(o_ref.dtype)

    if j_off == 0:
        @pl.when(j == 0)
        def _():
            attend(LC)

    @pl.when(j > 0)
    def _():
        attend(L)


def diff_mixer(u, qw, kw, cos, sin, dlam, subw, lam_init, with_ctx):
    j_off = 0 if with_ctx else LC // TQ_DIFF
    nblk = L // TQ_DIFF - j_off
    qc, kc, vc = U_OFF["diff_q"] // 128, U_OFF["diff_k"] // 128, U_OFF["diff_v"] // 128
    return pl.pallas_call(
        functools.partial(_diff_kernel, j_off=j_off, lam_init=lam_init),
        grid=(B, DIFF_HEADS, nblk),
        in_specs=[pl.BlockSpec((1, TQ_DIFF, 128), lambda b, h, j: (b, j + j_off, qc + h)),
                  pl.BlockSpec((1, L, 128), lambda b, h, j: (b, 0, kc + h)),
                  pl.BlockSpec((1, L, 128), lambda b, h, j: (b, 0, vc + h)),
                  pl.BlockSpec((1, 128), lambda b, h, j: (0, 0)),
                  pl.BlockSpec((1, 128), lambda b, h, j: (0, 0)),
                  pl.BlockSpec((L, 128), lambda b, h, j: (0, 0)),
                  pl.BlockSpec((L, 128), lambda b, h, j: (0, 0)),
                  pl.BlockSpec((4, 64), lambda b, h, j: (0, 0)),
                  pl.BlockSpec((1, 128), lambda b, h, j: (0, 0))],
        out_specs=pl.BlockSpec((1, TQ_DIFF, 128), lambda b, h, j: (b, j, h)),
        out_shape=jax.ShapeDtypeStruct((B, nblk * TQ_DIFF, GROUP_W), BF16),
        scratch_shapes=[pltpu.VMEM((L, 128), BF16)],
        compiler_params=_cparams(("parallel", "parallel", "arbitrary")),
    )(u, u, u, jnp.tile(qw, 2)[None], jnp.tile(kw, 2)[None], cos, sin, dlam, subw[None])


TM_OUT = 256
TOK_SUB = D // 128


def _store_token_rows(ref, val):
    R = val.shape[0]
    for s in range(TOK_SUB):
        ref[pl.ds(s, R, stride=TOK_SUB), :] = val[:, s * 128:(s + 1) * 128]


def _load_token_rows(ref, R):
    return jnp.concatenate([ref[pl.ds(s, R, stride=TOK_SUB), :] for s in range(TOK_SUB)], axis=-1)


def _outproj_kernel(x_ref, a_ref, b_ref, y_ref, d_ref, w_ref, nw_ref, ml_ref, mc_ref, rwh_ref, rwl_ref,
                    xo_ref, f_ref, lg_ref, *, i_off):
    is_ctx = (pl.program_id(1) + i_off) == 0

    def mod(k):
        return jnp.where(is_ctx, mc_ref[:, k * D:(k + 1) * D], ml_ref[0, :, k * D:(k + 1) * D])

    acc = jnp.dot(a_ref[0], w_ref[0:512, :], preferred_element_type=F32)
    acc += jnp.dot(b_ref[0], w_ref[512:1024, :], preferred_element_type=F32)
    acc += jnp.dot(y_ref[0], w_ref[1024:1536, :], preferred_element_type=F32)
    acc += jnp.dot(d_ref[0], w_ref[1536:2048, :], preferred_element_type=F32)
    xn = x_ref[0] + mod(2) * acc
    xo_ref[0] = xn
    ms = jnp.mean(xn * xn, axis=-1, keepdims=True)
    f = (xn * lax.rsqrt(ms + EPS) * nw_ref[...]) * (1.0 + mod(4)) + mod(3)
    _store_token_rows(f_ref.at[0], f)
    fh = f.astype(BF16)
    fl = (f - fh.astype(F32)).astype(BF16)
    lg_ref[0] = (jnp.dot(fh, rwh_ref[...], preferred_element_type=F32)
                 + jnp.dot(fl, rwh_ref[...], preferred_element_type=F32)
                 + jnp.dot(fh, rwl_ref[...], preferred_element_type=F32))


def out_proj(xall, mixes, w_out_b, nw, ml, mc, rw_pad, with_ctx):
    i_off = 0 if with_ctx else LC // TM_OUT
    nblk = L // TM_OUT - i_off
    rows = lambda b, i: (b, i + i_off, 0)
    rwh = rw_pad.astype(BF16)
    rwl = (rw_pad - rwh.astype(F32)).astype(BF16)
    return pl.pallas_call(
        functools.partial(_outproj_kernel, i_off=i_off),
        grid=(B, nblk),
        in_specs=[pl.BlockSpec((1, TM_OUT, D), rows)]
                 + [pl.BlockSpec((1, TM_OUT, GROUP_W), lambda b, i: (b, i, 0))] * 4
                 + [pl.BlockSpec((D, D), lambda b, i: (0, 0)),
                    pl.BlockSpec((1, D), lambda b, i: (0, 0)),
                    pl.BlockSpec((1, 1, 6 * D), lambda b, i: (b, 0, 0)),
                    pl.BlockSpec((1, 6 * D), lambda b, i: (0, 0)),
                    pl.BlockSpec((D, 128), lambda b, i: (0, 0)),
                    pl.BlockSpec((D, 128), lambda b, i: (0, 0))],
        out_specs=[pl.BlockSpec((1, TM_OUT, D), lambda b, i: (b, i, 0)),
                   pl.BlockSpec((1, TM_OUT * TOK_SUB, 128), lambda b, i: (b, i, 0)),
                   pl.BlockSpec((1, TM_OUT, 128), lambda b, i: (b, i, 0))],
        out_shape=[jax.ShapeDtypeStruct((B, nblk * TM_OUT, D), F32),
                   jax.ShapeDtypeStruct((B, nblk * TM_OUT * TOK_SUB, 128), F32),
                   jax.ShapeDtypeStruct((B, nblk * TM_OUT, 128), F32)],
        compiler_params=_cparams(("parallel", "parallel")),
    )(xall, *mixes, w_out_b, nw, ml, mc, rwh, rwl)


TM_MOE = 256
R_GATHER = 256


def _token_copy(src_hbm, idx, buf, r, sem):
    return pltpu.make_async_copy(src_hbm.at[pl.ds(idx * TOK_SUB, TOK_SUB)], buf.at[pl.ds(r * TOK_SUB, TOK_SUB)], sem)


def _gather_kernel(src_ref, f_hbm, o_ref, buf, sem):
    t = pl.program_id(0)

    def issue(r, carry):
        _token_copy(f_hbm, src_ref[t * R_GATHER + r], buf, r, sem).start()
        return carry

    lax.fori_loop(0, R_GATHER, issue, 0, unroll=8)

    def drain(r, carry):
        _token_copy(f_hbm, 0, buf, r, sem).wait()
        return carry

    lax.fori_loop(0, R_GATHER, drain, 0, unroll=8)
    o_ref[...] = _load_token_rows(buf, R_GATHER).astype(BF16)


def gather_rows(src, f_tok, nrows):
    return pl.pallas_call(
        _gather_kernel,
        grid_spec=pltpu.PrefetchScalarGridSpec(
            num_scalar_prefetch=1,
            grid=(nrows // R_GATHER,),
            in_specs=[pl.BlockSpec(memory_space=pl.ANY)],
            out_specs=pl.BlockSpec((R_GATHER, D), lambda t, s: (t, 0)),
            scratch_shapes=[pltpu.VMEM((R_GATHER * TOK_SUB, 128), F32), pltpu.SemaphoreType.DMA(())]),
        out_shape=jax.ShapeDtypeStruct((nrows, D), BF16),
        compiler_params=_cparams(("arbitrary",)),
    )(src, f_tok)


def _moe_kernel(te_ref, tv_ref, x_ref, wg_ref, wu_ref, wd_ref, rw_ref, o_ref):
    @pl.when(tv_ref[pl.program_id(0)] == 0)
    def _():
        o_ref[...] = jnp.zeros_like(o_ref)

    @pl.when(tv_ref[pl.program_id(0)] > 0)
    def _():
        x = x_ref[...]
        g = jnp.dot(x, wg_ref[0], preferred_element_type=F32)
        u = jnp.dot(x, wu_ref[0], preferred_element_type=F32)
        h = (g / (1.0 + jnp.exp(-g)) * u).astype(BF16)
        y = jnp.dot(h, wd_ref[0], preferred_element_type=F32)
        _store_token_rows(o_ref, y * rw_ref[...])


def moe_experts(tile_e, tile_v, xs, wg, wu, wd, roww):
    nrows = xs.shape[0]
    return pl.pallas_call(
        _moe_kernel,
        grid_spec=pltpu.PrefetchScalarGridSpec(
            num_scalar_prefetch=2,
            grid=(nrows // TM_MOE,),
            in_specs=[pl.BlockSpec((TM_MOE, D), lambda t, te, tv: (t, 0)),
                      pl.BlockSpec((1, D, D_EXPERT), lambda t, te, tv: (te[t], 0, 0)),
                      pl.BlockSpec((1, D, D_EXPERT), lambda t, te, tv: (te[t], 0, 0)),
                      pl.BlockSpec((1, D_EXPERT, D), lambda t, te, tv: (te[t], 0, 0)),
                      pl.BlockSpec((TM_MOE, 1), lambda t, te, tv: (t, 0))],
            out_specs=pl.BlockSpec((TM_MOE * TOK_SUB, 128), lambda t, te, tv: (t, 0))),
        out_shape=jax.ShapeDtypeStruct((nrows * TOK_SUB, 128), F32),
        compiler_params=_cparams(("arbitrary",)),
    )(tile_e, tile_v, xs, wg, wu, wd, roww)


R_COMB = 256


def _combine_kernel(pos_ref, x_ref, ml_ref, mc_ref, y_hbm, o_ref, buf0, buf1, sem, *, i_off, rows_per_b):
    b = pl.program_id(0)
    i = pl.program_id(1)
    is_ctx = (i + i_off) == 0
    base = (b * rows_per_b + i * R_COMB) * 2

    def issue(r, carry):
        _token_copy(y_hbm, pos_ref[base + 2 * r], buf0, r, sem.at[0]).start()
        _token_copy(y_hbm, pos_ref[base + 2 * r + 1], buf1, r, sem.at[1]).start()
        return carry

    lax.fori_loop(0, R_COMB, issue, 0, unroll=8)

    def drain(r, carry):
        _token_copy(y_hbm, 0, buf0, r, sem.at[0]).wait()
        _token_copy(y_hbm, 0, buf1, r, sem.at[1]).wait()
        return carry

    lax.fori_loop(0, R_COMB, drain, 0, unroll=8)
    gf = jnp.where(is_ctx, mc_ref[:, 5 * D:6 * D], ml_ref[0, :, 5 * D:6 * D])
    for s in range(TOK_SUB):
        cs = slice(s * 128, (s + 1) * 128)
        y = buf0[pl.ds(s, R_COMB, stride=TOK_SUB), :] + buf1[pl.ds(s, R_COMB, stride=TOK_SUB), :]
        o_ref[0, :, cs] = x_ref[0, :, cs] + gf[:, cs] * y


def moe_combine(pos, xall, ml, mc, y_sorted, with_ctx):
    i_off = 0 if with_ctx else LC // R_COMB
    rows_out = xall.shape[1]
    nblk = rows_out // R_COMB
    return pl.pallas_call(
        functools.partial(_combine_kernel, i_off=i_off, rows_per_b=rows_out),
        grid_spec=pltpu.PrefetchScalarGridSpec(
            num_scalar_prefetch=1,
            grid=(B, nblk),
            in_specs=[pl.BlockSpec((1, R_COMB, D), lambda b, i, p: (b, i, 0)),
                      pl.BlockSpec((1, 1, 6 * D), lambda b, i, p: (b, 0, 0)),
                      pl.BlockSpec((1, 6 * D), lambda b, i, p: (0, 0)),
                      pl.BlockSpec(memory_space=pl.ANY)],
            out_specs=pl.BlockSpec((1, R_COMB, D), lambda b, i, p: (b, i, 0)),
            scratch_shapes=[pltpu.VMEM((R_COMB * TOK_SUB, 128), F32), pltpu.VMEM((R_COMB * TOK_SUB, 128), F32),
                            pltpu.SemaphoreType.DMA((2,))]),
        out_shape=jax.ShapeDtypeStruct((B, rows_out, D), F32),
        compiler_params=_cparams(("arbitrary", "arbitrary")),
    )(pos, xall, ml, mc, y_sorted)


def route(logits, router_bias):
    T = logits.shape[0]
    probs = jax.nn.softmax(logits, axis=-1)
    sel = (probs + router_bias.astype(F32)).reshape(T, N_GROUPS, EPG)
    group = jnp.argmax(jnp.max(sel, axis=-1), axis=-1)
    sel_g = jnp.take_along_axis(sel, group[:, None, None], axis=1)[:, 0]
    _, idx = lax.top_k(sel_g, TOP_K)
    expert = (group[:, None] * EPG + idx).astype(jnp.int32)
    wts = jnp.take_along_axis(probs, expert, axis=-1)
    wts = wts / jnp.sum(wts, axis=-1, keepdims=True)
    flat_e = expert.reshape(-1)
    onehot = (flat_e[:, None] == jnp.arange(N_EXPERTS, dtype=jnp.int32)[None, :]).astype(jnp.int32)
    csum = jnp.cumsum(onehot, axis=0)
    rank = jnp.sum(csum * onehot, axis=-1) - 1
    counts = csum[-1]
    ptiles = (counts + TM_MOE - 1) // TM_MOE
    tend = jnp.cumsum(ptiles)
    tstart = tend - ptiles
    dest = (tstart[flat_e] * TM_MOE + rank).astype(jnp.int32)
    ntiles = -(-(TOP_K * T) // TM_MOE) + N_EXPERTS
    nrows = ntiles * TM_MOE
    tid = jnp.arange(ntiles, dtype=jnp.int32)
    tile_e = jnp.minimum(jnp.searchsorted(tend, tid, side="right"), N_EXPERTS - 1).astype(jnp.int32)
    tile_v = (tid < tend[-1]).astype(jnp.int32)
    src = jnp.zeros((nrows,), jnp.int32).at[dest].set(jnp.repeat(jnp.arange(T, dtype=jnp.int32), TOP_K))
    roww = jnp.zeros((nrows, 1), F32).at[dest, 0].set(wts.reshape(-1))
    return src, roww, dest, tile_e, tile_v


GLA_QW = GLA_HEADS * GLA_DK
N_BLK = L // GLA_CHUNK
N_BLK_C = LC // GLA_CHUNK


def _log_sigmoid(z):
    return jnp.minimum(z, 0.0) - jnp.log1p(jnp.exp(-jnp.abs(z)))


def _head_rms(x, w):
    R, W = x.shape
    lane = lax.broadcasted_iota(jnp.int32, (R, 128), 1)
    left = lane < 64
    outs = []
    for c in range(W // 128):
        xc = x[:, c * 128:(c + 1) * 128]
        sq = xc * xc
        sl = jnp.sum(jnp.where(left, sq, 0.0), axis=-1, keepdims=True)
        sr = jnp.sum(jnp.where(left, 0.0, sq), axis=-1, keepdims=True)
        inv = jnp.where(left, lax.rsqrt(sl * (1.0 / 64) + EPS), lax.rsqrt(sr * (1.0 / 64) + EPS))
        outs.append(xc * inv * w)
    return jnp.concatenate(outs, axis=-1)


def _gla_kernel(q_ref, k_ref, v_ref, g_ref, a_ref, wf_ref, wb_ref, gb_ref, nw_ref, ind_ref, bdm_ref, o_ref,
                lf_ref, lb_ref, acc_ref, sf_ref, sb_ref, *, row_off):
    hi = lax.Precision.HIGHEST
    a = a_ref[0].astype(F32)
    lf_ref[...] = _log_sigmoid(jnp.dot(a, wf_ref[...], preferred_element_type=F32, precision=hi)
                               + gb_ref[0:1, :]) * (1.0 / GLA_TEMP)
    lb_ref[...] = _log_sigmoid(jnp.dot(a, wb_ref[...], preferred_element_type=F32, precision=hi)
                               + gb_ref[1:2, :]) * (1.0 / GLA_TEMP)
    acc_ref[...] = jnp.zeros_like(acc_ref)
    sf_ref[...] = jnp.zeros_like(sf_ref)
    sb_ref[...] = jnp.zeros_like(sb_ref)

    C = GLA_CHUNK
    r16 = lax.broadcasted_iota(jnp.int32, (C, C), 0)
    c16 = lax.broadcasted_iota(jnp.int32, (C, C), 1)
    rowi = lax.broadcasted_iota(jnp.int32, (C, GLA_QW), 0)

    def block(r0, la_ref, s_ref, forward):
        tri = ((r16 >= c16) if forward else (r16 <= c16)).astype(F32)
        q = q_ref[0, pl.ds(r0, C), :].astype(F32) * (GLA_DK ** -0.5)
        k = k_ref[0, pl.ds(r0, C), :].astype(F32)
        v = v_ref[0, pl.ds(r0, C), :].astype(F32)
        la = la_ref[pl.ds(r0, C), :]
        b = jnp.dot(tri, la, preferred_element_type=F32, precision=hi)
        btot = jnp.sum(la, axis=0, keepdims=True)
        st = s_ref[...]
        o = lax.dot_general((q * jnp.exp(b)).astype(BF16), st.astype(BF16), (((1,), (1,)), ((), ())),
                            preferred_element_type=F32)
        ps = []
        for s in range(C):
            seen = (rowi >= s) if forward else (rowi <= s)
            e = jnp.exp(jnp.where(seen, b - b[s:s + 1, :], -jnp.inf))
            ps.append(q * k[s:s + 1, :] * e)
        pm = jnp.concatenate(ps, axis=0).astype(BF16)
        rm = jnp.dot(pm, ind_ref[...], preferred_element_type=F32)
        for s in range(C):
            o = o + rm[s * C:(s + 1) * C, :] * v[s:s + 1, :]
        acc_ref[pl.ds(r0, C), :] += o
        kd = (k * jnp.exp(btot - b)).astype(BF16)
        upd = lax.dot_general(v.astype(BF16), kd, (((0,), (0,)), ((), ())), preferred_element_type=F32)
        s_ref[...] = st * jnp.exp(btot) + upd * bdm_ref[...]

    def body(i, carry):
        block(pl.multiple_of(i * C, C), lf_ref, sf_ref, True)
        jb = jnp.where(i < N_BLK_C, N_BLK_C - 1 - i, N_BLK + N_BLK_C - 1 - i)
        block(pl.multiple_of(jb * C, C), lb_ref, sb_ref, False)
        return carry

    lax.fori_loop(0, N_BLK, body, 0)
    o = _head_rms(acc_ref[row_off:, :], nw_ref[...])
    g = g_ref[0, row_off:, :].astype(F32)
    o_ref[0] = (o * (g / (1.0 + jnp.exp(-g)))).astype(o_ref.dtype)


def gla_mixer(u, gate_w, gate_b, norm_w, with_ctx):
    row_off = 0 if with_ctx else LC
    qc, kc = U_OFF["gla_q"] // 256, U_OFF["gla_k"] // 256
    vc, gc, ac = U_OFF["gla_v"] // 512, U_OFF["gla_g"] // 512, U_OFF["gla_a"] // 128
    wf = jnp.zeros((128, GLA_QW), F32).at[0:GLA_RANK].set(gate_w[0])
    wb = jnp.zeros((128, GLA_QW), F32).at[GLA_RANK:2 * GLA_RANK].set(gate_w[1])
    hd = np.arange(GLA_QW)[:, None] // GLA_DK == np.arange(GROUP_W)[None, :] // GLA_DV
    ind = jnp.asarray(hd, BF16)
    bdm = jnp.asarray(hd.T, F32)
    full = lambda *shape: pl.BlockSpec(shape, lambda b: (0,) * len(shape))
    return pl.pallas_call(
        functools.partial(_gla_kernel, row_off=row_off),
        grid=(B,),
        in_specs=[pl.BlockSpec((1, L, 256), lambda b: (b, 0, qc)),
                  pl.BlockSpec((1, L, 256), lambda b: (b, 0, kc)),
                  pl.BlockSpec((1, L, 512), lambda b: (b, 0, vc)),
                  pl.BlockSpec((1, L, 512), lambda b: (b, 0, gc)),
                  pl.BlockSpec((1, L, 128), lambda b: (b, 0, ac)),
                  full(128, GLA_QW), full(128, GLA_QW), full(2, GLA_QW), full(1, 128),
                  full(GLA_QW, GROUP_W), full(GROUP_W, GLA_QW)],
        out_specs=pl.BlockSpec((1, L - row_off, GROUP_W), lambda b: (b, 0, 0)),
        out_shape=jax.ShapeDtypeStruct((B, L - row_off, GROUP_W), BF16),
        scratch_shapes=[pltpu.VMEM((L, GLA_QW), F32), pltpu.VMEM((L, GLA_QW), F32), pltpu.VMEM((L, GROUP_W), F32),
                        pltpu.VMEM((GROUP_W, GLA_QW), F32), pltpu.VMEM((GROUP_W, GLA_QW), F32)],
        compiler_params=_cparams(("parallel",)),
    )(u, u, u, u, u, wf, wb, gate_b, jnp.tile(norm_w, 2)[None], ind, bdm)


HY_CT = 256
HY_TK = 512


@functools.lru_cache(maxsize=None)
def _dft_consts(Lh):
    k = np.arange(Lh, dtype=np.int64)
    ph = (np.outer(k, k) % (2 * Lh)).astype(np.float64) * (np.pi / Lh)
    sgn = (1.0 - 2.0 * (k % 2)).astype(np.float32)[:, None]
    return np.cos(ph).astype(np.float32), np.sin(ph).astype(np.float32), sgn


def _hyena_features(Lh):
    t = jnp.linspace(0.0, 1.0, Lh, dtype=F32)[:, None]
    w = (2.0 * math.pi / Lh) * jnp.arange(Lh, dtype=F32)[:, None]
    bands = jnp.linspace(1e-4, HY_BANDS - 1, HY_BANDS, dtype=F32)
    z = jnp.concatenate([t, jnp.cos(w * bands), -jnp.sin(w * bands)], axis=-1)
    return jnp.pad(z, ((0, 0), (0, 128 - z.shape[1])))


def _hy_filter_kernel(z_ref, w1_ref, b1_ref, w2_ref, b2_ref, fr_ref, w3f_ref, w3b_ref, dl_ref, c_ref, s_ref, sgn_ref,
                      hc_ref, hs_ref, hn_ref, *, Lh):
    hi = lax.Precision.HIGHEST
    z = z_ref[...]
    h = jnp.sin(fr_ref[0:1, :] * (jnp.dot(z, w1_ref[...], preferred_element_type=F32, precision=hi) + b1_ref[...]))
    h = jnp.sin(fr_ref[1:2, :] * (jnp.dot(h, w2_ref[...], preferred_element_type=F32, precision=hi) + b2_ref[...]))
    dec = jnp.exp(-(z[:, 0:1] * dl_ref[...]))
    hf = jnp.dot(h, w3f_ref[...], preferred_element_type=F32, precision=hi) * dec
    hb = jnp.dot(h, w3b_ref[...], preferred_element_type=F32, precision=hi) * dec
    nrm = jnp.sum(jnp.abs(hf), axis=0, keepdims=True) + jnp.sum(jnp.abs(hb), axis=0, keepdims=True)
    row = lax.broadcasted_iota(jnp.int32, (Lh, 1), 0)
    hf = hf / nrm
    hb = jnp.where(row == 0, 0.0, hb / nrm)
    wk = jnp.where(row == 0, 0.5 / Lh, 1.0 / Lh)

    def project(m_ref, x):
        xh = x.astype(BF16)
        xl = (x - xh.astype(F32)).astype(BF16)
        return (jnp.dot(m_ref[...], xh, preferred_element_type=F32)
                + jnp.dot(m_ref[...], xl, preferred_element_type=F32))

    am = hf + hb
    hc_ref[0] = project(c_ref, am) * wk
    hs_ref[0] = project(s_ref, hf - hb) * wk
    hn_ref[0] = jnp.sum(am * sgn_ref[...], axis=0, keepdims=True) * (0.5 / Lh)


def hyena_spectrum(Lh, w1, b1, w2, b2, w3, fr, cmat, smat, sgn):
    z = _hyena_features(Lh)
    w1p = jnp.pad(w1, ((0, 128 - w1.shape[0]), (0, 0)))
    deltas = jnp.linspace(HY_MIN_DECAY, HY_MAX_DECAY, HY_CH, dtype=F32)[None]
    nct = HY_CH // HY_CT
    full = lambda *shape: pl.BlockSpec(shape, lambda o, c: (0,) * len(shape))
    return pl.pallas_call(
        functools.partial(_hy_filter_kernel, Lh=Lh),
        grid=(HY_ORDER, nct),
        in_specs=[full(Lh, 128), full(128, 64), full(1, 64), full(64, 64), full(1, 64), full(2, 64),
                  pl.BlockSpec((64, HY_CT), lambda o, c: (0, o * 2 * nct + c)),
                  pl.BlockSpec((64, HY_CT), lambda o, c: (0, o * 2 * nct + nct + c)),
                  pl.BlockSpec((1, HY_CT), lambda o, c: (0, c)),
                  full(Lh, Lh), full(Lh, Lh), full(Lh, 1)],
        out_specs=[pl.BlockSpec((1, Lh, HY_CT), lambda o, c: (o, 0, c)),
                   pl.BlockSpec((1, Lh, HY_CT), lambda o, c: (o, 0, c)),
                   pl.BlockSpec((1, 1, HY_CT), lambda o, c: (o, 0, c))],
        out_shape=[jax.ShapeDtypeStruct((HY_ORDER, Lh, HY_CH), F32),
                   jax.ShapeDtypeStruct((HY_ORDER, Lh, HY_CH), F32),
                   jax.ShapeDtypeStruct((HY_ORDER, 1, HY_CH), F32)],
        compiler_params=_cparams(("arbitrary", "arbitrary")),
    )(z, w1p, b1[None], w2, b2[None], fr, w3, w3, deltas, cmat, smat, sgn)


def _hyena_seq(row0, Lh, x1_ref, x2_ref, y_ref, cw_refs, cb_refs, bias_ref, c_ref, s_ref, hc_ref, hs_ref, hn_ref, *,
               scratch):
    r = lax.broadcasted_iota(jnp.int32, (Lh, 1), 0)

    def sconv(u_ref, w_ref, b_ref):
        u = u_ref[0, row0:row0 + Lh, :].astype(F32)
        up = jnp.where(r == 0, 0.0, pltpu.roll(u, 1, 0))
        dn = jnp.where(r == Lh - 1, 0.0, pltpu.roll(u, Lh - 1, 0))
        return up * w_ref[0:1, :] + u * w_ref[1:2, :] + dn * w_ref[2:3, :] + b_ref[...]

    gate_refs = (x1_ref, x2_ref)
    ys_ref, yb_ref, cv_ref = scratch
    rows = slice(0, Lh)
    ys_ref[rows, :] = sconv(y_ref, cw_refs[2], cb_refs[2])
    sgn = (1 - 2 * (r & 1)).astype(F32)
    tk = min(HY_TK, Lh)
    for o in range(HY_ORDER):
        y = ys_ref[rows, :]
        yb_ref[rows, :] = y.astype(BF16)
        cv_ref[rows, :] = sgn * (jnp.sum(y * sgn, axis=0, keepdims=True) * hn_ref[o])
        for m in range(Lh // tk):
            fs = slice(m * tk, (m + 1) * tk)
            yc = jnp.dot(c_ref[fs, :], yb_ref[rows, :], preferred_element_type=F32)
            ysn = jnp.dot(s_ref[fs, :], yb_ref[rows, :], preferred_element_type=F32)
            hc = hc_ref[o, fs, :]
            hs = hs_ref[o, fs, :]
            pc = (yc * hc - ysn * hs).astype(BF16)
            ps = (yc * hs + ysn * hc).astype(BF16)
            cv_ref[rows, :] += (jnp.dot(c_ref[:, fs], pc, preferred_element_type=F32)
                                + jnp.dot(s_ref[:, fs], ps, preferred_element_type=F32))
        ys_ref[rows, :] = (sconv(gate_refs[o], cw_refs[o], cb_refs[o])
                           * (cv_ref[rows, :] + ys_ref[rows, :] * bias_ref[o:o + 1, :]))
    return ys_ref[rows, :]


def _hyena_kernel(x1_ref, x2_ref, y_ref, w1_ref, w2_ref, w3_ref, b1_ref, b2_ref, b3_ref, bias_ref, *rest, with_ctx):
    nmat = 10 if with_ctx else 5
    o_ref = rest[nmat]
    scratch = rest[nmat + 1:]
    seqs = [(LC, S, rest[0:5])]
    if with_ctx:
        seqs.append((0, LC, rest[5:10]))
    for row0, Lh, mats in seqs:
        y = _hyena_seq(row0, Lh, x1_ref, x2_ref, y_ref, (w1_ref, w2_ref, w3_ref), (b1_ref, b2_ref, b3_ref),
                       bias_ref, *mats, scratch=scratch)
        out0 = row0 if with_ctx else 0
        o_ref[0, out0:out0 + Lh, :] = y.astype(o_ref.dtype)


def hyena_mixer(u, conv_w, conv_b, w1, b1, w2, b2, w3, fr, bias, with_ctx):
    nct = HY_CH // HY_CT
    hc0 = U_OFF["hy"] // HY_CT
    once = pl.Buffered(1)
    consts = []
    specs = []
    for Lh in ((S, LC) if with_ctx else (S,)):
        cm, sm, sgn = _dft_consts(Lh)
        cm, sm, sgn = jnp.asarray(cm, BF16), jnp.asarray(sm, BF16), jnp.asarray(sgn)
        hc, hs, hn = hyena_spectrum(Lh, w1, b1, w2, b2, w3, fr, cm, sm, sgn)
        consts += [cm, sm, hc, hs, hn]
        specs += [pl.BlockSpec((Lh, Lh), lambda c, b: (0, 0), pipeline_mode=once),
                  pl.BlockSpec((Lh, Lh), lambda c, b: (0, 0), pipeline_mode=once),
                  pl.BlockSpec((HY_ORDER, Lh, HY_CT), lambda c, b: (0, 0, c), pipeline_mode=once),
                  pl.BlockSpec((HY_ORDER, Lh, HY_CT), lambda c, b: (0, 0, c), pipeline_mode=once),
                  pl.BlockSpec((HY_ORDER, 1, HY_CT), lambda c, b: (0, 0, c), pipeline_mode=once)]
    rows_out = L if with_ctx else S
    ublk = lambda j: pl.BlockSpec((1, L, HY_CT), lambda c, b: (b, 0, hc0 + j * nct + c))
    wblk = lambda j: pl.BlockSpec((3, HY_CT), lambda c, b: (0, j * nct + c))
    bblk = lambda j: pl.BlockSpec((1, HY_CT), lambda c, b: (0, j * nct + c))
    return pl.pallas_call(
        functools.partial(_hyena_kernel, with_ctx=with_ctx),
        grid=(nct, B),
        in_specs=[ublk(0), ublk(1), ublk(2), wblk(0), wblk(1), wblk(2), bblk(0), bblk(1), bblk(2),
                  pl.BlockSpec((HY_ORDER, HY_CT), lambda c, b: (0, c))] + specs,
        out_specs=pl.BlockSpec((1, rows_out, HY_CT), lambda c, b: (b, 0, c)),
        out_shape=jax.ShapeDtypeStruct((B, rows_out, HY_CH), BF16),
        scratch_shapes=[pltpu.VMEM((S, HY_CT), F32), pltpu.VMEM((S, HY_CT), BF16), pltpu.VMEM((S, HY_CT), F32)],
        compiler_params=_cparams(("arbitrary", "arbitrary")),
    )(u, u, u, conv_w, conv_w, conv_w, conv_b[None], conv_b[None], conv_b[None], bias, *consts)


def _permute_w_in(w):
    parts = []
    for n in U_ORDER:
        o, wd = REF_COLS[n]
        p = w[:, o:o + wd]
        if wd < 128:
            p = jnp.pad(p, ((0, 0), (0, 128 - wd)))
        parts.append(p)
    parts.append(jnp.zeros((w.shape[0], U_W - U_USED), w.dtype))
    return jnp.concatenate(parts, axis=1).astype(BF16)


def kernel(x, c, ctx, c_ctx, norm1_w, norm2_w, ada_w, ada_b, w_in, w_out, gla_gate_w, gla_gate_b, gla_norm_w,
           swa_q_norm_w, swa_k_norm_w, swa_sink, hyena_conv_w, hyena_conv_b, hyena_ffn_w1, hyena_ffn_b1,
           hyena_ffn_w2, hyena_ffn_b2, hyena_ffn_w3, hyena_ffn_freq, hyena_bias, diff_q_norm_w, diff_k_norm_w,
           diff_lambda, diff_subln_w, router_w, router_bias, expert_w_gate, expert_w_up, expert_w_down):
    assert x.shape == (B, S, D) and ctx.shape == (B, LC, D)
    cc = jnp.zeros((16, D), F32).at[:B].set(c).at[B].set(c_ctx)
    mods = ada_mod(cc, ada_w, ada_b)
    cos, sin = rope_tables128()
    rw_pad = jnp.pad(router_w, ((0, 0), (0, 128 - N_EXPERTS)))
    xall = jnp.concatenate([ctx, x], axis=1)

    for l in range(DEPTH):
        with_ctx = l < DEPTH - 1
        lam_init = 0.8 - 0.6 * math.exp(-0.3 * l)
        ml = mods[l, :B].reshape(B, 1, 6 * D)
        mc = mods[l, B:B + 1]
        u = in_proj(xall, norm1_w[l][None], ml, mc, _permute_w_in(w_in[l]))
        mix_a = gla_mixer(u, gla_gate_w[l], gla_gate_b[l], gla_norm_w[l], with_ctx)
        mix_b = swa_mixer(u, swa_sink[l], swa_q_norm_w[l], swa_k_norm_w[l], cos, sin, with_ctx)
        mix_y = hyena_mixer(u, hyena_conv_w[l], hyena_conv_b[l], hyena_ffn_w1[l], hyena_ffn_b1[l],
                            hyena_ffn_w2[l], hyena_ffn_b2[l], hyena_ffn_w3[l], hyena_ffn_freq[l],
                            hyena_bias[l], with_ctx)
        mix_d = diff_mixer(u, diff_q_norm_w[l], diff_k_norm_w[l], cos, sin, diff_lambda[l], diff_subln_w[l],
                           lam_init, with_ctx)
        xall, f, logits = out_proj(xall, (mix_a, mix_b, mix_y, mix_d), w_out[l].astype(BF16), norm2_w[l][None],
                                   ml, mc, rw_pad, with_ctx)
        T = B * xall.shape[1]
        src, roww, dest, tile_e, tile_v = route(logits.reshape(T, 128)[:, :N_EXPERTS], router_bias)
        xs = gather_rows(src, f.reshape(T * TOK_SUB, 128), src.shape[0])
        ys = moe_experts(tile_e, tile_v, xs, expert_w_gate[l].astype(BF16), expert_w_up[l].astype(BF16),
                         expert_w_down[l].astype(BF16), roww)
        xall = moe_combine(dest, xall, ml, mc, ys, with_ctx)
    return xall
```

```python
import functools
import math

import numpy as np
import jax
import jax.numpy as jnp
from jax import lax
from jax.experimental import pallas as pl
from jax.experimental.pallas import tpu as pltpu

F32 = jnp.float32
BF16 = jnp.bfloat16

D = 2048
B = 8
S = 2048
LC = 256
L = LC + S
DEPTH = 2
GRID_W = 64
HEAD_DIM = 64
ROPE_THETA = 10000.0
EPS = 1e-6
GROUP_W = 512

GLA_DV = 64
GLA_DK = 32
GLA_HEADS = 8
GLA_RANK = 16
GLA_TEMP = 16.0
GLA_CHUNK = 16
SWA_HEADS = 8
SWA_KV = 2
SWA_WINDOW = 128
HY_CH = 512
HY_ORDER = 2
HY_BANDS = 16
HY_TARGET = 1e-2
HY_MIN_DECAY = math.log(1.0 / HY_TARGET) / 1.5
HY_MAX_DECAY = math.log(1.0 / HY_TARGET) / 0.3
DIFF_HEADS = 4
N_EXPERTS = 16
N_GROUPS = 4
EPG = 4
TOP_K = 2
D_EXPERT = 1024

REF_COLS = dict(gla_q=(0, 256), gla_k=(256, 256), gla_v=(512, 512), gla_g=(1024, 512), gla_a=(1536, 32),
                swa_q=(1568, 512), swa_k=(2080, 128), swa_v=(2208, 128), hy=(2336, 1536),
                diff_q=(3872, 512), diff_k=(4384, 512), diff_v=(4896, 512))
U_ORDER = ("hy", "gla_v", "gla_g", "swa_q", "diff_q", "diff_k", "diff_v", "gla_q", "gla_k", "swa_k", "swa_v", "gla_a")
U_OFF = {}
_o = 0
for _n in U_ORDER:
    U_OFF[_n] = _o
    _o += max(REF_COLS[_n][1], 128)
U_USED = _o
TN_IN = 512
U_W = -(-U_USED // TN_IN) * TN_IN

VMEM_LIMIT = 56 * 1024 * 1024


def _cparams(sem):
    return pltpu.CompilerParams(dimension_semantics=sem, vmem_limit_bytes=VMEM_LIMIT)


def _ada_kernel(c_ref, w_ref, b_ref, o_ref):
    c = c_ref[...]
    a = c / (1.0 + jnp.exp(-c))
    o_ref[0] = jnp.dot(a, w_ref[0], preferred_element_type=F32, precision=lax.Precision.HIGHEST) + b_ref[0]


def ada_mod(cc, ada_w, ada_b):
    tn = 1024
    return pl.pallas_call(
        _ada_kernel,
        grid=(DEPTH, 6 * D // tn),
        in_specs=[pl.BlockSpec((16, D), lambda l, j: (0, 0)),
                  pl.BlockSpec((1, D, tn), lambda l, j: (l, 0, j)),
                  pl.BlockSpec((1, 1, tn), lambda l, j: (l, 0, j))],
        out_specs=pl.BlockSpec((1, 16, tn), lambda l, j: (l, 0, j)),
        out_shape=jax.ShapeDtypeStruct((DEPTH, 16, 6 * D), F32),
        compiler_params=_cparams(("parallel", "parallel")),
    )(cc, ada_w, ada_b.reshape(DEPTH, 1, 6 * D))


TM_IN = 1152
CH_IN = 32


def _inproj_kernel(x_ref, nw_ref, scl_ref, shl_ref, scc_ref, shc_ref, w_ref, o_ref, h_ref):
    i = pl.program_id(1)

    @pl.when(pl.program_id(2) == 0)
    def _():
        def chunk(c, carry):
            r0 = pl.multiple_of(c * CH_IN, CH_IN)
            x = x_ref[0, pl.ds(r0, CH_IN), :]
            ms = jnp.mean(x * x, axis=-1, keepdims=True)
            y = x * lax.rsqrt(ms + EPS) * nw_ref[...]
            row = i * TM_IN + r0 + lax.broadcasted_iota(jnp.int32, (CH_IN, 1), 0)
            is_ctx = row < LC
            sc = jnp.where(is_ctx, scc_ref[...], scl_ref[0])
            sh = jnp.where(is_ctx, shc_ref[...], shl_ref[0])
            h_ref[pl.ds(r0, CH_IN), :] = (y * (1.0 + sc) + sh).astype(BF16)
            return carry

        lax.fori_loop(0, TM_IN // CH_IN, chunk, 0)

    o_ref[0] = jnp.dot(h_ref[...], w_ref[...], preferred_element_type=F32).astype(o_ref.dtype)


def in_proj(xall, nw, ml, mc, w_in_p):
    return pl.pallas_call(
        _inproj_kernel,
        grid=(B, L // TM_IN, U_W // TN_IN),
        in_specs=[pl.BlockSpec((1, TM_IN, D), lambda b, i, j: (b, i, 0)),
                  pl.BlockSpec((1, D), lambda b, i, j: (0, 0)),
                  pl.BlockSpec((1, 1, D), lambda b, i, j: (b, 0, 1)),
                  pl.BlockSpec((1, 1, D), lambda b, i, j: (b, 0, 0)),
                  pl.BlockSpec((1, D), lambda b, i, j: (0, 1)),
                  pl.BlockSpec((1, D), lambda b, i, j: (0, 0)),
                  pl.BlockSpec((D, TN_IN), lambda b, i, j: (0, j))],
        out_specs=pl.BlockSpec((1, TM_IN, TN_IN), lambda b, i, j: (b, i, j)),
        out_shape=jax.ShapeDtypeStruct((B, L, U_W), BF16),
        scratch_shapes=[pltpu.VMEM((TM_IN, D), BF16)],
        compiler_params=_cparams(("parallel", "parallel", "arbitrary")),
    )(xall, nw, ml, ml, mc, mc, w_in_p)


def _norm_rope(x, w, cos, sin):
    R, W = x.shape
    lane = lax.broadcasted_iota(jnp.int32, (R, 128), 1)
    left = lane < HEAD_DIM
    first = (lane & 31) < 16
    outs = []
    for c in range(W // 128):
        xc = x[:, c * 128:(c + 1) * 128]
        sq = xc * xc
        sl = jnp.sum(jnp.where(left, sq, 0.0), axis=-1, keepdims=True)
        sr = jnp.sum(jnp.where(left, 0.0, sq), axis=-1, keepdims=True)
        inv = jnp.where(left, lax.rsqrt(sl * (1.0 / HEAD_DIM) + EPS), lax.rsqrt(sr * (1.0 / HEAD_DIM) + EPS))
        y = xc * inv * w
        rot = jnp.where(first, pltpu.roll(y, 112, 1), pltpu.roll(y, 16, 1))
        outs.append(y * cos + rot * sin)
    return outs[0] if len(outs) == 1 else jnp.concatenate(outs, axis=-1)


def rope_tables128():
    n = HEAD_DIM // 4
    rows = S // GRID_W
    row_ids = jnp.repeat(jnp.arange(rows), GRID_W).astype(F32)
    col_ids = jnp.tile(jnp.arange(GRID_W), rows).astype(F32)
    inv = ROPE_THETA ** (-jnp.arange(n, dtype=F32) / n)
    ar = row_ids[:, None] * inv
    ac = col_ids[:, None] * inv
    cos64 = jnp.concatenate([jnp.cos(ar), jnp.cos(ar), jnp.cos(ac), jnp.cos(ac)], axis=-1)
    sin64 = jnp.concatenate([-jnp.sin(ar), jnp.sin(ar), -jnp.sin(ac), jnp.sin(ac)], axis=-1)
    cos = jnp.concatenate([jnp.ones((LC, 64), F32), cos64], axis=0)
    sin = jnp.concatenate([jnp.zeros((LC, 64), F32), sin64], axis=0)
    return jnp.tile(cos, (1, 2)), jnp.tile(sin, (1, 2))


SWA_BAND = 3 * SWA_WINDOW


def _swa_kernel(sink_ref, q_ref, k_ref, v_ref, qw_ref, kw_ref, cos_ref, sin_ref, o_ref, kn_ref, *, n_off):
    n = pl.program_id(1) + n_off

    @pl.when(pl.program_id(1) == 0)
    def _():
        kn_ref[...] = _norm_rope(k_ref[0].astype(F32), kw_ref[...], cos_ref[...], sin_ref[...]).astype(BF16)

    r0 = pl.multiple_of(n * 128, 128)
    q = _norm_rope(q_ref[0].astype(F32), qw_ref[...], cos_ref[pl.ds(r0, 128), :], sin_ref[pl.ds(r0, 128), :])
    q = (q * (HEAD_DIM ** -0.5)).astype(BF16)

    nb = n - LC // 128
    is_lat = nb >= 0
    lstart = jnp.clip((nb - 1) * 128, 0, S - SWA_BAND)
    start = pl.multiple_of(LC + lstart, 128)
    kk = jnp.concatenate([kn_ref[0:LC, :], kn_ref[pl.ds(start, SWA_BAND), :]], axis=0)
    vv = jnp.concatenate([v_ref[0, 0:LC, :], v_ref[0, pl.ds(start, SWA_BAND), :]], axis=0)
    nk = LC + SWA_BAND
    row = lax.broadcasted_iota(jnp.int32, (512, nk), 0) & 127
    col = lax.broadcasted_iota(jnp.int32, (512, nk), 1)
    qpos = nb * 128 + row
    kpos = lstart + col - LC
    valid = (col < LC) | ((jnp.abs(qpos - kpos) <= SWA_WINDOW) & is_lat)
    rowi = lax.broadcasted_iota(jnp.int32, (512, 1), 0)
    outs = [None] * SWA_HEADS
    for hk in range(SWA_KV):
        qs = jnp.concatenate([q[:, (hk * 4 + g) * 64:(hk * 4 + g + 1) * 64] for g in range(4)], axis=0)
        s = lax.dot_general(qs, kk[:, hk * 64:(hk + 1) * 64], (((1,), (1,)), ((), ())),
                            preferred_element_type=F32)
        s = jnp.where(valid, s, -jnp.inf)
        sk = jnp.where(rowi < 128, sink_ref[hk * 4],
                       jnp.where(rowi < 256, sink_ref[hk * 4 + 1],
                                 jnp.where(rowi < 384, sink_ref[hk * 4 + 2], sink_ref[hk * 4 + 3])))
        m = jnp.maximum(jnp.max(s, axis=-1, keepdims=True), sk)
        e = jnp.exp(s - m)
        den = jnp.sum(e, axis=-1, keepdims=True) + jnp.exp(sk - m)
        o = jnp.dot(e.astype(BF16), vv[:, hk * 64:(hk + 1) * 64], preferred_element_type=F32) / den
        for g in range(4):
            outs[hk * 4 + g] = o[g * 128:(g + 1) * 128]
    o_ref[0] = jnp.concatenate(outs, axis=-1).astype(o_ref.dtype)


def swa_mixer(u, sink, qw, kw, cos, sin, with_ctx):
    n_off = 0 if with_ctx else LC // 128
    nblk = L // 128 - n_off
    qc, kc, vc = U_OFF["swa_q"] // 512, U_OFF["swa_k"] // 128, U_OFF["swa_v"] // 128
    return pl.pallas_call(
        functools.partial(_swa_kernel, n_off=n_off),
        grid_spec=pltpu.PrefetchScalarGridSpec(
            num_scalar_prefetch=1,
            grid=(B, nblk),
            in_specs=[pl.BlockSpec((1, 128, 512), lambda b, n, s: (b, n + n_off, qc)),
                      pl.BlockSpec((1, L, 128), lambda b, n, s: (b, 0, kc)),
                      pl.BlockSpec((1, L, 128), lambda b, n, s: (b, 0, vc)),
                      pl.BlockSpec((1, 128), lambda b, n, s: (0, 0)),
                      pl.BlockSpec((1, 128), lambda b, n, s: (0, 0)),
                      pl.BlockSpec((L, 128), lambda b, n, s: (0, 0)),
                      pl.BlockSpec((L, 128), lambda b, n, s: (0, 0))],
            out_specs=pl.BlockSpec((1, 128, 512), lambda b, n, s: (b, n, 0)),
            scratch_shapes=[pltpu.VMEM((L, 128), BF16)]),
        out_shape=jax.ShapeDtypeStruct((B, nblk * 128, GROUP_W), BF16),
        compiler_params=_cparams(("parallel", "arbitrary")),
    )(sink, u, u, u, jnp.tile(qw, 2)[None], jnp.tile(kw, 2)[None], cos, sin)


TQ_DIFF = 256


def _diff_kernel(q_ref, k_ref, v_ref, qw_ref, kw_ref, cos_ref, sin_ref, dl_ref, sw_ref, o_ref, kn_ref,
                 *, j_off, lam_init):
    j = pl.program_id(2) + j_off

    @pl.when(pl.program_id(2) == 0)
    def _():
        kn_ref[...] = _norm_rope(k_ref[0].astype(F32), kw_ref[...], cos_ref[...], sin_ref[...]).astype(BF16)

    dl = dl_ref[...]
    lam = (jnp.exp(jnp.sum(dl[0:1] * dl[1:2], axis=-1, keepdims=True))
           - jnp.exp(jnp.sum(dl[2:3] * dl[3:4], axis=-1, keepdims=True)) + lam_init)
    r0 = pl.multiple_of(j * TQ_DIFF, TQ_DIFF)
    q = _norm_rope(q_ref[0].astype(F32), qw_ref[...], cos_ref[pl.ds(r0, TQ_DIFF), :], sin_ref[pl.ds(r0, TQ_DIFF), :])
    q = (q * (HEAD_DIM ** -0.5)).astype(BF16)

    def attend(nk):
        os_ = []
        for m in range(2):
            s = lax.dot_general(q[:, m * 64:(m + 1) * 64], kn_ref[0:nk, m * 64:(m + 1) * 64],
                                (((1,), (1,)), ((), ())), preferred_element_type=F32)
            e = jnp.exp(s - jnp.max(s, axis=-1, keepdims=True))
            den = jnp.sum(e, axis=-1, keepdims=True)
            os_.append(jnp.dot(e.astype(BF16), v_ref[0, 0:nk, :], preferred_element_type=F32) / den)
        o = os_[0] - lam * os_[1]
        ms = jnp.mean(o * o, axis=-1, keepdims=True)
        o = o * lax.rsqrt(ms + EPS) * sw_ref[...] * (1.0 - lam_init)
        o_ref[0] = o.astype(o_ref.dtype)

    if j_off == 0:
        @pl.when(j == 0)
        def _():
            attend(LC)

    @pl.when(j > 0)
    def _():
        attend(L)


def diff_mixer(u, qw, kw, cos, sin, dlam, subw, lam_init, with_ctx):
    j_off = 0 if with_ctx else LC // TQ_DIFF
    nblk = L // TQ_DIFF - j_off
    qc, kc, vc = U_OFF["diff_q"] // 128, U_OFF["diff_k"] // 128, U_OFF["diff_v"] // 128
    return pl.pallas_call(
        functools.partial(_diff_kernel, j_off=j_off, lam_init=lam_init),
        grid=(B, DIFF_HEADS, nblk),
        in_specs=[pl.BlockSpec((1, TQ_DIFF, 128), lambda b, h, j: (b, j + j_off, qc + h)),
                  pl.BlockSpec((1, L, 128), lambda b, h, j: (b, 0, kc + h)),
                  pl.BlockSpec((1, L, 128), lambda b, h, j: (b, 0, vc + h)),
                  pl.BlockSpec((1, 128), lambda b, h, j: (0, 0)),
                  pl.BlockSpec((1, 128), lambda b, h, j: (0, 0)),
                  pl.BlockSpec((L, 128), lambda b, h, j: (0, 0)),
                  pl.BlockSpec((L, 128), lambda b, h, j: (0, 0)),
                  pl.BlockSpec((4, 64), lambda b, h, j: (0, 0)),
                  pl.BlockSpec((1, 128), lambda b, h, j: (0, 0))],
        out_specs=pl.BlockSpec((1, TQ_DIFF, 128), lambda b, h, j: (b, j, h)),
        out_shape=jax.ShapeDtypeStruct((B, nblk * TQ_DIFF, GROUP_W), BF16),
        scratch_shapes=[pltpu.VMEM((L, 128), BF16)],
        compiler_params=_cparams(("parallel", "parallel", "arbitrary")),
    )(u, u, u, jnp.tile(qw, 2)[None], jnp.tile(kw, 2)[None], cos, sin, dlam, subw[None])


TM_OUT = 256
TOK_SUB = D // 128


def _store_token_rows(ref, val):
    R = val.shape[0]
    for s in range(TOK_SUB):
        ref[pl.ds(s, R, stride=TOK_SUB), :] = val[:, s * 128:(s + 1) * 128]


def _load_token_rows(ref, R):
    return jnp.concatenate([ref[pl.ds(s, R, stride=TOK_SUB), :] for s in range(TOK_SUB)], axis=-1)


def _outproj_kernel(x_ref, a_ref, b_ref, y_ref, d_ref, w_ref, nw_ref, ml_ref, mc_ref, rwh_ref, rwl_ref,
                    xo_ref, f_ref, lg_ref, *, i_off):
    is_ctx = (pl.program_id(1) + i_off) == 0

    def mod(k):
        return jnp.where(is_ctx, mc_ref[:, k * D:(k + 1) * D], ml_ref[0, :, k * D:(k + 1) * D])

    acc = jnp.dot(a_ref[0], w_ref[0:512, :], preferred_element_type=F32)
    acc += jnp.dot(b_ref[0], w_ref[512:1024, :], preferred_element_type=F32)
    acc += jnp.dot(y_ref[0], w_ref[1024:1536, :], preferred_element_type=F32)
    acc += jnp.dot(d_ref[0], w_ref[1536:2048, :], preferred_element_type=F32)
    xn = x_ref[0] + mod(2) * acc
    xo_ref[0] = xn
    ms = jnp.mean(xn * xn, axis=-1, keepdims=True)
    f = (xn * lax.rsqrt(ms + EPS) * nw_ref[...]) * (1.0 + mod(4)) + mod(3)
    _store_token_rows(f_ref.at[0], f)
    fh = f.astype(BF16)
    fl = (f - fh.astype(F32)).astype(BF16)
    lg_ref[0] = (jnp.dot(fh, rwh_ref[...], preferred_element_type=F32)
                 + jnp.dot(fl, rwh_ref[...], preferred_element_type=F32)
                 + jnp.dot(fh, rwl_ref[...], preferred_element_type=F32))


def out_proj(xall, mixes, w_out_b, nw, ml, mc, rw_pad, with_ctx):
    i_off = 0 if with_ctx else LC // TM_OUT
    nblk = L // TM_OUT - i_off
    rows = lambda b, i: (b, i + i_off, 0)
    rwh = rw_pad.astype(BF16)
    rwl = (rw_pad - rwh.astype(F32)).astype(BF16)
    return pl.pallas_call(
        functools.partial(_outproj_kernel, i_off=i_off),
        grid=(B, nblk),
        in_specs=[pl.BlockSpec((1, TM_OUT, D), rows)]
                 + [pl.BlockSpec((1, TM_OUT, GROUP_W), lambda b, i: (b, i, 0))] * 4
                 + [pl.BlockSpec((D, D), lambda b, i: (0, 0)),
                    pl.BlockSpec((1, D), lambda b, i: (0, 0)),
                    pl.BlockSpec((1, 1, 6 * D), lambda b, i: (b, 0, 0)),
                    pl.BlockSpec((1, 6 * D), lambda b, i: (0, 0)),
                    pl.BlockSpec((D, 128), lambda b, i: (0, 0)),
                    pl.BlockSpec((D, 128), lambda b, i: (0, 0))],
        out_specs=[pl.BlockSpec((1, TM_OUT, D), lambda b, i: (b, i, 0)),
                   pl.BlockSpec((1, TM_OUT * TOK_SUB, 128), lambda b, i: (b, i, 0)),
                   pl.BlockSpec((1, TM_OUT, 128), lambda b, i: (b, i, 0))],
        out_shape=[jax.ShapeDtypeStruct((B, nblk * TM_OUT, D), F32),
                   jax.ShapeDtypeStruct((B, nblk * TM_OUT * TOK_SUB, 128), F32),
                   jax.ShapeDtypeStruct((B, nblk * TM_OUT, 128), F32)],
        compiler_params=_cparams(("parallel", "parallel")),
    )(xall, *mixes, w_out_b, nw, ml, mc, rwh, rwl)


TM_MOE = 256
R_GATHER = 256


def _token_copy(src_hbm, idx, buf, r, sem):
    return pltpu.make_async_copy(src_hbm.at[pl.ds(idx * TOK_SUB, TOK_SUB)], buf.at[pl.ds(r * TOK_SUB, TOK_SUB)], sem)


def _gather_kernel(src_ref, f_hbm, o_ref, buf, sem):
    t = pl.program_id(0)

    def issue(step, slot):
        def body(r, carry):
            _token_copy(f_hbm, src_ref[step * R_GATHER + r], buf.at[slot], r, sem.at[slot]).start()
            return carry

        lax.fori_loop(0, R_GATHER, body, 0, unroll=8)

    @pl.when(t == 0)
    def _():
        issue(0, 0)

    @pl.when(t + 1 < pl.num_programs(0))
    def _():
        issue(t + 1, (t + 1) % 2)

    slot = t % 2

    def drain(r, carry):
        _token_copy(f_hbm, 0, buf.at[slot], r, sem.at[slot]).wait()
        return carry

    lax.fori_loop(0, R_GATHER, drain, 0, unroll=8)
    o_ref[...] = _load_token_rows(buf.at[slot], R_GATHER).astype(BF16)


def gather_rows(src, f_tok, nrows):
    return pl.pallas_call(
        _gather_kernel,
        grid_spec=pltpu.PrefetchScalarGridSpec(
            num_scalar_prefetch=1,
            grid=(nrows // R_GATHER,),
            in_specs=[pl.BlockSpec(memory_space=pl.ANY)],
            out_specs=pl.BlockSpec((R_GATHER, D), lambda t, s: (t, 0)),
            scratch_shapes=[pltpu.VMEM((2, R_GATHER * TOK_SUB, 128), F32), pltpu.SemaphoreType.DMA((2,))]),
        out_shape=jax.ShapeDtypeStruct((nrows, D), BF16),
        compiler_params=_cparams(("arbitrary",)),
    )(src, f_tok)


def _moe_kernel(te_ref, tv_ref, tn_ref, x_ref, wg_hbm, wu_hbm, wd_hbm, o_ref, sg, su, sd, wg_b, wu_b, wd_b, sem):
    t = pl.program_id(0)
    e = te_ref[t]

    def copies(ex):
        return (pltpu.make_async_copy(wg_hbm.at[ex], sg, sem.at[0]),
                pltpu.make_async_copy(wu_hbm.at[ex], su, sem.at[1]),
                pltpu.make_async_copy(wd_hbm.at[ex], sd, sem.at[2]))

    @pl.when(t == 0)
    def _():
        for cp in copies(e):
            cp.start()

    @pl.when((t == 0) | (e != te_ref[jnp.maximum(t - 1, 0)]))
    def _():
        for cp in copies(e):
            cp.wait()
        wg_b[...] = sg[...].astype(BF16)
        wu_b[...] = su[...].astype(BF16)
        wd_b[...] = sd[...].astype(BF16)
        nxt = tn_ref[t]

        @pl.when(nxt >= 0)
        def _():
            for cp in copies(nxt):
                cp.start()

    @pl.when(tv_ref[t] == 0)
    def _():
        o_ref[...] = jnp.zeros_like(o_ref)

    @pl.when(tv_ref[t] > 0)
    def _():
        x = x_ref[...]
        g = jnp.dot(x, wg_b[...], preferred_element_type=F32)
        u = jnp.dot(x, wu_b[...], preferred_element_type=F32)
        h = (g / (1.0 + jnp.exp(-g)) * u).astype(BF16)
        _store_token_rows(o_ref, jnp.dot(h, wd_b[...], preferred_element_type=F32))


def moe_experts(tile_e, tile_v, tile_n, xs, wg, wu, wd):
    nrows = xs.shape[0]
    return pl.pallas_call(
        _moe_kernel,
        grid_spec=pltpu.PrefetchScalarGridSpec(
            num_scalar_prefetch=3,
            grid=(nrows // TM_MOE,),
            in_specs=[pl.BlockSpec((TM_MOE, D), lambda t, te, tv, tn: (t, 0)),
                      pl.BlockSpec(memory_space=pl.ANY),
                      pl.BlockSpec(memory_space=pl.ANY),
                      pl.BlockSpec(memory_space=pl.ANY)],
            out_specs=pl.BlockSpec((TM_MOE * TOK_SUB, 128), lambda t, te, tv, tn: (t, 0)),
            scratch_shapes=[pltpu.VMEM((D, D_EXPERT), F32), pltpu.VMEM((D, D_EXPERT), F32),
                            pltpu.VMEM((D_EXPERT, D), F32),
                            pltpu.VMEM((D, D_EXPERT), BF16), pltpu.VMEM((D, D_EXPERT), BF16),
                            pltpu.VMEM((D_EXPERT, D), BF16),
                            pltpu.SemaphoreType.DMA((3,))]),
        out_shape=jax.ShapeDtypeStruct((nrows * TOK_SUB, 128), F32),
        compiler_params=_cparams(("arbitrary",)),
    )(tile_e, tile_v, tile_n, xs, wg, wu, wd)


R_COMB = 256


def _combine_kernel(pos_ref, x_ref, w_ref, ml_ref, mc_ref, y_hbm, o_ref, buf, sem, *, i_off):
    nblk = pl.num_programs(1)
    step = pl.program_id(0) * nblk + pl.program_id(1)
    is_ctx = (pl.program_id(1) + i_off) == 0

    def issue(st, slot):
        base = st * (R_COMB * TOP_K)

        def body(r, carry):
            for k in range(TOP_K):
                _token_copy(y_hbm, pos_ref[base + TOP_K * r + k], buf.at[slot, k], r, sem.at[slot, k]).start()
            return carry

        lax.fori_loop(0, R_COMB, body, 0, unroll=4)

    @pl.when(step == 0)
    def _():
        issue(0, 0)

    @pl.when(step + 1 < pl.num_programs(0) * nblk)
    def _():
        issue(step + 1, (step + 1) % 2)

    slot = step % 2

    def drain(r, carry):
        for k in range(TOP_K):
            _token_copy(y_hbm, 0, buf.at[slot, k], r, sem.at[slot, k]).wait()
        return carry

    lax.fori_loop(0, R_COMB, drain, 0, unroll=4)
    gf = jnp.where(is_ctx, mc_ref[:, 5 * D:6 * D], ml_ref[0, :, 5 * D:6 * D])
    w0 = w_ref[0, :, 0:1]
    w1 = w_ref[0, :, 1:2]
    for s in range(TOK_SUB):
        cs = slice(s * 128, (s + 1) * 128)
        y = (w0 * buf[slot, 0, pl.ds(s, R_COMB, stride=TOK_SUB), :]
             + w1 * buf[slot, 1, pl.ds(s, R_COMB, stride=TOK_SUB), :])
        o_ref[0, :, cs] = x_ref[0, :, cs] + gf[:, cs] * y


def moe_combine(pos, wts, xall, ml, mc, y_sorted, with_ctx):
    i_off = 0 if with_ctx else LC // R_COMB
    rows_out = xall.shape[1]
    nblk = rows_out // R_COMB
    return pl.pallas_call(
        functools.partial(_combine_kernel, i_off=i_off),
        grid_spec=pltpu.PrefetchScalarGridSpec(
            num_scalar_prefetch=1,
            grid=(B, nblk),
            in_specs=[pl.BlockSpec((1, R_COMB, D), lambda b, i, p: (b, i, 0)),
                      pl.BlockSpec((1, R_COMB, TOP_K), lambda b, i, p: (b, i, 0)),
                      pl.BlockSpec((1, 1, 6 * D), lambda b, i, p: (b, 0, 0)),
                      pl.BlockSpec((1, 6 * D), lambda b, i, p: (0, 0)),
                      pl.BlockSpec(memory_space=pl.ANY)],
            out_specs=pl.BlockSpec((1, R_COMB, D), lambda b, i, p: (b, i, 0)),
            scratch_shapes=[pltpu.VMEM((2, TOP_K, R_COMB * TOK_SUB, 128), F32),
                            pltpu.SemaphoreType.DMA((2, TOP_K))]),
        out_shape=jax.ShapeDtypeStruct((B, rows_out, D), F32),
        compiler_params=_cparams(("arbitrary", "arbitrary")),
    )(pos, xall, wts.reshape(B, rows_out, TOP_K), ml, mc, y_sorted)


def route(logits, router_bias):
    T = logits.shape[0]
    probs = jax.nn.softmax(logits, axis=-1)
    sel = (probs + router_bias.astype(F32)).reshape(T, N_GROUPS, EPG)
    group = jnp.argmax(jnp.max(sel, axis=-1), axis=-1)
    sel_g = jnp.take_along_axis(sel, group[:, None, None], axis=1)[:, 0]
    _, idx = lax.top_k(sel_g, TOP_K)
    expert = (group[:, None] * EPG + idx).astype(jnp.int32)
    wts = jnp.take_along_axis(probs, expert, axis=-1)
    wts = wts / jnp.sum(wts, axis=-1, keepdims=True)
    flat_e = expert.reshape(-1)
    onehot = (flat_e[:, None] == jnp.arange(N_EXPERTS, dtype=jnp.int32)[None, :]).astype(jnp.int32)
    csum = jnp.cumsum(onehot, axis=0)
    rank = jnp.sum(csum * onehot, axis=-1) - 1
    counts = csum[-1]
    ptiles = (counts + TM_MOE - 1) // TM_MOE
    tend = jnp.cumsum(ptiles)
    tstart = tend - ptiles
    dest = (tstart[flat_e] * TM_MOE + rank).astype(jnp.int32)
    ntiles = -(-(TOP_K * T) // TM_MOE) + N_EXPERTS
    nrows = ntiles * TM_MOE
    tid = jnp.arange(ntiles, dtype=jnp.int32)
    tile_v = (tid < tend[-1]).astype(jnp.int32)
    tile_e = jnp.searchsorted(tend, jnp.minimum(tid, tend[-1] - 1), side="right").astype(jnp.int32)
    eid = jnp.arange(N_EXPERTS, dtype=jnp.int32)
    later = (eid[None, :] > eid[:, None]) & (ptiles[None, :] > 0)
    nxt = jnp.min(jnp.where(later, eid[None, :], N_EXPERTS), axis=1)
    tile_n = jnp.where(nxt < N_EXPERTS, nxt, -1)[tile_e].astype(jnp.int32)
    src = jnp.zeros((nrows,), jnp.int32).at[dest].set(jnp.repeat(jnp.arange(T, dtype=jnp.int32), TOP_K))
    return src, wts, dest, tile_e, tile_v, tile_n


GLA_QW = GLA_HEADS * GLA_DK
N_BLK = L // GLA_CHUNK
N_BLK_C = LC // GLA_CHUNK


def _log_sigmoid(z):
    return jnp.minimum(z, 0.0) - jnp.log1p(jnp.exp(-jnp.abs(z)))


def _head_rms(x, w):
    R, W = x.shape
    lane = lax.broadcasted_iota(jnp.int32, (R, 128), 1)
    left = lane < 64
    outs = []
    for c in range(W // 128):
        xc = x[:, c * 128:(c + 1) * 128]
        sq = xc * xc
        sl = jnp.sum(jnp.where(left, sq, 0.0), axis=-1, keepdims=True)
        sr = jnp.sum(jnp.where(left, 0.0, sq), axis=-1, keepdims=True)
        inv = jnp.where(left, lax.rsqrt(sl * (1.0 / 64) + EPS), lax.rsqrt(sr * (1.0 / 64) + EPS))
        outs.append(xc * inv * w)
    return jnp.concatenate(outs, axis=-1)


def _gla_kernel(q_ref, k_ref, v_ref, g_ref, a_ref, wf_ref, wb_ref, gb_ref, nw_ref, ind_ref, bdm_ref, o_ref,
                lf_ref, lb_ref, acc_ref, sf_ref, sb_ref, *, row_off):
    hi = lax.Precision.HIGHEST
    a = a_ref[0].astype(F32)
    lf_ref[...] = _log_sigmoid(jnp.dot(a, wf_ref[...], preferred_element_type=F32, precision=hi)
                               + gb_ref[0:1, :]) * (1.0 / GLA_TEMP)
    lb_ref[...] = _log_sigmoid(jnp.dot(a, wb_ref[...], preferred_element_type=F32, precision=hi)
                               + gb_ref[1:2, :]) * (1.0 / GLA_TEMP)
    acc_ref[...] = jnp.zeros_like(acc_ref)
    sf_ref[...] = jnp.zeros_like(sf_ref)
    sb_ref[...] = jnp.zeros_like(sb_ref)

    C = GLA_CHUNK
    r16 = lax.broadcasted_iota(jnp.int32, (C, C), 0)
    c16 = lax.broadcasted_iota(jnp.int32, (C, C), 1)
    rowi = lax.broadcasted_iota(jnp.int32, (C, GLA_QW), 0)

    def block(r0, la_ref, s_ref, forward):
        tri = ((r16 >= c16) if forward else (r16 <= c16)).astype(F32)
        q = q_ref[0, pl.ds(r0, C), :].astype(F32) * (GLA_DK ** -0.5)
        k = k_ref[0, pl.ds(r0, C), :].astype(F32)
        v = v_ref[0, pl.ds(r0, C), :].astype(F32)
        la = la_ref[pl.ds(r0, C), :]
        b = jnp.dot(tri, la, preferred_element_type=F32, precision=hi)
        btot = jnp.sum(la, axis=0, keepdims=True)
        st = s_ref[...]
        o = lax.dot_general((q * jnp.exp(b)).astype(BF16), st.astype(BF16), (((1,), (1,)), ((), ())),
                            preferred_element_type=F32)
        ps = []
        for s in range(C):
            seen = (rowi >= s) if forward else (rowi <= s)
            e = jnp.exp(jnp.where(seen, b - b[s:s + 1, :], -jnp.inf))
            ps.append(q * k[s:s + 1, :] * e)
        pm = jnp.concatenate(ps, axis=0).astype(BF16)
        rm = jnp.dot(pm, ind_ref[...], preferred_element_type=F32)
        for s in range(C):
            o = o + rm[s * C:(s + 1) * C, :] * v[s:s + 1, :]
        acc_ref[pl.ds(r0, C), :] += o
        kd = (k * jnp.exp(btot - b)).astype(BF16)
        upd = lax.dot_general(v.astype(BF16), kd, (((0,), (0,)), ((), ())), preferred_element_type=F32)
        s_ref[...] = st * jnp.exp(btot) + upd * bdm_ref[...]

    def body(i, carry):
        block(pl.multiple_of(i * C, C), lf_ref, sf_ref, True)
        jb = jnp.where(i < N_BLK_C, N_BLK_C - 1 - i, N_BLK + N_BLK_C - 1 - i)
        block(pl.multiple_of(jb * C, C), lb_ref, sb_ref, False)
        return carry

    lax.fori_loop(0, N_BLK, body, 0)
    o = _head_rms(acc_ref[row_off:, :], nw_ref[...])
    g = g_ref[0, row_off:, :].astype(F32)
    o_ref[0] = (o * (g / (1.0 + jnp.exp(-g)))).astype(o_ref.dtype)


def gla_mixer(u, gate_w, gate_b, norm_w, with_ctx):
    row_off = 0 if with_ctx else LC
    qc, kc = U_OFF["gla_q"] // 256, U_OFF["gla_k"] // 256
    vc, gc, ac = U_OFF["gla_v"] // 512, U_OFF["gla_g"] // 512, U_OFF["gla_a"] // 128
    wf = jnp.zeros((128, GLA_QW), F32).at[0:GLA_RANK].set(gate_w[0])
    wb = jnp.zeros((128, GLA_QW), F32).at[GLA_RANK:2 * GLA_RANK].set(gate_w[1])
    hd = np.arange(GLA_QW)[:, None] // GLA_DK == np.arange(GROUP_W)[None, :] // GLA_DV
    ind = jnp.asarray(hd, BF16)
    bdm = jnp.asarray(hd.T, F32)
    full = lambda *shape: pl.BlockSpec(shape, lambda b: (0,) * len(shape))
    return pl.pallas_call(
        functools.partial(_gla_kernel, row_off=row_off),
        grid=(B,),
        in_specs=[pl.BlockSpec((1, L, 256), lambda b: (b, 0, qc)),
                  pl.BlockSpec((1, L, 256), lambda b: (b, 0, kc)),
                  pl.BlockSpec((1, L, 512), lambda b: (b, 0, vc)),
                  pl.BlockSpec((1, L, 512), lambda b: (b, 0, gc)),
                  pl.BlockSpec((1, L, 128), lambda b: (b, 0, ac)),
                  full(128, GLA_QW), full(128, GLA_QW), full(2, GLA_QW), full(1, 128),
                  full(GLA_QW, GROUP_W), full(GROUP_W, GLA_QW)],
        out_specs=pl.BlockSpec((1, L - row_off, GROUP_W), lambda b: (b, 0, 0)),
        out_shape=jax.ShapeDtypeStruct((B, L - row_off, GROUP_W), BF16),
        scratch_shapes=[pltpu.VMEM((L, GLA_QW), F32), pltpu.VMEM((L, GLA_QW), F32), pltpu.VMEM((L, GROUP_W), F32),
                        pltpu.VMEM((GROUP_W, GLA_QW), F32), pltpu.VMEM((GROUP_W, GLA_QW), F32)],
        compiler_params=_cparams(("parallel",)),
    )(u, u, u, u, u, wf, wb, gate_b, jnp.tile(norm_w, 2)[None], ind, bdm)


HY_CT = 256
HY_TK = 512


@functools.lru_cache(maxsize=None)
def _dft_consts(Lh):
    k = np.arange(Lh, dtype=np.int64)
    ph = (np.outer(k, k) % (2 * Lh)).astype(np.float64) * (np.pi / Lh)
    sgn = (1.0 - 2.0 * (k % 2)).astype(np.float32)[:, None]
    return np.cos(ph).astype(np.float32), np.sin(ph).astype(np.float32), sgn


def _hyena_features(Lh):
    t = jnp.linspace(0.0, 1.0, Lh, dtype=F32)[:, None]
    w = (2.0 * math.pi / Lh) * jnp.arange(Lh, dtype=F32)[:, None]
    bands = jnp.linspace(1e-4, HY_BANDS - 1, HY_BANDS, dtype=F32)
    z = jnp.concatenate([t, jnp.cos(w * bands), -jnp.sin(w * bands)], axis=-1)
    return jnp.pad(z, ((0, 0), (0, 128 - z.shape[1])))


def _hy_filter_kernel(z_ref, w1_ref, b1_ref, w2_ref, b2_ref, fr_ref, w3f_ref, w3b_ref, dl_ref, c_ref, s_ref, sgn_ref,
                      hc_ref, hs_ref, hn_ref, *, Lh):
    hi = lax.Precision.HIGHEST
    z = z_ref[...]
    h = jnp.sin(fr_ref[0:1, :] * (jnp.dot(z, w1_ref[...], preferred_element_type=F32, precision=hi) + b1_ref[...]))
    h = jnp.sin(fr_ref[1:2, :] * (jnp.dot(h, w2_ref[...], preferred_element_type=F32, precision=hi) + b2_ref[...]))
    dec = jnp.exp(-(z[:, 0:1] * dl_ref[...]))
    hf = jnp.dot(h, w3f_ref[...], preferred_element_type=F32, precision=hi) * dec
    hb = jnp.dot(h, w3b_ref[...], preferred_element_type=F32, precision=hi) * dec
    nrm = jnp.sum(jnp.abs(hf), axis=0, keepdims=True) + jnp.sum(jnp.abs(hb), axis=0, keepdims=True)
    row = lax.broadcasted_iota(jnp.int32, (Lh, 1), 0)
    hf = hf / nrm
    hb = jnp.where(row == 0, 0.0, hb / nrm)
    wk = jnp.where(row == 0, 0.5 / Lh, 1.0 / Lh)

    def project(m_ref, x):
        xh = x.astype(BF16)
        xl = (x - xh.astype(F32)).astype(BF16)
        return (jnp.dot(m_ref[...], xh, preferred_element_type=F32)
                + jnp.dot(m_ref[...], xl, preferred_element_type=F32))

    am = hf + hb
    hc_ref[0] = project(c_ref, am) * wk
    hs_ref[0] = project(s_ref, hf - hb) * wk
    hn_ref[0] = jnp.sum(am * sgn_ref[...], axis=0, keepdims=True) * (0.5 / Lh)


def hyena_spectrum(Lh, w1, b1, w2, b2, w3, fr, cmat, smat, sgn):
    z = _hyena_features(Lh)
    w1p = jnp.pad(w1, ((0, 128 - w1.shape[0]), (0, 0)))
    deltas = jnp.linspace(HY_MIN_DECAY, HY_MAX_DECAY, HY_CH, dtype=F32)[None]
    nct = HY_CH // HY_CT
    full = lambda *shape: pl.BlockSpec(shape, lambda o, c: (0,) * len(shape))
    return pl.pallas_call(
        functools.partial(_hy_filter_kernel, Lh=Lh),
        grid=(HY_ORDER, nct),
        in_specs=[full(Lh, 128), full(128, 64), full(1, 64), full(64, 64), full(1, 64), full(2, 64),
                  pl.BlockSpec((64, HY_CT), lambda o, c: (0, o * 2 * nct + c)),
                  pl.BlockSpec((64, HY_CT), lambda o, c: (0, o * 2 * nct + nct + c)),
                  pl.BlockSpec((1, HY_CT), lambda o, c: (0, c)),
                  full(Lh, Lh), full(Lh, Lh), full(Lh, 1)],
        out_specs=[pl.BlockSpec((1, Lh, HY_CT), lambda o, c: (o, 0, c)),
                   pl.BlockSpec((1, Lh, HY_CT), lambda o, c: (o, 0, c)),
                   pl.BlockSpec((1, 1, HY_CT), lambda o, c: (o, 0, c))],
        out_shape=[jax.ShapeDtypeStruct((HY_ORDER, Lh, HY_CH), F32),
                   jax.ShapeDtypeStruct((HY_ORDER, Lh, HY_CH), F32),
                   jax.ShapeDtypeStruct((HY_ORDER, 1, HY_CH), F32)],
        compiler_params=_cparams(("arbitrary", "arbitrary")),
    )(z, w1p, b1[None], w2, b2[None], fr, w3, w3, deltas, cmat, smat, sgn)


def _hyena_seq(row0, Lh, x1_ref, x2_ref, y_ref, cw_refs, cb_refs, bias_ref, c_ref, s_ref, hc_ref, hs_ref, hn_ref, *,
               scratch):
    r = lax.broadcasted_iota(jnp.int32, (Lh, 1), 0)

    def sconv(u_ref, w_ref, b_ref):
        u = u_ref[0, row0:row0 + Lh, :].astype(F32)
        up = jnp.where(r == 0, 0.0, pltpu.roll(u, 1, 0))
        dn = jnp.where(r == Lh - 1, 0.0, pltpu.roll(u, Lh - 1, 0))
        return up * w_ref[0:1, :] + u * w_ref[1:2, :] + dn * w_ref[2:3, :] + b_ref[...]

    gate_refs = (x1_ref, x2_ref)
    ys_ref, yb_ref, cv_ref = scratch
    rows = slice(0, Lh)
    ys_ref[rows, :] = sconv(y_ref, cw_refs[2], cb_refs[2])
    sgn = (1 - 2 * (r & 1)).astype(F32)
    tk = min(HY_TK, Lh)
    for o in range(HY_ORDER):
        y = ys_ref[rows, :]
        yb_ref[rows, :] = y.astype(BF16)
        cv_ref[rows, :] = sgn * (jnp.sum(y * sgn, axis=0, keepdims=True) * hn_ref[o])
        for m in range(Lh // tk):
            fs = slice(m * tk, (m + 1) * tk)
            yc = jnp.dot(c_ref[fs, :], yb_ref[rows, :], preferred_element_type=F32)
            ysn = jnp.dot(s_ref[fs, :], yb_ref[rows, :], preferred_element_type=F32)
            hc = hc_ref[o, fs, :]
            hs = hs_ref[o, fs, :]
            pc = (yc * hc - ysn * hs).astype(BF16)
            ps = (yc * hs + ysn * hc).astype(BF16)
            cv_ref[rows, :] += (jnp.dot(c_ref[:, fs], pc, preferred_element_type=F32)
                                + jnp.dot(s_ref[:, fs], ps, preferred_element_type=F32))
        ys_ref[rows, :] = (sconv(gate_refs[o], cw_refs[o], cb_refs[o])
                           * (cv_ref[rows, :] + ys_ref[rows, :] * bias_ref[o:o + 1, :]))
    return ys_ref[rows, :]


def _hyena_kernel(x1_ref, x2_ref, y_ref, w1_ref, w2_ref, w3_ref, b1_ref, b2_ref, b3_ref, bias_ref, *rest, with_ctx):
    nmat = 10 if with_ctx else 5
    o_ref = rest[nmat]
    scratch = rest[nmat + 1:]
    seqs = [(LC, S, rest[0:5])]
    if with_ctx:
        seqs.append((0, LC, rest[5:10]))
    for row0, Lh, mats in seqs:
        y = _hyena_seq(row0, Lh, x1_ref, x2_ref, y_ref, (w1_ref, w2_ref, w3_ref), (b1_ref, b2_ref, b3_ref),
                       bias_ref, *mats, scratch=scratch)
        out0 = row0 if with_ctx else 0
        o_ref[0, out0:out0 + Lh, :] = y.astype(o_ref.dtype)


def hyena_mixer(u, conv_w, conv_b, w1, b1, w2, b2, w3, fr, bias, with_ctx):
    nct = HY_CH // HY_CT
    hc0 = U_OFF["hy"] // HY_CT
    once = pl.Buffered(1)
    consts = []
    specs = []
    for Lh in ((S, LC) if with_ctx else (S,)):
        cm, sm, sgn = _dft_consts(Lh)
        cm, sm, sgn = jnp.asarray(cm, BF16), jnp.asarray(sm, BF16), jnp.asarray(sgn)
        hc, hs, hn = hyena_spectrum(Lh, w1, b1, w2, b2, w3, fr, cm, sm, sgn)
        consts += [cm, sm, hc, hs, hn]
        specs += [pl.BlockSpec((Lh, Lh), lambda c, b: (0, 0), pipeline_mode=once),
                  pl.BlockSpec((Lh, Lh), lambda c, b: (0, 0), pipeline_mode=once),
                  pl.BlockSpec((HY_ORDER, Lh, HY_CT), lambda c, b: (0, 0, c), pipeline_mode=once),
                  pl.BlockSpec((HY_ORDER, Lh, HY_CT), lambda c, b: (0, 0, c), pipeline_mode=once),
                  pl.BlockSpec((HY_ORDER, 1, HY_CT), lambda c, b: (0, 0, c), pipeline_mode=once)]
    rows_out = L if with_ctx else S
    ublk = lambda j: pl.BlockSpec((1, L, HY_CT), lambda c, b: (b, 0, hc0 + j * nct + c))
    wblk = lambda j: pl.BlockSpec((3, HY_CT), lambda c, b: (0, j * nct + c))
    bblk = lambda j: pl.BlockSpec((1, HY_CT), lambda c, b: (0, j * nct + c))
    return pl.pallas_call(
        functools.partial(_hyena_kernel, with_ctx=with_ctx),
        grid=(nct, B),
        in_specs=[ublk(0), ublk(1), ublk(2), wblk(0), wblk(1), wblk(2), bblk(0), bblk(1), bblk(2),
                  pl.BlockSpec((HY_ORDER, HY_CT), lambda c, b: (0, c))] + specs,
        out_specs=pl.BlockSpec((1, rows_out, HY_CT), lambda c, b: (b, 0, c)),
        out_shape=jax.ShapeDtypeStruct((B, rows_out, HY_CH), BF16),
        scratch_shapes=[pltpu.VMEM((S, HY_CT), F32), pltpu.VMEM((S, HY_CT), BF16), pltpu.VMEM((S, HY_CT), F32)],
        compiler_params=_cparams(("arbitrary", "arbitrary")),
    )(u, u, u, conv_w, conv_w, conv_w, conv_b[None], conv_b[None], conv_b[None], bias, *consts)


def _permute_w_in(w):
    parts = []
    for n in U_ORDER:
        o, wd = REF_COLS[n]
        p = w[:, o:o + wd]
        if wd < 128:
            p = jnp.pad(p, ((0, 0), (0, 128 - wd)))
        parts.append(p)
    parts.append(jnp.zeros((w.shape[0], U_W - U_USED), w.dtype))
    return jnp.concatenate(parts, axis=1).astype(BF16)


def kernel(x, c, ctx, c_ctx, norm1_w, norm2_w, ada_w, ada_b, w_in, w_out, gla_gate_w, gla_gate_b, gla_norm_w,
           swa_q_norm_w, swa_k_norm_w, swa_sink, hyena_conv_w, hyena_conv_b, hyena_ffn_w1, hyena_ffn_b1,
           hyena_ffn_w2, hyena_ffn_b2, hyena_ffn_w3, hyena_ffn_freq, hyena_bias, diff_q_norm_w, diff_k_norm_w,
           diff_lambda, diff_subln_w, router_w, router_bias, expert_w_gate, expert_w_up, expert_w_down):
    assert x.shape == (B, S, D) and ctx.shape == (B, LC, D)
    cc = jnp.zeros((16, D), F32).at[:B].set(c).at[B].set(c_ctx)
    mods = ada_mod(cc, ada_w, ada_b)
    cos, sin = rope_tables128()
    rw_pad = jnp.pad(router_w, ((0, 0), (0, 128 - N_EXPERTS)))
    xall = jnp.concatenate([ctx, x], axis=1)

    for l in range(DEPTH):
        with_ctx = l < DEPTH - 1
        lam_init = 0.8 - 0.6 * math.exp(-0.3 * l)
        ml = mods[l, :B].reshape(B, 1, 6 * D)
        mc = mods[l, B:B + 1]
        u = in_proj(xall, norm1_w[l][None], ml, mc, _permute_w_in(w_in[l]))
        mix_a = gla_mixer(u, gla_gate_w[l], gla_gate_b[l], gla_norm_w[l], with_ctx)
        mix_b = swa_mixer(u, swa_sink[l], swa_q_norm_w[l], swa_k_norm_w[l], cos, sin, with_ctx)
        mix_y = hyena_mixer(u, hyena_conv_w[l], hyena_conv_b[l], hyena_ffn_w1[l], hyena_ffn_b1[l],
                            hyena_ffn_w2[l], hyena_ffn_b2[l], hyena_ffn_w3[l], hyena_ffn_freq[l],
                            hyena_bias[l], with_ctx)
        mix_d = diff_mixer(u, diff_q_norm_w[l], diff_k_norm_w[l], cos, sin, diff_lambda[l], diff_subln_w[l],
                           lam_init, with_ctx)
        xall, f, logits = out_proj(xall, (mix_a, mix_b, mix_y, mix_d), w_out[l].astype(BF16), norm2_w[l][None],
                                   ml, mc, rw_pad, with_ctx)
        T = B * xall.shape[1]
        src, wts, dest, tile_e, tile_v, tile_n = route(logits.reshape(T, 128)[:, :N_EXPERTS], router_bias)
        xs = gather_rows(src, f.reshape(T * TOK_SUB, 128), src.shape[0])
        ys = moe_experts(tile_e, tile_v, tile_n, xs, expert_w_gate[l], expert_w_up[l], expert_w_down[l])
        xall = moe_combine(dest, wts, xall, ml, mc, ys, with_ctx)
    return xall
```

```python
import functools
import math

import numpy as np
import jax
import jax.numpy as jnp
from jax import lax
from jax.experimental import pallas as pl
from jax.experimental.pallas import tpu as pltpu

F32 = jnp.float32
BF16 = jnp.bfloat16

D = 2048
B = 8
S = 2048
LC = 256
L = LC + S
DEPTH = 2
GRID_W = 64
HEAD_DIM = 64
ROPE_THETA = 10000.0
EPS = 1e-6
GROUP_W = 512

GLA_DV = 64
GLA_DK = 32
GLA_HEADS = 8
GLA_RANK = 16
GLA_TEMP = 16.0
GLA_CHUNK = 16
SWA_HEADS = 8
SWA_KV = 2
SWA_WINDOW = 128
HY_CH = 512
HY_ORDER = 2
HY_BANDS = 16
HY_TARGET = 1e-2
HY_MIN_DECAY = math.log(1.0 / HY_TARGET) / 1.5
HY_MAX_DECAY = math.log(1.0 / HY_TARGET) / 0.3
DIFF_HEADS = 4
N_EXPERTS = 16
N_GROUPS = 4
EPG = 4
TOP_K = 2
D_EXPERT = 1024

REF_COLS = dict(gla_q=(0, 256), gla_k=(256, 256), gla_v=(512, 512), gla_g=(1024, 512), gla_a=(1536, 32),
                swa_q=(1568, 512), swa_k=(2080, 128), swa_v=(2208, 128), hy=(2336, 1536),
                diff_q=(3872, 512), diff_k=(4384, 512), diff_v=(4896, 512))
U_ORDER = ("hy", "gla_v", "gla_g", "swa_q", "diff_q", "diff_k", "diff_v", "gla_q", "gla_k", "swa_k", "swa_v", "gla_a")
U_OFF = {}
_o = 0
for _n in U_ORDER:
    U_OFF[_n] = _o
    _o += max(REF_COLS[_n][1], 128)
U_USED = _o
TN_IN = 512
U_W = -(-U_USED // TN_IN) * TN_IN

VMEM_LIMIT = 56 * 1024 * 1024


def _cparams(sem):
    return pltpu.CompilerParams(dimension_semantics=sem, vmem_limit_bytes=VMEM_LIMIT)


def _ada_kernel(c_ref, w_ref, b_ref, o_ref):
    c = c_ref[...]
    a = c / (1.0 + jnp.exp(-c))
    o_ref[0] = jnp.dot(a, w_ref[0], preferred_element_type=F32, precision=lax.Precision.HIGHEST) + b_ref[0]


def ada_mod(cc, ada_w, ada_b):
    tn = 1024
    return pl.pallas_call(
        _ada_kernel,
        grid=(DEPTH, 6 * D // tn),
        in_specs=[pl.BlockSpec((16, D), lambda l, j: (0, 0)),
                  pl.BlockSpec((1, D, tn), lambda l, j: (l, 0, j)),
                  pl.BlockSpec((1, 1, tn), lambda l, j: (l, 0, j))],
        out_specs=pl.BlockSpec((1, 16, tn), lambda l, j: (l, 0, j)),
        out_shape=jax.ShapeDtypeStruct((DEPTH, 16, 6 * D), F32),
        compiler_params=_cparams(("parallel", "parallel")),
    )(cc, ada_w, ada_b.reshape(DEPTH, 1, 6 * D))


TM_IN = 1152
CH_IN = 32


def _inproj_kernel(x_ref, nw_ref, scl_ref, shl_ref, scc_ref, shc_ref, w_ref, o_ref, h_ref):
    i = pl.program_id(1)

    @pl.when(pl.program_id(2) == 0)
    def _():
        def chunk(c, carry):
            r0 = pl.multiple_of(c * CH_IN, CH_IN)
            x = x_ref[0, pl.ds(r0, CH_IN), :]
            ms = jnp.mean(x * x, axis=-1, keepdims=True)
            y = x * lax.rsqrt(ms + EPS) * nw_ref[...]
            row = i * TM_IN + r0 + lax.broadcasted_iota(jnp.int32, (CH_IN, 1), 0)
            is_ctx = row < LC
            sc = jnp.where(is_ctx, scc_ref[...], scl_ref[0])
            sh = jnp.where(is_ctx, shc_ref[...], shl_ref[0])
            h_ref[pl.ds(r0, CH_IN), :] = (y * (1.0 + sc) + sh).astype(BF16)
            return carry

        lax.fori_loop(0, TM_IN // CH_IN, chunk, 0)

    o_ref[0] = jnp.dot(h_ref[...], w_ref[...], preferred_element_type=F32).astype(o_ref.dtype)


def in_proj(xall, nw, ml, mc, w_in_p):
    return pl.pallas_call(
        _inproj_kernel,
        grid=(B, L // TM_IN, U_W // TN_IN),
        in_specs=[pl.BlockSpec((1, TM_IN, D), lambda b, i, j: (b, i, 0)),
                  pl.BlockSpec((1, D), lambda b, i, j: (0, 0)),
                  pl.BlockSpec((1, 1, D), lambda b, i, j: (b, 0, 1)),
                  pl.BlockSpec((1, 1, D), lambda b, i, j: (b, 0, 0)),
                  pl.BlockSpec((1, D), lambda b, i, j: (0, 1)),
                  pl.BlockSpec((1, D), lambda b, i, j: (0, 0)),
                  pl.BlockSpec((D, TN_IN), lambda b, i, j: (0, j))],
        out_specs=pl.BlockSpec((1, TM_IN, TN_IN), lambda b, i, j: (b, i, j)),
        out_shape=jax.ShapeDtypeStruct((B, L, U_W), BF16),
        scratch_shapes=[pltpu.VMEM((TM_IN, D), BF16)],
        compiler_params=_cparams(("parallel", "parallel", "arbitrary")),
    )(xall, nw, ml, ml, mc, mc, w_in_p)


def _norm_rope(x, w, cos, sin):
    R, W = x.shape
    lane = lax.broadcasted_iota(jnp.int32, (R, 128), 1)
    left = lane < HEAD_DIM
    first = (lane & 31) < 16
    outs = []
    for c in range(W // 128):
        xc = x[:, c * 128:(c + 1) * 128]
        sq = xc * xc
        sl = jnp.sum(jnp.where(left, sq, 0.0), axis=-1, keepdims=True)
        sr = jnp.sum(jnp.where(left, 0.0, sq), axis=-1, keepdims=True)
        inv = jnp.where(left, lax.rsqrt(sl * (1.0 / HEAD_DIM) + EPS), lax.rsqrt(sr * (1.0 / HEAD_DIM) + EPS))
        y = xc * inv * w
        rot = jnp.where(first, pltpu.roll(y, 112, 1), pltpu.roll(y, 16, 1))
        outs.append(y * cos + rot * sin)
    return outs[0] if len(outs) == 1 else jnp.concatenate(outs, axis=-1)


def rope_tables128():
    n = HEAD_DIM // 4
    rows = S // GRID_W
    row_ids = jnp.repeat(jnp.arange(rows), GRID_W).astype(F32)
    col_ids = jnp.tile(jnp.arange(GRID_W), rows).astype(F32)
    inv = ROPE_THETA ** (-jnp.arange(n, dtype=F32) / n)
    ar = row_ids[:, None] * inv
    ac = col_ids[:, None] * inv
    cos64 = jnp.concatenate([jnp.cos(ar), jnp.cos(ar), jnp.cos(ac), jnp.cos(ac)], axis=-1)
    sin64 = jnp.concatenate([-jnp.sin(ar), jnp.sin(ar), -jnp.sin(ac), jnp.sin(ac)], axis=-1)
    cos = jnp.concatenate([jnp.ones((LC, 64), F32), cos64], axis=0)
    sin = jnp.concatenate([jnp.zeros((LC, 64), F32), sin64], axis=0)
    return jnp.tile(cos, (1, 2)), jnp.tile(sin, (1, 2))


SWA_BAND = 3 * SWA_WINDOW


def _swa_kernel(sink_ref, q_ref, k_ref, v_ref, qw_ref, kw_ref, cos_ref, sin_ref, o_ref, kn_ref, *, n_off):
    n = pl.program_id(1) + n_off

    @pl.when(pl.program_id(1) == 0)
    def _():
        kn_ref[...] = _norm_rope(k_ref[0].astype(F32), kw_ref[...], cos_ref[...], sin_ref[...]).astype(BF16)

    r0 = pl.multiple_of(n * 128, 128)
    q = _norm_rope(q_ref[0].astype(F32), qw_ref[...], cos_ref[pl.ds(r0, 128), :], sin_ref[pl.ds(r0, 128), :])
    q = (q * (HEAD_DIM ** -0.5)).astype(BF16)

    nb = n - LC // 128
    is_lat = nb >= 0
    lstart = jnp.clip((nb - 1) * 128, 0, S - SWA_BAND)
    start = pl.multiple_of(LC + lstart, 128)
    kk = jnp.concatenate([kn_ref[0:LC, :], kn_ref[pl.ds(start, SWA_BAND), :]], axis=0)
    vv = jnp.concatenate([v_ref[0, 0:LC, :], v_ref[0, pl.ds(start, SWA_BAND), :]], axis=0)
    nk = LC + SWA_BAND
    row = lax.broadcasted_iota(jnp.int32, (512, nk), 0) & 127
    col = lax.broadcasted_iota(jnp.int32, (512, nk), 1)
    qpos = nb * 128 + row
    kpos = lstart + col - LC
    valid = (col < LC) | ((jnp.abs(qpos - kpos) <= SWA_WINDOW) & is_lat)
    rowi = lax.broadcasted_iota(jnp.int32, (512, 1), 0)
    outs = [None] * SWA_HEADS
    for hk in range(SWA_KV):
        qs = jnp.concatenate([q[:, (hk * 4 + g) * 64:(hk * 4 + g + 1) * 64] for g in range(4)], axis=0)
        s = lax.dot_general(qs, kk[:, hk * 64:(hk + 1) * 64], (((1,), (1,)), ((), ())),
                            preferred_element_type=F32)
        s = jnp.where(valid, s, -jnp.inf)
        sk = jnp.where(rowi < 128, sink_ref[hk * 4],
                       jnp.where(rowi < 256, sink_ref[hk * 4 + 1],
                                 jnp.where(rowi < 384, sink_ref[hk * 4 + 2], sink_ref[hk * 4 + 3])))
        m = jnp.maximum(jnp.max(s, axis=-1, keepdims=True), sk)
        e = jnp.exp(s - m)
        den = jnp.sum(e, axis=-1, keepdims=True) + jnp.exp(sk - m)
        o = jnp.dot(e.astype(BF16), vv[:, hk * 64:(hk + 1) * 64], preferred_element_type=F32) / den
        for g in range(4):
            outs[hk * 4 + g] = o[g * 128:(g + 1) * 128]
    o_ref[0] = jnp.concatenate(outs, axis=-1).astype(o_ref.dtype)


def swa_mixer(u, sink, qw, kw, cos, sin, with_ctx):
    n_off = 0 if with_ctx else LC // 128
    nblk = L // 128 - n_off
    qc, kc, vc = U_OFF["swa_q"] // 512, U_OFF["swa_k"] // 128, U_OFF["swa_v"] // 128
    return pl.pallas_call(
        functools.partial(_swa_kernel, n_off=n_off),
        grid_spec=pltpu.PrefetchScalarGridSpec(
            num_scalar_prefetch=1,
            grid=(B, nblk),
            in_specs=[pl.BlockSpec((1, 128, 512), lambda b, n, s: (b, n + n_off, qc)),
                      pl.BlockSpec((1, L, 128), lambda b, n, s: (b, 0, kc)),
                      pl.BlockSpec((1, L, 128), lambda b, n, s: (b, 0, vc)),
                      pl.BlockSpec((1, 128), lambda b, n, s: (0, 0)),
                      pl.BlockSpec((1, 128), lambda b, n, s: (0, 0)),
                      pl.BlockSpec((L, 128), lambda b, n, s: (0, 0)),
                      pl.BlockSpec((L, 128), lambda b, n, s: (0, 0))],
            out_specs=pl.BlockSpec((1, 128, 512), lambda b, n, s: (b, n, 0)),
            scratch_shapes=[pltpu.VMEM((L, 128), BF16)]),
        out_shape=jax.ShapeDtypeStruct((B, nblk * 128, GROUP_W), BF16),
        compiler_params=_cparams(("parallel", "arbitrary")),
    )(sink, u, u, u, jnp.tile(qw, 2)[None], jnp.tile(kw, 2)[None], cos, sin)


TQ_DIFF = 256


def _diff_kernel(q_ref, k_ref, v_ref, qw_ref, kw_ref, cos_ref, sin_ref, dl_ref, sw_ref, o_ref, kn_ref, vx_ref,
                 *, j_off, lam_init):
    j = pl.program_id(2) + j_off

    @pl.when(pl.program_id(2) == 0)
    def _():
        kn_ref[...] = _norm_rope(k_ref[0].astype(F32), kw_ref[...], cos_ref[...], sin_ref[...]).astype(BF16)
        vx_ref[:, 0:128] = v_ref[0]
        vx_ref[:, 128:256] = jnp.ones((L, 128), BF16)

    dl = dl_ref[...]
    lam = (jnp.exp(jnp.sum(dl[0:1] * dl[1:2], axis=-1, keepdims=True))
           - jnp.exp(jnp.sum(dl[2:3] * dl[3:4], axis=-1, keepdims=True)) + lam_init)
    r0 = pl.multiple_of(j * TQ_DIFF, TQ_DIFF)
    q = _norm_rope(q_ref[0].astype(F32), qw_ref[...], cos_ref[pl.ds(r0, TQ_DIFF), :], sin_ref[pl.ds(r0, TQ_DIFF), :])
    q = (q * (HEAD_DIM ** -0.5)).astype(BF16)

    def attend(nk):
        os_ = []
        for m in range(2):
            s = lax.dot_general(q[:, m * 64:(m + 1) * 64], kn_ref[0:nk, m * 64:(m + 1) * 64],
                                (((1,), (1,)), ((), ())), preferred_element_type=F32)
            e = jnp.exp((s - jnp.max(s, axis=-1, keepdims=True)).astype(BF16))
            oe = jnp.dot(e, vx_ref[0:nk, :], preferred_element_type=F32)
            os_.append(oe[:, 0:128] / oe[:, 128:129])
        o = os_[0] - lam * os_[1]
        ms = jnp.mean(o * o, axis=-1, keepdims=True)
        o = o * lax.rsqrt(ms + EPS) * sw_ref[...] * (1.0 - lam_init)
        o_ref[0] = o.astype(o_ref.dtype)

    if j_off == 0:
        @pl.when(j == 0)
        def _():
            attend(LC)

    @pl.when(j > 0)
    def _():
        attend(L)


def diff_mixer(u, qw, kw, cos, sin, dlam, subw, lam_init, with_ctx):
    j_off = 0 if with_ctx else LC // TQ_DIFF
    nblk = L // TQ_DIFF - j_off
    qc, kc, vc = U_OFF["diff_q"] // 128, U_OFF["diff_k"] // 128, U_OFF["diff_v"] // 128
    return pl.pallas_call(
        functools.partial(_diff_kernel, j_off=j_off, lam_init=lam_init),
        grid=(B, DIFF_HEADS, nblk),
        in_specs=[pl.BlockSpec((1, TQ_DIFF, 128), lambda b, h, j: (b, j + j_off, qc + h)),
                  pl.BlockSpec((1, L, 128), lambda b, h, j: (b, 0, kc + h)),
                  pl.BlockSpec((1, L, 128), lambda b, h, j: (b, 0, vc + h)),
                  pl.BlockSpec((1, 128), lambda b, h, j: (0, 0)),
                  pl.BlockSpec((1, 128), lambda b, h, j: (0, 0)),
                  pl.BlockSpec((L, 128), lambda b, h, j: (0, 0)),
                  pl.BlockSpec((L, 128), lambda b, h, j: (0, 0)),
                  pl.BlockSpec((4, 64), lambda b, h, j: (0, 0)),
                  pl.BlockSpec((1, 128), lambda b, h, j: (0, 0))],
        out_specs=pl.BlockSpec((1, TQ_DIFF, 128), lambda b, h, j: (b, j, h)),
        out_shape=jax.ShapeDtypeStruct((B, nblk * TQ_DIFF, GROUP_W), BF16),
        scratch_shapes=[pltpu.VMEM((L, 128), BF16), pltpu.VMEM((L, 256), BF16)],
        compiler_params=_cparams(("parallel", "parallel", "arbitrary")),
    )(u, u, u, jnp.tile(qw, 2)[None], jnp.tile(kw, 2)[None], cos, sin, dlam, subw[None])


TM_OUT = 256
TOK_SUB = D // 128


def _store_token_rows(ref, val):
    R = val.shape[0]
    for s in range(TOK_SUB):
        ref[pl.ds(s, R, stride=TOK_SUB), :] = val[:, s * 128:(s + 1) * 128]


def _load_token_rows(ref, R):
    return jnp.concatenate([ref[pl.ds(s, R, stride=TOK_SUB), :] for s in range(TOK_SUB)], axis=-1)


def _outproj_kernel(x_ref, a_ref, b_ref, y_ref, d_ref, w_ref, nw_ref, ml_ref, mc_ref, rwh_ref, rwl_ref,
                    xo_ref, f_ref, lg_ref, *, i_off):
    is_ctx = (pl.program_id(1) + i_off) == 0

    def mod(k):
        return jnp.where(is_ctx, mc_ref[:, k * D:(k + 1) * D], ml_ref[0, :, k * D:(k + 1) * D])

    acc = jnp.dot(a_ref[0], w_ref[0:512, :], preferred_element_type=F32)
    acc += jnp.dot(b_ref[0], w_ref[512:1024, :], preferred_element_type=F32)
    acc += jnp.dot(y_ref[0], w_ref[1024:1536, :], preferred_element_type=F32)
    acc += jnp.dot(d_ref[0], w_ref[1536:2048, :], preferred_element_type=F32)
    xn = x_ref[0] + mod(2) * acc
    xo_ref[0] = xn
    ms = jnp.mean(xn * xn, axis=-1, keepdims=True)
    f = (xn * lax.rsqrt(ms + EPS) * nw_ref[...]) * (1.0 + mod(4)) + mod(3)
    _store_token_rows(f_ref.at[0], f)
    fh = f.astype(BF16)
    fl = (f - fh.astype(F32)).astype(BF16)
    lg_ref[0] = (jnp.dot(fh, rwh_ref[...], preferred_element_type=F32)
                 + jnp.dot(fl, rwh_ref[...], preferred_element_type=F32)
                 + jnp.dot(fh, rwl_ref[...], preferred_element_type=F32))


def out_proj(xall, mixes, w_out_b, nw, ml, mc, rw_pad, with_ctx):
    i_off = 0 if with_ctx else LC // TM_OUT
    nblk = L // TM_OUT - i_off
    rows = lambda b, i: (b, i + i_off, 0)
    rwh = rw_pad.astype(BF16)
    rwl = (rw_pad - rwh.astype(F32)).astype(BF16)
    return pl.pallas_call(
        functools.partial(_outproj_kernel, i_off=i_off),
        grid=(B, nblk),
        in_specs=[pl.BlockSpec((1, TM_OUT, D), rows)]
                 + [pl.BlockSpec((1, TM_OUT, GROUP_W), lambda b, i: (b, i, 0))] * 4
                 + [pl.BlockSpec((D, D), lambda b, i: (0, 0)),
                    pl.BlockSpec((1, D), lambda b, i: (0, 0)),
                    pl.BlockSpec((1, 1, 6 * D), lambda b, i: (b, 0, 0)),
                    pl.BlockSpec((1, 6 * D), lambda b, i: (0, 0)),
                    pl.BlockSpec((D, 128), lambda b, i: (0, 0)),
                    pl.BlockSpec((D, 128), lambda b, i: (0, 0))],
        out_specs=[pl.BlockSpec((1, TM_OUT, D), lambda b, i: (b, i, 0)),
                   pl.BlockSpec((1, TM_OUT * TOK_SUB, 128), lambda b, i: (b, i, 0)),
                   pl.BlockSpec((1, TM_OUT, 128), lambda b, i: (b, i, 0))],
        out_shape=[jax.ShapeDtypeStruct((B, nblk * TM_OUT, D), F32),
                   jax.ShapeDtypeStruct((B, nblk * TM_OUT * TOK_SUB, 128), F32),
                   jax.ShapeDtypeStruct((B, nblk * TM_OUT, 128), F32)],
        compiler_params=_cparams(("parallel", "parallel")),
    )(xall, *mixes, w_out_b, nw, ml, mc, rwh, rwl)


TM_MOE = 256
R_GATHER = 256


def _token_copy(src_hbm, idx, buf, r, sem):
    return pltpu.make_async_copy(src_hbm.at[pl.ds(idx * TOK_SUB, TOK_SUB)], buf.at[pl.ds(r * TOK_SUB, TOK_SUB)], sem)


def _gather_kernel(src_ref, f_hbm, o_ref, buf, sem):
    t = pl.program_id(0)

    def issue(step, slot):
        def body(r, carry):
            _token_copy(f_hbm, src_ref[step * R_GATHER + r], buf.at[slot], r, sem.at[slot]).start()
            return carry

        lax.fori_loop(0, R_GATHER, body, 0, unroll=8)

    @pl.when(t == 0)
    def _():
        issue(0, 0)

    @pl.when(t + 1 < pl.num_programs(0))
    def _():
        issue(t + 1, (t + 1) % 2)

    slot = t % 2

    def drain(r, carry):
        _token_copy(f_hbm, 0, buf.at[slot], r, sem.at[slot]).wait()
        return carry

    lax.fori_loop(0, R_GATHER, drain, 0, unroll=8)
    o_ref[...] = _load_token_rows(buf.at[slot], R_GATHER).astype(BF16)


def gather_rows(src, f_tok, nrows):
    return pl.pallas_call(
        _gather_kernel,
        grid_spec=pltpu.PrefetchScalarGridSpec(
            num_scalar_prefetch=1,
            grid=(nrows // R_GATHER,),
            in_specs=[pl.BlockSpec(memory_space=pl.ANY)],
            out_specs=pl.BlockSpec((R_GATHER, D), lambda t, s: (t, 0)),
            scratch_shapes=[pltpu.VMEM((2, R_GATHER * TOK_SUB, 128), F32), pltpu.SemaphoreType.DMA((2,))]),
        out_shape=jax.ShapeDtypeStruct((nrows, D), BF16),
        compiler_params=_cparams(("arbitrary",)),
    )(src, f_tok)


def _moe_kernel(te_ref, tv_ref, tn_ref, x_ref, wg_hbm, wu_hbm, wd_hbm, o_ref, sg, su, sd, wg_b, wu_b, wd_b, sem,
                *, layer):
    t = pl.program_id(0)
    e = te_ref[t]

    def copies(ex):
        return (pltpu.make_async_copy(wg_hbm.at[layer, ex], sg, sem.at[0]),
                pltpu.make_async_copy(wu_hbm.at[layer, ex], su, sem.at[1]),
                pltpu.make_async_copy(wd_hbm.at[layer, ex], sd, sem.at[2]))

    @pl.when(t == 0)
    def _():
        for cp in copies(e):
            cp.start()

    @pl.when((t == 0) | (e != te_ref[jnp.maximum(t - 1, 0)]))
    def _():
        for cp in copies(e):
            cp.wait()
        wg_b[...] = sg[...].astype(BF16)
        wu_b[...] = su[...].astype(BF16)
        wd_b[...] = sd[...].astype(BF16)
        nxt = tn_ref[t]

        @pl.when(nxt >= 0)
        def _():
            for cp in copies(nxt):
                cp.start()

    @pl.when(tv_ref[t] == 0)
    def _():
        o_ref[...] = jnp.zeros_like(o_ref)

    @pl.when(tv_ref[t] > 0)
    def _():
        x = x_ref[...]
        g = jnp.dot(x, wg_b[...], preferred_element_type=F32)
        u = jnp.dot(x, wu_b[...], preferred_element_type=F32)
        h = (g / (1.0 + jnp.exp(-g)) * u).astype(BF16)
        _store_token_rows(o_ref, jnp.dot(h, wd_b[...], preferred_element_type=F32))


def moe_experts(tile_e, tile_v, tile_n, xs, wg, wu, wd, layer):
    nrows = xs.shape[0]
    return pl.pallas_call(
        functools.partial(_moe_kernel, layer=layer),
        grid_spec=pltpu.PrefetchScalarGridSpec(
            num_scalar_prefetch=3,
            grid=(nrows // TM_MOE,),
            in_specs=[pl.BlockSpec((TM_MOE, D), lambda t, te, tv, tn: (t, 0)),
                      pl.BlockSpec(memory_space=pl.ANY),
                      pl.BlockSpec(memory_space=pl.ANY),
                      pl.BlockSpec(memory_space=pl.ANY)],
            out_specs=pl.BlockSpec((TM_MOE * TOK_SUB, 128), lambda t, te, tv, tn: (t, 0)),
            scratch_shapes=[pltpu.VMEM((D, D_EXPERT), F32), pltpu.VMEM((D, D_EXPERT), F32),
                            pltpu.VMEM((D_EXPERT, D), F32),
                            pltpu.VMEM((D, D_EXPERT), BF16), pltpu.VMEM((D, D_EXPERT), BF16),
                            pltpu.VMEM((D_EXPERT, D), BF16),
                            pltpu.SemaphoreType.DMA((3,))]),
        out_shape=jax.ShapeDtypeStruct((nrows * TOK_SUB, 128), F32),
        compiler_params=_cparams(("arbitrary",)),
    )(tile_e, tile_v, tile_n, xs, wg, wu, wd)


R_COMB = 256


def _combine_kernel(pos_ref, x_ref, w_ref, ml_ref, mc_ref, y_hbm, o_ref, buf, sem, *, i_off):
    nblk = pl.num_programs(1)
    step = pl.program_id(0) * nblk + pl.program_id(1)
    is_ctx = (pl.program_id(1) + i_off) == 0

    def issue(st, slot):
        base = st * (R_COMB * TOP_K)

        def body(r, carry):
            for k in range(TOP_K):
                _token_copy(y_hbm, pos_ref[base + TOP_K * r + k], buf.at[slot, k], r, sem.at[slot, k]).start()
            return carry

        lax.fori_loop(0, R_COMB, body, 0, unroll=4)

    @pl.when(step == 0)
    def _():
        issue(0, 0)

    @pl.when(step + 1 < pl.num_programs(0) * nblk)
    def _():
        issue(step + 1, (step + 1) % 2)

    slot = step % 2

    def drain(r, carry):
        for k in range(TOP_K):
            _token_copy(y_hbm, 0, buf.at[slot, k], r, sem.at[slot, k]).wait()
        return carry

    lax.fori_loop(0, R_COMB, drain, 0, unroll=4)
    gf = jnp.where(is_ctx, mc_ref[:, 5 * D:6 * D], ml_ref[0, :, 5 * D:6 * D])
    w0 = w_ref[0, :, 0:1]
    w1 = w_ref[0, :, 1:2]
    for s in range(TOK_SUB):
        cs = slice(s * 128, (s + 1) * 128)
        y = (w0 * buf[slot, 0, pl.ds(s, R_COMB, stride=TOK_SUB), :]
             + w1 * buf[slot, 1, pl.ds(s, R_COMB, stride=TOK_SUB), :])
        o_ref[0, :, cs] = x_ref[0, :, cs] + gf[:, cs] * y


def moe_combine(pos, wts, xall, ml, mc, y_sorted, with_ctx):
    i_off = 0 if with_ctx else LC // R_COMB
    rows_out = xall.shape[1]
    nblk = rows_out // R_COMB
    return pl.pallas_call(
        functools.partial(_combine_kernel, i_off=i_off),
        grid_spec=pltpu.PrefetchScalarGridSpec(
            num_scalar_prefetch=1,
            grid=(B, nblk),
            in_specs=[pl.BlockSpec((1, R_COMB, D), lambda b, i, p: (b, i, 0)),
                      pl.BlockSpec((1, R_COMB, TOP_K), lambda b, i, p: (b, i, 0)),
                      pl.BlockSpec((1, 1, 6 * D), lambda b, i, p: (b, 0, 0)),
                      pl.BlockSpec((1, 6 * D), lambda b, i, p: (0, 0)),
                      pl.BlockSpec(memory_space=pl.ANY)],
            out_specs=pl.BlockSpec((1, R_COMB, D), lambda b, i, p: (b, i, 0)),
            scratch_shapes=[pltpu.VMEM((2, TOP_K, R_COMB * TOK_SUB, 128), F32),
                            pltpu.SemaphoreType.DMA((2, TOP_K))]),
        out_shape=jax.ShapeDtypeStruct((B, rows_out, D), F32),
        compiler_params=_cparams(("arbitrary", "arbitrary")),
    )(pos, xall, wts.reshape(B, rows_out, TOP_K), ml, mc, y_sorted)


def route(logits, router_bias):
    T = logits.shape[0]
    probs = jax.nn.softmax(logits, axis=-1)
    sel = (probs + router_bias.astype(F32)).reshape(T, N_GROUPS, EPG)
    group = jnp.argmax(jnp.max(sel, axis=-1), axis=-1)
    sel_g = jnp.take_along_axis(sel, group[:, None, None], axis=1)[:, 0]
    _, idx = lax.top_k(sel_g, TOP_K)
    expert = (group[:, None] * EPG + idx).astype(jnp.int32)
    wts = jnp.take_along_axis(probs, expert, axis=-1)
    wts = wts / jnp.sum(wts, axis=-1, keepdims=True)
    flat_e = expert.reshape(-1)
    onehot = (flat_e[:, None] == jnp.arange(N_EXPERTS, dtype=jnp.int32)[None, :]).astype(jnp.int32)
    csum = jnp.cumsum(onehot, axis=0)
    rank = jnp.sum(csum * onehot, axis=-1) - 1
    counts = csum[-1]
    ptiles = (counts + TM_MOE - 1) // TM_MOE
    tend = jnp.cumsum(ptiles)
    tstart = tend - ptiles
    dest = (tstart[flat_e] * TM_MOE + rank).astype(jnp.int32)
    ntiles = -(-(TOP_K * T) // TM_MOE) + N_EXPERTS
    nrows = ntiles * TM_MOE
    tid = jnp.arange(ntiles, dtype=jnp.int32)
    tile_v = (tid < tend[-1]).astype(jnp.int32)
    tile_e = jnp.searchsorted(tend, jnp.minimum(tid, tend[-1] - 1), side="right").astype(jnp.int32)
    eid = jnp.arange(N_EXPERTS, dtype=jnp.int32)
    later = (eid[None, :] > eid[:, None]) & (ptiles[None, :] > 0)
    nxt = jnp.min(jnp.where(later, eid[None, :], N_EXPERTS), axis=1)
    tile_n = jnp.where(nxt < N_EXPERTS, nxt, -1)[tile_e].astype(jnp.int32)
    src = jnp.zeros((nrows,), jnp.int32).at[dest].set(jnp.repeat(jnp.arange(T, dtype=jnp.int32), TOP_K))
    return src, wts, dest, tile_e, tile_v, tile_n


GLA_QW = GLA_HEADS * GLA_DK
N_BLK = L // GLA_CHUNK
N_BLK_C = LC // GLA_CHUNK


def _log_sigmoid(z):
    return jnp.minimum(z, 0.0) - jnp.log1p(jnp.exp(-jnp.abs(z)))


def _head_rms(x, w):
    R, W = x.shape
    lane = lax.broadcasted_iota(jnp.int32, (R, 128), 1)
    left = lane < 64
    outs = []
    for c in range(W // 128):
        xc = x[:, c * 128:(c + 1) * 128]
        sq = xc * xc
        sl = jnp.sum(jnp.where(left, sq, 0.0), axis=-1, keepdims=True)
        sr = jnp.sum(jnp.where(left, 0.0, sq), axis=-1, keepdims=True)
        inv = jnp.where(left, lax.rsqrt(sl * (1.0 / 64) + EPS), lax.rsqrt(sr * (1.0 / 64) + EPS))
        outs.append(xc * inv * w)
    return jnp.concatenate(outs, axis=-1)


def _gla_kernel(q_ref, k_ref, v_ref, g_ref, a_ref, wf_ref, wb_ref, gb_ref, nw_ref, ind_ref, bdm_ref, o_ref,
                lf_ref, lb_ref, acc_ref, sf_ref, sb_ref, *, row_off):
    hi = lax.Precision.HIGHEST
    C = GLA_CHUNK
    a = a_ref[0].astype(F32)
    rin = lax.broadcasted_iota(jnp.int32, (L, 1), 0) & (C - 1)

    def block_cumsum(x, forward):
        for sh in (1, 2, 4, 8):
            if forward:
                x = x + jnp.where(rin >= sh, pltpu.roll(x, sh, 0), 0.0)
            else:
                x = x + jnp.where(rin < C - sh, pltpu.roll(x, L - sh, 0), 0.0)
        return x

    lf_ref[...] = block_cumsum(_log_sigmoid(jnp.dot(a, wf_ref[...], preferred_element_type=F32, precision=hi)
                                            + gb_ref[0:1, :]) * (1.0 / GLA_TEMP), True)
    lb_ref[...] = block_cumsum(_log_sigmoid(jnp.dot(a, wb_ref[...], preferred_element_type=F32, precision=hi)
                                            + gb_ref[1:2, :]) * (1.0 / GLA_TEMP), False)
    acc_ref[...] = jnp.zeros_like(acc_ref)
    sf_ref[...] = jnp.zeros_like(sf_ref)
    sb_ref[...] = jnp.zeros_like(sb_ref)

    rowi = lax.broadcasted_iota(jnp.int32, (C, GLA_QW), 0)

    def block(r0, b_ref, s_ref, forward):
        q = q_ref[0, pl.ds(r0, C), :].astype(F32) * (GLA_DK ** -0.5)
        k = k_ref[0, pl.ds(r0, C), :].astype(F32)
        v = v_ref[0, pl.ds(r0, C), :].astype(F32)
        b = b_ref[pl.ds(r0, C), :]
        btot = b[C - 1:C, :] if forward else b[0:1, :]
        st = s_ref[...]
        o = lax.dot_general((q * jnp.exp(b)).astype(BF16), st.astype(BF16), (((1,), (1,)), ((), ())),
                            preferred_element_type=F32)
        ps = []
        for s in range(C):
            seen = (rowi >= s) if forward else (rowi <= s)
            e = jnp.exp(jnp.where(seen, b - b[s:s + 1, :], -jnp.inf))
            ps.append(q * k[s:s + 1, :] * e)
        pm = jnp.concatenate(ps, axis=0).astype(BF16)
        rm = jnp.dot(pm, ind_ref[...], preferred_element_type=F32)
        for s in range(C):
            o = o + rm[s * C:(s + 1) * C, :] * v[s:s + 1, :]
        acc_ref[pl.ds(r0, C), :] += o
        kd = (k * jnp.exp(btot - b)).astype(BF16)
        upd = lax.dot_general(v.astype(BF16), kd, (((0,), (0,)), ((), ())), preferred_element_type=F32)
        s_ref[...] = st * jnp.exp(btot) + upd * bdm_ref[...]

    def body(i, carry):
        block(pl.multiple_of(i * C, C), lf_ref, sf_ref, True)
        jb = jnp.where(i < N_BLK_C, N_BLK_C - 1 - i, N_BLK + N_BLK_C - 1 - i)
        block(pl.multiple_of(jb * C, C), lb_ref, sb_ref, False)
        return carry

    lax.fori_loop(0, N_BLK, body, 0)
    o = _head_rms(acc_ref[row_off:, :], nw_ref[...])
    g = g_ref[0, row_off:, :].astype(F32)
    o_ref[0] = (o * (g / (1.0 + jnp.exp(-g)))).astype(o_ref.dtype)


def gla_mixer(u, gate_w, gate_b, norm_w, with_ctx):
    row_off = 0 if with_ctx else LC
    qc, kc = U_OFF["gla_q"] // 256, U_OFF["gla_k"] // 256
    vc, gc, ac = U_OFF["gla_v"] // 512, U_OFF["gla_g"] // 512, U_OFF["gla_a"] // 128
    wf = jnp.zeros((128, GLA_QW), F32).at[0:GLA_RANK].set(gate_w[0])
    wb = jnp.zeros((128, GLA_QW), F32).at[GLA_RANK:2 * GLA_RANK].set(gate_w[1])
    hd = np.arange(GLA_QW)[:, None] // GLA_DK == np.arange(GROUP_W)[None, :] // GLA_DV
    ind = jnp.asarray(hd, BF16)
    bdm = jnp.asarray(hd.T, F32)
    full = lambda *shape: pl.BlockSpec(shape, lambda b: (0,) * len(shape))
    return pl.pallas_call(
        functools.partial(_gla_kernel, row_off=row_off),
        grid=(B,),
        in_specs=[pl.BlockSpec((1, L, 256), lambda b: (b, 0, qc)),
                  pl.BlockSpec((1, L, 256), lambda b: (b, 0, kc)),
                  pl.BlockSpec((1, L, 512), lambda b: (b, 0, vc)),
                  pl.BlockSpec((1, L, 512), lambda b: (b, 0, gc)),
                  pl.BlockSpec((1, L, 128), lambda b: (b, 0, ac)),
                  full(128, GLA_QW), full(128, GLA_QW), full(2, GLA_QW), full(1, 128),
                  full(GLA_QW, GROUP_W), full(GROUP_W, GLA_QW)],
        out_specs=pl.BlockSpec((1, L - row_off, GROUP_W), lambda b: (b, 0, 0)),
        out_shape=jax.ShapeDtypeStruct((B, L - row_off, GROUP_W), BF16),
        scratch_shapes=[pltpu.VMEM((L, GLA_QW), F32), pltpu.VMEM((L, GLA_QW), F32), pltpu.VMEM((L, GROUP_W), F32),
                        pltpu.VMEM((GROUP_W, GLA_QW), F32), pltpu.VMEM((GROUP_W, GLA_QW), F32)],
        compiler_params=_cparams(("parallel",)),
    )(u, u, u, u, u, wf, wb, gate_b, jnp.tile(norm_w, 2)[None], ind, bdm)


HY_CT = 256
HY_TK = 512


@functools.lru_cache(maxsize=None)
def _dft_consts(Lh):
    k = np.arange(Lh, dtype=np.int64)
    ph = (np.outer(k, k) % (2 * Lh)).astype(np.float64) * (np.pi / Lh)
    sgn = (1.0 - 2.0 * (k % 2)).astype(np.float32)[:, None]
    return np.cos(ph).astype(np.float32), np.sin(ph).astype(np.float32), sgn


def _hyena_features(Lh):
    t = jnp.linspace(0.0, 1.0, Lh, dtype=F32)[:, None]
    w = (2.0 * math.pi / Lh) * jnp.arange(Lh, dtype=F32)[:, None]
    bands = jnp.linspace(1e-4, HY_BANDS - 1, HY_BANDS, dtype=F32)
    z = jnp.concatenate([t, jnp.cos(w * bands), -jnp.sin(w * bands)], axis=-1)
    return jnp.pad(z, ((0, 0), (0, 128 - z.shape[1])))


def _hy_filter_kernel(z_ref, w1_ref, b1_ref, w2_ref, b2_ref, fr_ref, w3f_ref, w3b_ref, dl_ref, c_ref, s_ref, sgn_ref,
                      hc_ref, hs_ref, hn_ref, *, Lh):
    hi = lax.Precision.HIGHEST
    z = z_ref[...]
    h = jnp.sin(fr_ref[0:1, :] * (jnp.dot(z, w1_ref[...], preferred_element_type=F32, precision=hi) + b1_ref[...]))
    h = jnp.sin(fr_ref[1:2, :] * (jnp.dot(h, w2_ref[...], preferred_element_type=F32, precision=hi) + b2_ref[...]))
    dec = jnp.exp(-(z[:, 0:1] * dl_ref[...]))
    hf = jnp.dot(h, w3f_ref[...], preferred_element_type=F32, precision=hi) * dec
    hb = jnp.dot(h, w3b_ref[...], preferred_element_type=F32, precision=hi) * dec
    nrm = jnp.sum(jnp.abs(hf), axis=0, keepdims=True) + jnp.sum(jnp.abs(hb), axis=0, keepdims=True)
    row = lax.broadcasted_iota(jnp.int32, (Lh, 1), 0)
    hf = hf / nrm
    hb = jnp.where(row == 0, 0.0, hb / nrm)
    wk = jnp.where(row == 0, 0.5 / Lh, 1.0 / Lh)

    def project(m_ref, x):
        xh = x.astype(BF16)
        xl = (x - xh.astype(F32)).astype(BF16)
        return (jnp.dot(m_ref[...], xh, preferred_element_type=F32)
                + jnp.dot(m_ref[...], xl, preferred_element_type=F32))

    am = hf + hb
    hc_ref[0] = project(c_ref, am) * wk
    hs_ref[0] = project(s_ref, hf - hb) * wk
    hn_ref[0] = jnp.sum(am * sgn_ref[...], axis=0, keepdims=True) * (0.5 / Lh)


def hyena_spectrum(Lh, w1, b1, w2, b2, w3, fr, cmat, smat, sgn):
    z = _hyena_features(Lh)
    w1p = jnp.pad(w1, ((0, 128 - w1.shape[0]), (0, 0)))
    deltas = jnp.linspace(HY_MIN_DECAY, HY_MAX_DECAY, HY_CH, dtype=F32)[None]
    nct = HY_CH // HY_CT
    full = lambda *shape: pl.BlockSpec(shape, lambda o, c: (0,) * len(shape))
    return pl.pallas_call(
        functools.partial(_hy_filter_kernel, Lh=Lh),
        grid=(HY_ORDER, nct),
        in_specs=[full(Lh, 128), full(128, 64), full(1, 64), full(64, 64), full(1, 64), full(2, 64),
                  pl.BlockSpec((64, HY_CT), lambda o, c: (0, o * 2 * nct + c)),
                  pl.BlockSpec((64, HY_CT), lambda o, c: (0, o * 2 * nct + nct + c)),
                  pl.BlockSpec((1, HY_CT), lambda o, c: (0, c)),
                  full(Lh, Lh), full(Lh, Lh), full(Lh, 1)],
        out_specs=[pl.BlockSpec((1, Lh, HY_CT), lambda o, c: (o, 0, c)),
                   pl.BlockSpec((1, Lh, HY_CT), lambda o, c: (o, 0, c)),
                   pl.BlockSpec((1, 1, HY_CT), lambda o, c: (o, 0, c))],
        out_shape=[jax.ShapeDtypeStruct((HY_ORDER, Lh, HY_CH), F32),
                   jax.ShapeDtypeStruct((HY_ORDER, Lh, HY_CH), F32),
                   jax.ShapeDtypeStruct((HY_ORDER, 1, HY_CH), F32)],
        compiler_params=_cparams(("arbitrary", "arbitrary")),
    )(z, w1p, b1[None], w2, b2[None], fr, w3, w3, deltas, cmat, smat, sgn)


def _hyena_seq(row0, Lh, x1_ref, x2_ref, y_ref, cw_refs, cb_refs, bias_ref, c_ref, s_ref, hc_ref, hs_ref, hn_ref, *,
               scratch):
    r = lax.broadcasted_iota(jnp.int32, (Lh, 1), 0)

    def sconv(u_ref, w_ref, b_ref):
        u = u_ref[0, row0:row0 + Lh, :].astype(F32)
        up = jnp.where(r == 0, 0.0, pltpu.roll(u, 1, 0))
        dn = jnp.where(r == Lh - 1, 0.0, pltpu.roll(u, Lh - 1, 0))
        return up * w_ref[0:1, :] + u * w_ref[1:2, :] + dn * w_ref[2:3, :] + b_ref[...]

    gate_refs = (x1_ref, x2_ref)
    ys_ref, yb_ref, cv_ref = scratch
    rows = slice(0, Lh)
    ys_ref[rows, :] = sconv(y_ref, cw_refs[2], cb_refs[2])
    sgn = (1 - 2 * (r & 1)).astype(F32)
    tk = min(HY_TK, Lh)
    for o in range(HY_ORDER):
        y = ys_ref[rows, :]
        yb_ref[rows, :] = y.astype(BF16)
        cv_ref[rows, :] = sgn * (jnp.sum(y * sgn, axis=0, keepdims=True) * hn_ref[o])
        for m in range(Lh // tk):
            fs = slice(m * tk, (m + 1) * tk)
            yc = jnp.dot(c_ref[fs, :], yb_ref[rows, :], preferred_element_type=F32)
            ysn = jnp.dot(s_ref[fs, :], yb_ref[rows, :], preferred_element_type=F32)
            hc = hc_ref[o, fs, :]
            hs = hs_ref[o, fs, :]
            pc = (yc * hc - ysn * hs).astype(BF16)
            ps = (yc * hs + ysn * hc).astype(BF16)
            cv_ref[rows, :] += (jnp.dot(c_ref[:, fs], pc, preferred_element_type=F32)
                                + jnp.dot(s_ref[:, fs], ps, preferred_element_type=F32))
        ys_ref[rows, :] = (sconv(gate_refs[o], cw_refs[o], cb_refs[o])
                           * (cv_ref[rows, :] + ys_ref[rows, :] * bias_ref[o:o + 1, :]))
    return ys_ref[rows, :]


def _hyena_kernel(x1_ref, x2_ref, y_ref, w1_ref, w2_ref, w3_ref, b1_ref, b2_ref, b3_ref, bias_ref, *rest, with_ctx):
    nmat = 10 if with_ctx else 5
    o_ref = rest[nmat]
    scratch = rest[nmat + 1:]
    seqs = [(LC, S, rest[0:5])]
    if with_ctx:
        seqs.append((0, LC, rest[5:10]))
    for row0, Lh, mats in seqs:
        y = _hyena_seq(row0, Lh, x1_ref, x2_ref, y_ref, (w1_ref, w2_ref, w3_ref), (b1_ref, b2_ref, b3_ref),
                       bias_ref, *mats, scratch=scratch)
        out0 = row0 if with_ctx else 0
        o_ref[0, out0:out0 + Lh, :] = y.astype(o_ref.dtype)


def hyena_mixer(u, conv_w, conv_b, w1, b1, w2, b2, w3, fr, bias, with_ctx):
    nct = HY_CH // HY_CT
    hc0 = U_OFF["hy"] // HY_CT
    once = pl.Buffered(1)
    consts = []
    specs = []
    for Lh in ((S, LC) if with_ctx else (S,)):
        cm, sm, sgn = _dft_consts(Lh)
        cm, sm, sgn = jnp.asarray(cm, BF16), jnp.asarray(sm, BF16), jnp.asarray(sgn)
        hc, hs, hn = hyena_spectrum(Lh, w1, b1, w2, b2, w3, fr, cm, sm, sgn)
        consts += [cm, sm, hc, hs, hn]
        specs += [pl.BlockSpec((Lh, Lh), lambda c, b: (0, 0), pipeline_mode=once),
                  pl.BlockSpec((Lh, Lh), lambda c, b: (0, 0), pipeline_mode=once),
                  pl.BlockSpec((HY_ORDER, Lh, HY_CT), lambda c, b: (0, 0, c), pipeline_mode=once),
                  pl.BlockSpec((HY_ORDER, Lh, HY_CT), lambda c, b: (0, 0, c), pipeline_mode=once),
                  pl.BlockSpec((HY_ORDER, 1, HY_CT), lambda c, b: (0, 0, c), pipeline_mode=once)]
    rows_out = L if with_ctx else S
    ublk = lambda j: pl.BlockSpec((1, L, HY_CT), lambda c, b: (b, 0, hc0 + j * nct + c))
    wblk = lambda j: pl.BlockSpec((3, HY_CT), lambda c, b: (0, j * nct + c))
    bblk = lambda j: pl.BlockSpec((1, HY_CT), lambda c, b: (0, j * nct + c))
    return pl.pallas_call(
        functools.partial(_hyena_kernel, with_ctx=with_ctx),
        grid=(nct, B),
        in_specs=[ublk(0), ublk(1), ublk(2), wblk(0), wblk(1), wblk(2), bblk(0), bblk(1), bblk(2),
                  pl.BlockSpec((HY_ORDER, HY_CT), lambda c, b: (0, c))] + specs,
        out_specs=pl.BlockSpec((1, rows_out, HY_CT), lambda c, b: (b, 0, c)),
        out_shape=jax.ShapeDtypeStruct((B, rows_out, HY_CH), BF16),
        scratch_shapes=[pltpu.VMEM((S, HY_CT), F32), pltpu.VMEM((S, HY_CT), BF16), pltpu.VMEM((S, HY_CT), F32)],
        compiler_params=_cparams(("arbitrary", "arbitrary")),
    )(u, u, u, conv_w, conv_w, conv_w, conv_b[None], conv_b[None], conv_b[None], bias, *consts)


def _permute_w_in(w):
    parts = []
    for n in U_ORDER:
        o, wd = REF_COLS[n]
        p = w[:, o:o + wd]
        if wd < 128:
            p = jnp.pad(p, ((0, 0), (0, 128 - wd)))
        parts.append(p)
    parts.append(jnp.zeros((w.shape[0], U_W - U_USED), w.dtype))
    return jnp.concatenate(parts, axis=1).astype(BF16)


def kernel(x, c, ctx, c_ctx, norm1_w, norm2_w, ada_w, ada_b, w_in, w_out, gla_gate_w, gla_gate_b, gla_norm_w,
           swa_q_norm_w, swa_k_norm_w, swa_sink, hyena_conv_w, hyena_conv_b, hyena_ffn_w1, hyena_ffn_b1,
           hyena_ffn_w2, hyena_ffn_b2, hyena_ffn_w3, hyena_ffn_freq, hyena_bias, diff_q_norm_w, diff_k_norm_w,
           diff_lambda, diff_subln_w, router_w, router_bias, expert_w_gate, expert_w_up, expert_w_down):
    assert x.shape == (B, S, D) and ctx.shape == (B, LC, D)
    cc = jnp.zeros((16, D), F32).at[:B].set(c).at[B].set(c_ctx)
    mods = ada_mod(cc, ada_w, ada_b)
    cos, sin = rope_tables128()
    rw_pad = jnp.pad(router_w, ((0, 0), (0, 128 - N_EXPERTS)))
    xall = jnp.concatenate([ctx, x], axis=1)

    for l in range(DEPTH):
        with_ctx = l < DEPTH - 1
        lam_init = 0.8 - 0.6 * math.exp(-0.3 * l)
        ml = mods[l, :B].reshape(B, 1, 6 * D)
        mc = mods[l, B:B + 1]
        u = in_proj(xall, norm1_w[l][None], ml, mc, _permute_w_in(w_in[l]))
        mix_a = gla_mixer(u, gla_gate_w[l], gla_gate_b[l], gla_norm_w[l], with_ctx)
        mix_b = swa_mixer(u, swa_sink[l], swa_q_norm_w[l], swa_k_norm_w[l], cos, sin, with_ctx)
        mix_y = hyena_mixer(u, hyena_conv_w[l], hyena_conv_b[l], hyena_ffn_w1[l], hyena_ffn_b1[l],
                            hyena_ffn_w2[l], hyena_ffn_b2[l], hyena_ffn_w3[l], hyena_ffn_freq[l],
                            hyena_bias[l], with_ctx)
        mix_d = diff_mixer(u, diff_q_norm_w[l], diff_k_norm_w[l], cos, sin, diff_lambda[l], diff_subln_w[l],
                           lam_init, with_ctx)
        xall, f, logits = out_proj(xall, (mix_a, mix_b, mix_y, mix_d), w_out[l].astype(BF16), norm2_w[l][None],
                                   ml, mc, rw_pad, with_ctx)
        T = B * xall.shape[1]
        src, wts, dest, tile_e, tile_v, tile_n = route(logits.reshape(T, 128)[:, :N_EXPERTS], router_bias)
        xs = gather_rows(src, f.reshape(T * TOK_SUB, 128), src.shape[0])
        ys = moe_experts(tile_e, tile_v, tile_n, xs, expert_w_gate, expert_w_up, expert_w_down, l)
        xall = moe_combine(dest, wts, xall, ml, mc, ys, with_ctx)
    return xall
```

```python
import functools
import math

import numpy as np
import jax
import jax.numpy as jnp
from jax import lax
from jax.experimental import pallas as pl
from jax.experimental.pallas import tpu as pltpu

F32 = jnp.float32
BF16 = jnp.bfloat16

D = 2048
B = 8
S = 2048
LC = 256
L = LC + S
DEPTH = 2
GRID_W = 64
HEAD_DIM = 64
ROPE_THETA = 10000.0
EPS = 1e-6
GROUP_W = 512

GLA_DV = 64
GLA_DK = 32
GLA_HEADS = 8
GLA_RANK = 16
GLA_TEMP = 16.0
GLA_CHUNK = 16
SWA_HEADS = 8
SWA_KV = 2
SWA_WINDOW = 128
HY_CH = 512
HY_ORDER = 2
HY_BANDS = 16
HY_TARGET = 1e-2
HY_MIN_DECAY = math.log(1.0 / HY_TARGET) / 1.5
HY_MAX_DECAY = math.log(1.0 / HY_TARGET) / 0.3
DIFF_HEADS = 4
N_EXPERTS = 16
N_GROUPS = 4
EPG = 4
TOP_K = 2
D_EXPERT = 1024

REF_COLS = dict(gla_q=(0, 256), gla_k=(256, 256), gla_v=(512, 512), gla_g=(1024, 512), gla_a=(1536, 32),
                swa_q=(1568, 512), swa_k=(2080, 128), swa_v=(2208, 128), hy=(2336, 1536),
                diff_q=(3872, 512), diff_k=(4384, 512), diff_v=(4896, 512))
U_ORDER = ("hy", "gla_v", "gla_g", "swa_q", "diff_q", "diff_k", "diff_v", "gla_q", "gla_k", "swa_k", "swa_v", "gla_a")
U_OFF = {}
_o = 0
for _n in U_ORDER:
    U_OFF[_n] = _o
    _o += max(REF_COLS[_n][1], 128)
U_USED = _o
TN_IN = 512
U_W = -(-U_USED // TN_IN) * TN_IN

VMEM_LIMIT = 56 * 1024 * 1024


def _cparams(sem):
    return pltpu.CompilerParams(dimension_semantics=sem, vmem_limit_bytes=VMEM_LIMIT)


def _ada_kernel(c_ref, w_ref, b_ref, o_ref):
    c = c_ref[...]
    a = c / (1.0 + jnp.exp(-c))
    o_ref[0] = jnp.dot(a, w_ref[0], preferred_element_type=F32, precision=lax.Precision.HIGHEST) + b_ref[0]


def ada_mod(cc, ada_w, ada_b):
    tn = 1024
    return pl.pallas_call(
        _ada_kernel,
        grid=(DEPTH, 6 * D // tn),
        in_specs=[pl.BlockSpec((16, D), lambda l, j: (0, 0)),
                  pl.BlockSpec((1, D, tn), lambda l, j: (l, 0, j)),
                  pl.BlockSpec((1, 1, tn), lambda l, j: (l, 0, j))],
        out_specs=pl.BlockSpec((1, 16, tn), lambda l, j: (l, 0, j)),
        out_shape=jax.ShapeDtypeStruct((DEPTH, 16, 6 * D), F32),
        compiler_params=_cparams(("parallel", "parallel")),
    )(cc, ada_w, ada_b.reshape(DEPTH, 1, 6 * D))


TM_IN = 1152
CH_IN = 32


def _inproj_kernel(x_ref, nw_ref, scl_ref, shl_ref, scc_ref, shc_ref, w_ref, o_ref, h_ref):
    i = pl.program_id(1)

    @pl.when(pl.program_id(2) == 0)
    def _():
        def chunk(c, carry):
            r0 = pl.multiple_of(c * CH_IN, CH_IN)
            x = x_ref[0, pl.ds(r0, CH_IN), :]
            ms = jnp.mean(x * x, axis=-1, keepdims=True)
            y = x * lax.rsqrt(ms + EPS) * nw_ref[...]
            row = i * TM_IN + r0 + lax.broadcasted_iota(jnp.int32, (CH_IN, 1), 0)
            is_ctx = row < LC
            sc = jnp.where(is_ctx, scc_ref[...], scl_ref[0])
            sh = jnp.where(is_ctx, shc_ref[...], shl_ref[0])
            h_ref[pl.ds(r0, CH_IN), :] = (y * (1.0 + sc) + sh).astype(BF16)
            return carry

        lax.fori_loop(0, TM_IN // CH_IN, chunk, 0)

    o_ref[0] = jnp.dot(h_ref[...], w_ref[...], preferred_element_type=F32).astype(o_ref.dtype)


def in_proj(xall, nw, ml, mc, w_in_p):
    return pl.pallas_call(
        _inproj_kernel,
        grid=(B, L // TM_IN, U_W // TN_IN),
        in_specs=[pl.BlockSpec((1, TM_IN, D), lambda b, i, j: (b, i, 0)),
                  pl.BlockSpec((1, D), lambda b, i, j: (0, 0)),
                  pl.BlockSpec((1, 1, D), lambda b, i, j: (b, 0, 1)),
                  pl.BlockSpec((1, 1, D), lambda b, i, j: (b, 0, 0)),
                  pl.BlockSpec((1, D), lambda b, i, j: (0, 1)),
                  pl.BlockSpec((1, D), lambda b, i, j: (0, 0)),
                  pl.BlockSpec((D, TN_IN), lambda b, i, j: (0, j))],
        out_specs=pl.BlockSpec((1, TM_IN, TN_IN), lambda b, i, j: (b, i, j)),
        out_shape=jax.ShapeDtypeStruct((B, L, U_W), BF16),
        scratch_shapes=[pltpu.VMEM((TM_IN, D), BF16)],
        compiler_params=_cparams(("parallel", "parallel", "arbitrary")),
    )(xall, nw, ml, ml, mc, mc, w_in_p)


def _norm_rope(x, w, cos, sin):
    R, W = x.shape
    lane = lax.broadcasted_iota(jnp.int32, (R, 128), 1)
    left = lane < HEAD_DIM
    first = (lane & 31) < 16
    outs = []
    for c in range(W // 128):
        xc = x[:, c * 128:(c + 1) * 128]
        sq = xc * xc
        sl = jnp.sum(jnp.where(left, sq, 0.0), axis=-1, keepdims=True)
        sr = jnp.sum(jnp.where(left, 0.0, sq), axis=-1, keepdims=True)
        inv = jnp.where(left, lax.rsqrt(sl * (1.0 / HEAD_DIM) + EPS), lax.rsqrt(sr * (1.0 / HEAD_DIM) + EPS))
        y = xc * inv * w
        rot = jnp.where(first, pltpu.roll(y, 112, 1), pltpu.roll(y, 16, 1))
        outs.append(y * cos + rot * sin)
    return outs[0] if len(outs) == 1 else jnp.concatenate(outs, axis=-1)


def rope_tables128():
    n = HEAD_DIM // 4
    rows = S // GRID_W
    row_ids = jnp.repeat(jnp.arange(rows), GRID_W).astype(F32)
    col_ids = jnp.tile(jnp.arange(GRID_W), rows).astype(F32)
    inv = ROPE_THETA ** (-jnp.arange(n, dtype=F32) / n)
    ar = row_ids[:, None] * inv
    ac = col_ids[:, None] * inv
    cos64 = jnp.concatenate([jnp.cos(ar), jnp.cos(ar), jnp.cos(ac), jnp.cos(ac)], axis=-1)
    sin64 = jnp.concatenate([-jnp.sin(ar), jnp.sin(ar), -jnp.sin(ac), jnp.sin(ac)], axis=-1)
    cos = jnp.concatenate([jnp.ones((LC, 64), F32), cos64], axis=0)
    sin = jnp.concatenate([jnp.zeros((LC, 64), F32), sin64], axis=0)
    return jnp.tile(cos, (1, 2)), jnp.tile(sin, (1, 2))


def _row_max(s):
    m = s[:, 0:128]
    for c in range(1, s.shape[1] // 128):
        m = jnp.maximum(m, s[:, c * 128:(c + 1) * 128])
    return jnp.max(m, axis=-1, keepdims=True)


def _row_sum(s):
    a = s[:, 0:128]
    for c in range(1, s.shape[1] // 128):
        a = a + s[:, c * 128:(c + 1) * 128]
    return jnp.sum(a, axis=-1, keepdims=True)


SWA_BAND = 3 * SWA_WINDOW


def _swa_kernel(sink_ref, q_ref, k_ref, v_ref, qw_ref, kw_ref, cos_ref, sin_ref, o_ref, kn_ref, *, n_off):
    n = pl.program_id(1) + n_off

    @pl.when(pl.program_id(1) == 0)
    def _():
        kn_ref[...] = _norm_rope(k_ref[0].astype(F32), kw_ref[...], cos_ref[...], sin_ref[...]).astype(BF16)

    r0 = pl.multiple_of(n * 128, 128)
    q = _norm_rope(q_ref[0].astype(F32), qw_ref[...], cos_ref[pl.ds(r0, 128), :], sin_ref[pl.ds(r0, 128), :])
    q = (q * (HEAD_DIM ** -0.5)).astype(BF16)

    nb = n - LC // 128
    is_lat = nb >= 0
    lstart = jnp.clip((nb - 1) * 128, 0, S - SWA_BAND)
    start = pl.multiple_of(LC + lstart, 128)
    kk = jnp.concatenate([kn_ref[0:LC, :], kn_ref[pl.ds(start, SWA_BAND), :]], axis=0)
    vv = jnp.concatenate([v_ref[0, 0:LC, :], v_ref[0, pl.ds(start, SWA_BAND), :]], axis=0)
    nk = LC + SWA_BAND
    row = lax.broadcasted_iota(jnp.int32, (512, nk), 0) & 127
    col = lax.broadcasted_iota(jnp.int32, (512, nk), 1)
    qpos = nb * 128 + row
    kpos = lstart + col - LC
    valid = (col < LC) | ((jnp.abs(qpos - kpos) <= SWA_WINDOW) & is_lat)
    rowi = lax.broadcasted_iota(jnp.int32, (512, 1), 0)
    outs = [None] * SWA_HEADS
    for hk in range(SWA_KV):
        qs = jnp.concatenate([q[:, (hk * 4 + g) * 64:(hk * 4 + g + 1) * 64] for g in range(4)], axis=0)
        s = lax.dot_general(qs, kk[:, hk * 64:(hk + 1) * 64], (((1,), (1,)), ((), ())),
                            preferred_element_type=F32)
        s = jnp.where(valid, s, -jnp.inf)
        sk = jnp.where(rowi < 128, sink_ref[hk * 4],
                       jnp.where(rowi < 256, sink_ref[hk * 4 + 1],
                                 jnp.where(rowi < 384, sink_ref[hk * 4 + 2], sink_ref[hk * 4 + 3])))
        m = jnp.maximum(_row_max(s), sk)
        e = jnp.exp(s - m)
        den = _row_sum(e) + jnp.exp(sk - m)
        o = jnp.dot(e.astype(BF16), vv[:, hk * 64:(hk + 1) * 64], preferred_element_type=F32) / den
        for g in range(4):
            outs[hk * 4 + g] = o[g * 128:(g + 1) * 128]
    o_ref[0] = jnp.concatenate(outs, axis=-1).astype(o_ref.dtype)


def swa_mixer(u, sink, qw, kw, cos, sin, with_ctx):
    n_off = 0 if with_ctx else LC // 128
    nblk = L // 128 - n_off
    qc, kc, vc = U_OFF["swa_q"] // 512, U_OFF["swa_k"] // 128, U_OFF["swa_v"] // 128
    return pl.pallas_call(
        functools.partial(_swa_kernel, n_off=n_off),
        grid_spec=pltpu.PrefetchScalarGridSpec(
            num_scalar_prefetch=1,
            grid=(B, nblk),
            in_specs=[pl.BlockSpec((1, 128, 512), lambda b, n, s: (b, n + n_off, qc)),
                      pl.BlockSpec((1, L, 128), lambda b, n, s: (b, 0, kc)),
                      pl.BlockSpec((1, L, 128), lambda b, n, s: (b, 0, vc)),
                      pl.BlockSpec((1, 128), lambda b, n, s: (0, 0)),
                      pl.BlockSpec((1, 128), lambda b, n, s: (0, 0)),
                      pl.BlockSpec((L, 128), lambda b, n, s: (0, 0)),
                      pl.BlockSpec((L, 128), lambda b, n, s: (0, 0))],
            out_specs=pl.BlockSpec((1, 128, 512), lambda b, n, s: (b, n, 0)),
            scratch_shapes=[pltpu.VMEM((L, 128), BF16)]),
        out_shape=jax.ShapeDtypeStruct((B, nblk * 128, GROUP_W), BF16),
        compiler_params=_cparams(("parallel", "arbitrary")),
    )(sink, u, u, u, jnp.tile(qw, 2)[None], jnp.tile(kw, 2)[None], cos, sin)


TQ_DIFF = 256
KC_DIFF = 768


def _diff_kernel(q_ref, k_ref, v_ref, qw_ref, kw_ref, cos_ref, sin_ref, dl_ref, sw_ref, o_ref, kn_ref, vx_ref,
                 *, j_off, lam_init):
    j = pl.program_id(2) + j_off

    @pl.when(pl.program_id(2) == 0)
    def _():
        kn_ref[...] = _norm_rope(k_ref[0].astype(F32), kw_ref[...], cos_ref[...], sin_ref[...]).astype(BF16)
        vx_ref[:, 0:128] = v_ref[0]
        vx_ref[:, 128:256] = jnp.ones((L, 128), BF16)

    dl = dl_ref[...]
    lam = (jnp.exp(jnp.sum(dl[0:1] * dl[1:2], axis=-1, keepdims=True))
           - jnp.exp(jnp.sum(dl[2:3] * dl[3:4], axis=-1, keepdims=True)) + lam_init)
    r0 = pl.multiple_of(j * TQ_DIFF, TQ_DIFF)
    q = _norm_rope(q_ref[0].astype(F32), qw_ref[...], cos_ref[pl.ds(r0, TQ_DIFF), :], sin_ref[pl.ds(r0, TQ_DIFF), :])
    q = q * (HEAD_DIM ** -0.5)
    lane = lax.broadcasted_iota(jnp.int32, (TQ_DIFF, 128), 1)
    qms = [jnp.where(lane < 64, q, 0.0).astype(BF16), jnp.where(lane < 64, 0.0, q).astype(BF16)]

    def attend(nk):
        os_ = []
        kc = min(KC_DIFF, nk)
        for m in range(2):
            mrun = acc = None
            for c0 in range(0, nk, kc):
                s = lax.dot_general(qms[m], kn_ref[c0:c0 + kc, :], (((1,), (1,)), ((), ())),
                                    preferred_element_type=F32)
                mnew = _row_max(s) if mrun is None else jnp.maximum(mrun, _row_max(s))
                e = jnp.exp((s - mnew).astype(BF16))
                oe = jnp.dot(e, vx_ref[c0:c0 + kc, :], preferred_element_type=F32)
                acc = oe if acc is None else acc * jnp.exp(mrun - mnew) + oe
                mrun = mnew
            os_.append(acc[:, 0:128] / acc[:, 128:129])
        o = os_[0] - lam * os_[1]
        ms = jnp.mean(o * o, axis=-1, keepdims=True)
        o = o * lax.rsqrt(ms + EPS) * sw_ref[...] * (1.0 - lam_init)
        o_ref[0] = o.astype(o_ref.dtype)

    if j_off == 0:
        @pl.when(j == 0)
        def _():
            attend(LC)

    @pl.when(j > 0)
    def _():
        attend(L)


def diff_mixer(u, qw, kw, cos, sin, dlam, subw, lam_init, with_ctx):
    j_off = 0 if with_ctx else LC // TQ_DIFF
    nblk = L // TQ_DIFF - j_off
    qc, kc, vc = U_OFF["diff_q"] // 128, U_OFF["diff_k"] // 128, U_OFF["diff_v"] // 128
    return pl.pallas_call(
        functools.partial(_diff_kernel, j_off=j_off, lam_init=lam_init),
        grid=(B, DIFF_HEADS, nblk),
        in_specs=[pl.BlockSpec((1, TQ_DIFF, 128), lambda b, h, j: (b, j + j_off, qc + h)),
                  pl.BlockSpec((1, L, 128), lambda b, h, j: (b, 0, kc + h)),
                  pl.BlockSpec((1, L, 128), lambda b, h, j: (b, 0, vc + h)),
                  pl.BlockSpec((1, 128), lambda b, h, j: (0, 0)),
                  pl.BlockSpec((1, 128), lambda b, h, j: (0, 0)),
                  pl.BlockSpec((L, 128), lambda b, h, j: (0, 0)),
                  pl.BlockSpec((L, 128), lambda b, h, j: (0, 0)),
                  pl.BlockSpec((4, 64), lambda b, h, j: (0, 0)),
                  pl.BlockSpec((1, 128), lambda b, h, j: (0, 0))],
        out_specs=pl.BlockSpec((1, TQ_DIFF, 128), lambda b, h, j: (b, j, h)),
        out_shape=jax.ShapeDtypeStruct((B, nblk * TQ_DIFF, GROUP_W), BF16),
        scratch_shapes=[pltpu.VMEM((L, 128), BF16), pltpu.VMEM((L, 256), BF16)],
        compiler_params=_cparams(("parallel", "parallel", "arbitrary")),
    )(u, u, u, jnp.tile(qw, 2)[None], jnp.tile(kw, 2)[None], cos, sin, dlam, subw[None])


TM_OUT = 256
TOK_SUB = D // 128


def _store_token_rows(ref, val):
    R = val.shape[0]
    for s in range(TOK_SUB):
        ref[pl.ds(s, R, stride=TOK_SUB), :] = val[:, s * 128:(s + 1) * 128]


def _load_token_rows(ref, R):
    return jnp.concatenate([ref[pl.ds(s, R, stride=TOK_SUB), :] for s in range(TOK_SUB)], axis=-1)


def _outproj_kernel(x_ref, a_ref, b_ref, y_ref, d_ref, w_ref, nw_ref, ml_ref, mc_ref, rwh_ref, rwl_ref,
                    xo_ref, f_ref, lg_ref, *, i_off):
    is_ctx = (pl.program_id(1) + i_off) == 0

    def mod(k):
        return jnp.where(is_ctx, mc_ref[:, k * D:(k + 1) * D], ml_ref[0, :, k * D:(k + 1) * D])

    acc = jnp.dot(a_ref[0], w_ref[0:512, :], preferred_element_type=F32)
    acc += jnp.dot(b_ref[0], w_ref[512:1024, :], preferred_element_type=F32)
    acc += jnp.dot(y_ref[0], w_ref[1024:1536, :], preferred_element_type=F32)
    acc += jnp.dot(d_ref[0], w_ref[1536:2048, :], preferred_element_type=F32)
    xn = x_ref[0] + mod(2) * acc
    xo_ref[0] = xn
    ms = jnp.mean(xn * xn, axis=-1, keepdims=True)
    f = (xn * lax.rsqrt(ms + EPS) * nw_ref[...]) * (1.0 + mod(4)) + mod(3)
    _store_token_rows(f_ref.at[0], f)
    fh = f.astype(BF16)
    fl = (f - fh.astype(F32)).astype(BF16)
    lg_ref[0] = (jnp.dot(fh, rwh_ref[...], preferred_element_type=F32)
                 + jnp.dot(fl, rwh_ref[...], preferred_element_type=F32)
                 + jnp.dot(fh, rwl_ref[...], preferred_element_type=F32))


def out_proj(xall, mixes, w_out_b, nw, ml, mc, rw_pad, with_ctx):
    i_off = 0 if with_ctx else LC // TM_OUT
    nblk = L // TM_OUT - i_off
    rows = lambda b, i: (b, i + i_off, 0)
    rwh = rw_pad.astype(BF16)
    rwl = (rw_pad - rwh.astype(F32)).astype(BF16)
    return pl.pallas_call(
        functools.partial(_outproj_kernel, i_off=i_off),
        grid=(B, nblk),
        in_specs=[pl.BlockSpec((1, TM_OUT, D), rows)]
                 + [pl.BlockSpec((1, TM_OUT, GROUP_W), lambda b, i: (b, i, 0))] * 4
                 + [pl.BlockSpec((D, D), lambda b, i: (0, 0)),
                    pl.BlockSpec((1, D), lambda b, i: (0, 0)),
                    pl.BlockSpec((1, 1, 6 * D), lambda b, i: (b, 0, 0)),
                    pl.BlockSpec((1, 6 * D), lambda b, i: (0, 0)),
                    pl.BlockSpec((D, 128), lambda b, i: (0, 0)),
                    pl.BlockSpec((D, 128), lambda b, i: (0, 0))],
        out_specs=[pl.BlockSpec((1, TM_OUT, D), lambda b, i: (b, i, 0)),
                   pl.BlockSpec((1, TM_OUT * TOK_SUB, 128), lambda b, i: (b, i, 0)),
                   pl.BlockSpec((1, TM_OUT, 128), lambda b, i: (b, i, 0))],
        out_shape=[jax.ShapeDtypeStruct((B, nblk * TM_OUT, D), F32),
                   jax.ShapeDtypeStruct((B, nblk * TM_OUT * TOK_SUB, 128), F32),
                   jax.ShapeDtypeStruct((B, nblk * TM_OUT, 128), F32)],
        compiler_params=_cparams(("parallel", "parallel")),
    )(xall, *mixes, w_out_b, nw, ml, mc, rwh, rwl)


TM_MOE = 256
R_GATHER = 256


def _token_copy(src_hbm, idx, buf, r, sem):
    return pltpu.make_async_copy(src_hbm.at[pl.ds(idx * TOK_SUB, TOK_SUB)], buf.at[pl.ds(r * TOK_SUB, TOK_SUB)], sem)


def _token_start(src_hbm, idx, buf, r, sem, priority):
    pltpu.async_copy(src_hbm.at[pl.ds(idx * TOK_SUB, TOK_SUB)], buf.at[pl.ds(r * TOK_SUB, TOK_SUB)], sem,
                     priority=priority)


def _gather_kernel(src_ref, f_hbm, o_ref, buf, sem):
    t = pl.program_id(0)

    def issue(step, slot):
        def body(r2, carry):
            for p in range(2):
                r = 2 * r2 + p
                _token_start(f_hbm, src_ref[step * R_GATHER + r], buf.at[slot], r, sem.at[slot], p)
            return carry

        lax.fori_loop(0, R_GATHER // 2, body, 0, unroll=4)

    @pl.when(t == 0)
    def _():
        issue(0, 0)

    @pl.when(t + 1 < pl.num_programs(0))
    def _():
        issue(t + 1, (t + 1) % 2)

    slot = t % 2

    def drain(r, carry):
        _token_copy(f_hbm, 0, buf.at[slot], r, sem.at[slot]).wait()
        return carry

    lax.fori_loop(0, R_GATHER, drain, 0, unroll=8)
    o_ref[...] = _load_token_rows(buf.at[slot], R_GATHER).astype(BF16)


def gather_rows(src, f_tok, nrows):
    return pl.pallas_call(
        _gather_kernel,
        grid_spec=pltpu.PrefetchScalarGridSpec(
            num_scalar_prefetch=1,
            grid=(nrows // R_GATHER,),
            in_specs=[pl.BlockSpec(memory_space=pl.ANY)],
            out_specs=pl.BlockSpec((R_GATHER, D), lambda t, s: (t, 0)),
            scratch_shapes=[pltpu.VMEM((2, R_GATHER * TOK_SUB, 128), F32), pltpu.SemaphoreType.DMA((2,))]),
        out_shape=jax.ShapeDtypeStruct((nrows, D), BF16),
        compiler_params=_cparams(("arbitrary",)),
    )(src, f_tok)


def _moe_kernel(te_ref, tv_ref, tn_ref, x_ref, wg_hbm, wu_hbm, wd_hbm, o_ref, sg, su, sd, wg_b, wu_b, wd_b, sem,
                *, layer):
    t = pl.program_id(0)
    e = te_ref[t]

    def copies(ex):
        return (pltpu.make_async_copy(wg_hbm.at[layer, ex], sg, sem.at[0]),
                pltpu.make_async_copy(wu_hbm.at[layer, ex], su, sem.at[1]),
                pltpu.make_async_copy(wd_hbm.at[layer, ex], sd, sem.at[2]))

    @pl.when(t == 0)
    def _():
        for cp in copies(e):
            cp.start()

    @pl.when((t == 0) | (e != te_ref[jnp.maximum(t - 1, 0)]))
    def _():
        for cp in copies(e):
            cp.wait()
        wg_b[...] = sg[...].astype(BF16)
        wu_b[...] = su[...].astype(BF16)
        wd_b[...] = sd[...].astype(BF16)
        nxt = tn_ref[t]

        @pl.when(nxt >= 0)
        def _():
            for cp in copies(nxt):
                cp.start()

    @pl.when(tv_ref[t] == 0)
    def _():
        o_ref[...] = jnp.zeros_like(o_ref)

    @pl.when(tv_ref[t] > 0)
    def _():
        x = x_ref[...]
        g = jnp.dot(x, wg_b[...], preferred_element_type=F32)
        u = jnp.dot(x, wu_b[...], preferred_element_type=F32)
        h = (g / (1.0 + jnp.exp(-g)) * u).astype(BF16)
        _store_token_rows(o_ref, jnp.dot(h, wd_b[...], preferred_element_type=F32))


def moe_experts(tile_e, tile_v, tile_n, xs, wg, wu, wd, layer):
    nrows = xs.shape[0]
    return pl.pallas_call(
        functools.partial(_moe_kernel, layer=layer),
        grid_spec=pltpu.PrefetchScalarGridSpec(
            num_scalar_prefetch=3,
            grid=(nrows // TM_MOE,),
            in_specs=[pl.BlockSpec((TM_MOE, D), lambda t, te, tv, tn: (t, 0)),
                      pl.BlockSpec(memory_space=pl.ANY),
                      pl.BlockSpec(memory_space=pl.ANY),
                      pl.BlockSpec(memory_space=pl.ANY)],
            out_specs=pl.BlockSpec((TM_MOE * TOK_SUB, 128), lambda t, te, tv, tn: (t, 0)),
            scratch_shapes=[pltpu.VMEM((D, D_EXPERT), F32), pltpu.VMEM((D, D_EXPERT), F32),
                            pltpu.VMEM((D_EXPERT, D), F32),
                            pltpu.VMEM((D, D_EXPERT), BF16), pltpu.VMEM((D, D_EXPERT), BF16),
                            pltpu.VMEM((D_EXPERT, D), BF16),
                            pltpu.SemaphoreType.DMA((3,))]),
        out_shape=jax.ShapeDtypeStruct((nrows * TOK_SUB, 128), F32),
        compiler_params=_cparams(("arbitrary",)),
    )(tile_e, tile_v, tile_n, xs, wg, wu, wd)


R_COMB = 256


def _combine_kernel(pos_ref, x_ref, w_ref, ml_ref, mc_ref, y_hbm, o_ref, buf, sem, *, i_off):
    nblk = pl.num_programs(1)
    step = pl.program_id(0) * nblk + pl.program_id(1)
    is_ctx = (pl.program_id(1) + i_off) == 0

    def issue(st, slot):
        base = st * (R_COMB * TOP_K)

        def body(r, carry):
            for k in range(TOP_K):
                _token_start(y_hbm, pos_ref[base + TOP_K * r + k], buf.at[slot, k], r, sem.at[slot, k], k)
            return carry

        lax.fori_loop(0, R_COMB, body, 0, unroll=4)

    @pl.when(step == 0)
    def _():
        issue(0, 0)

    @pl.when(step + 1 < pl.num_programs(0) * nblk)
    def _():
        issue(step + 1, (step + 1) % 2)

    slot = step % 2

    def drain(r, carry):
        for k in range(TOP_K):
            _token_copy(y_hbm, 0, buf.at[slot, k], r, sem.at[slot, k]).wait()
        return carry

    lax.fori_loop(0, R_COMB, drain, 0, unroll=4)
    gf = jnp.where(is_ctx, mc_ref[:, 5 * D:6 * D], ml_ref[0, :, 5 * D:6 * D])
    w0 = w_ref[0, :, 0:1]
    w1 = w_ref[0, :, 1:2]
    for s in range(TOK_SUB):
        cs = slice(s * 128, (s + 1) * 128)
        y = (w0 * buf[slot, 0, pl.ds(s, R_COMB, stride=TOK_SUB), :]
             + w1 * buf[slot, 1, pl.ds(s, R_COMB, stride=TOK_SUB), :])
        o_ref[0, :, cs] = x_ref[0, :, cs] + gf[:, cs] * y


def moe_combine(pos, wts, xall, ml, mc, y_sorted, with_ctx):
    i_off = 0 if with_ctx else LC // R_COMB
    rows_out = xall.shape[1]
    nblk = rows_out // R_COMB
    return pl.pallas_call(
        functools.partial(_combine_kernel, i_off=i_off),
        grid_spec=pltpu.PrefetchScalarGridSpec(
            num_scalar_prefetch=1,
            grid=(B, nblk),
            in_specs=[pl.BlockSpec((1, R_COMB, D), lambda b, i, p: (b, i, 0)),
                      pl.BlockSpec((1, R_COMB, TOP_K), lambda b, i, p: (b, i, 0)),
                      pl.BlockSpec((1, 1, 6 * D), lambda b, i, p: (b, 0, 0)),
                      pl.BlockSpec((1, 6 * D), lambda b, i, p: (0, 0)),
                      pl.BlockSpec(memory_space=pl.ANY)],
            out_specs=pl.BlockSpec((1, R_COMB, D), lambda b, i, p: (b, i, 0)),
            scratch_shapes=[pltpu.VMEM((2, TOP_K, R_COMB * TOK_SUB, 128), F32),
                            pltpu.SemaphoreType.DMA((2, TOP_K))]),
        out_shape=jax.ShapeDtypeStruct((B, rows_out, D), F32),
        compiler_params=_cparams(("arbitrary", "arbitrary")),
    )(pos, xall, wts.reshape(B, rows_out, TOP_K), ml, mc, y_sorted)


def route(logits, router_bias):
    T = logits.shape[0]
    probs = jax.nn.softmax(logits, axis=-1)
    sel = (probs + router_bias.astype(F32)).reshape(T, N_GROUPS, EPG)
    group = jnp.argmax(jnp.max(sel, axis=-1), axis=-1)
    sel_g = jnp.take_along_axis(sel, group[:, None, None], axis=1)[:, 0]
    _, idx = lax.top_k(sel_g, TOP_K)
    expert = (group[:, None] * EPG + idx).astype(jnp.int32)
    wts = jnp.take_along_axis(probs, expert, axis=-1)
    wts = wts / jnp.sum(wts, axis=-1, keepdims=True)
    flat_e = expert.reshape(-1)
    onehot = (flat_e[:, None] == jnp.arange(N_EXPERTS, dtype=jnp.int32)[None, :]).astype(jnp.int32)
    csum = jnp.cumsum(onehot, axis=0)
    rank = jnp.sum(csum * onehot, axis=-1) - 1
    counts = csum[-1]
    ptiles = (counts + TM_MOE - 1) // TM_MOE
    tend = jnp.cumsum(ptiles)
    tstart = tend - ptiles
    dest = (tstart[flat_e] * TM_MOE + rank).astype(jnp.int32)
    ntiles = -(-(TOP_K * T) // TM_MOE) + N_EXPERTS
    nrows = ntiles * TM_MOE
    tid = jnp.arange(ntiles, dtype=jnp.int32)
    tile_v = (tid < tend[-1]).astype(jnp.int32)
    tile_e = jnp.searchsorted(tend, jnp.minimum(tid, tend[-1] - 1), side="right").astype(jnp.int32)
    eid = jnp.arange(N_EXPERTS, dtype=jnp.int32)
    later = (eid[None, :] > eid[:, None]) & (ptiles[None, :] > 0)
    nxt = jnp.min(jnp.where(later, eid[None, :], N_EXPERTS), axis=1)
    tile_n = jnp.where(nxt < N_EXPERTS, nxt, -1)[tile_e].astype(jnp.int32)
    src = jnp.zeros((nrows,), jnp.int32).at[dest].set(jnp.repeat(jnp.arange(T, dtype=jnp.int32), TOP_K))
    return src, wts, dest, tile_e, tile_v, tile_n


GLA_QW = GLA_HEADS * GLA_DK
N_BLK = L // GLA_CHUNK
N_BLK_C = LC // GLA_CHUNK


def _log_sigmoid(z):
    return jnp.minimum(z, 0.0) - jnp.log1p(jnp.exp(-jnp.abs(z)))


def _head_rms(x, w):
    R, W = x.shape
    lane = lax.broadcasted_iota(jnp.int32, (R, 128), 1)
    left = lane < 64
    outs = []
    for c in range(W // 128):
        xc = x[:, c * 128:(c + 1) * 128]
        sq = xc * xc
        sl = jnp.sum(jnp.where(left, sq, 0.0), axis=-1, keepdims=True)
        sr = jnp.sum(jnp.where(left, 0.0, sq), axis=-1, keepdims=True)
        inv = jnp.where(left, lax.rsqrt(sl * (1.0 / 64) + EPS), lax.rsqrt(sr * (1.0 / 64) + EPS))
        outs.append(xc * inv * w)
    return jnp.concatenate(outs, axis=-1)


def _gla_kernel(q_ref, k_ref, v_ref, g_ref, a_ref, wf_ref, wb_ref, gb_ref, nw_ref, ind_ref, bdm_ref, o_ref,
                lf_ref, lb_ref, acc_ref, sf_ref, sb_ref, *, row_off):
    hi = lax.Precision.HIGHEST
    C = GLA_CHUNK
    a = a_ref[0].astype(F32)
    rin = lax.broadcasted_iota(jnp.int32, (L, 1), 0) & (C - 1)

    def block_cumsum(x, forward):
        for sh in (1, 2, 4, 8):
            if forward:
                x = x + jnp.where(rin >= sh, pltpu.roll(x, sh, 0), 0.0)
            else:
                x = x + jnp.where(rin < C - sh, pltpu.roll(x, L - sh, 0), 0.0)
        return x

    lf_ref[...] = block_cumsum(_log_sigmoid(jnp.dot(a, wf_ref[...], preferred_element_type=F32, precision=hi)
                                            + gb_ref[0:1, :]) * (1.0 / GLA_TEMP), True)
    lb_ref[...] = block_cumsum(_log_sigmoid(jnp.dot(a, wb_ref[...], preferred_element_type=F32, precision=hi)
                                            + gb_ref[1:2, :]) * (1.0 / GLA_TEMP), False)
    acc_ref[...] = jnp.zeros_like(acc_ref)
    sf_ref[...] = jnp.zeros_like(sf_ref)
    sb_ref[...] = jnp.zeros_like(sb_ref)

    rowi = lax.broadcasted_iota(jnp.int32, (C, GLA_QW), 0)

    def block(r0, b_ref, s_ref, forward):
        q = q_ref[0, pl.ds(r0, C), :].astype(F32) * (GLA_DK ** -0.5)
        k = k_ref[0, pl.ds(r0, C), :].astype(F32)
        v = v_ref[0, pl.ds(r0, C), :].astype(F32)
        b = b_ref[pl.ds(r0, C), :]
        btot = b[C - 1:C, :] if forward else b[0:1, :]
        st = s_ref[...]
        o = lax.dot_general((q * jnp.exp(b)).astype(BF16), st.astype(BF16), (((1,), (1,)), ((), ())),
                            preferred_element_type=F32)
        ps = []
        for s in range(C):
            seen = (rowi >= s) if forward else (rowi <= s)
            e = jnp.exp(jnp.where(seen, b - b[s:s + 1, :], -jnp.inf))
            ps.append(q * k[s:s + 1, :] * e)
        pm = jnp.concatenate(ps, axis=0).astype(BF16)
        rm = jnp.dot(pm, ind_ref[...], preferred_element_type=F32)
        for s in range(C):
            o = o + rm[s * C:(s + 1) * C, :] * v[s:s + 1, :]
        acc_ref[pl.ds(r0, C), :] += o
        kd = (k * jnp.exp(btot - b)).astype(BF16)
        upd = lax.dot_general(v.astype(BF16), kd, (((0,), (0,)), ((), ())), preferred_element_type=F32)
        s_ref[...] = st * jnp.exp(btot) + upd * bdm_ref[...]

    def body(i, carry):
        block(pl.multiple_of(i * C, C), lf_ref, sf_ref, True)
        jb = jnp.where(i < N_BLK_C, N_BLK_C - 1 - i, N_BLK + N_BLK_C - 1 - i)
        block(pl.multiple_of(jb * C, C), lb_ref, sb_ref, False)
        return carry

    lax.fori_loop(0, N_BLK, body, 0)
    o = _head_rms(acc_ref[row_off:, :], nw_ref[...])
    g = g_ref[0, row_off:, :].astype(F32)
    o_ref[0] = (o * (g / (1.0 + jnp.exp(-g)))).astype(o_ref.dtype)


def gla_mixer(u, gate_w, gate_b, norm_w, with_ctx):
    row_off = 0 if with_ctx else LC
    qc, kc = U_OFF["gla_q"] // 256, U_OFF["gla_k"] // 256
    vc, gc, ac = U_OFF["gla_v"] // 512, U_OFF["gla_g"] // 512, U_OFF["gla_a"] // 128
    wf = jnp.zeros((128, GLA_QW), F32).at[0:GLA_RANK].set(gate_w[0])
    wb = jnp.zeros((128, GLA_QW), F32).at[GLA_RANK:2 * GLA_RANK].set(gate_w[1])
    hd = np.arange(GLA_QW)[:, None] // GLA_DK == np.arange(GROUP_W)[None, :] // GLA_DV
    ind = jnp.asarray(hd, BF16)
    bdm = jnp.asarray(hd.T, F32)
    full = lambda *shape: pl.BlockSpec(shape, lambda b: (0,) * len(shape))
    return pl.pallas_call(
        functools.partial(_gla_kernel, row_off=row_off),
        grid=(B,),
        in_specs=[pl.BlockSpec((1, L, 256), lambda b: (b, 0, qc)),
                  pl.BlockSpec((1, L, 256), lambda b: (b, 0, kc)),
                  pl.BlockSpec((1, L, 512), lambda b: (b, 0, vc)),
                  pl.BlockSpec((1, L, 512), lambda b: (b, 0, gc)),
                  pl.BlockSpec((1, L, 128), lambda b: (b, 0, ac)),
                  full(128, GLA_QW), full(128, GLA_QW), full(2, GLA_QW), full(1, 128),
                  full(GLA_QW, GROUP_W), full(GROUP_W, GLA_QW)],
        out_specs=pl.BlockSpec((1, L - row_off, GROUP_W), lambda b: (b, 0, 0)),
        out_shape=jax.ShapeDtypeStruct((B, L - row_off, GROUP_W), BF16),
        scratch_shapes=[pltpu.VMEM((L, GLA_QW), F32), pltpu.VMEM((L, GLA_QW), F32), pltpu.VMEM((L, GROUP_W), F32),
                        pltpu.VMEM((GROUP_W, GLA_QW), F32), pltpu.VMEM((GROUP_W, GLA_QW), F32)],
        compiler_params=_cparams(("parallel",)),
    )(u, u, u, u, u, wf, wb, gate_b, jnp.tile(norm_w, 2)[None], ind, bdm)


HY_CT = 256
HY_TK = 512


@functools.lru_cache(maxsize=None)
def _dft_consts(Lh):
    k = np.arange(Lh, dtype=np.int64)
    ph = (np.outer(k, k) % (2 * Lh)).astype(np.float64) * (np.pi / Lh)
    sgn = (1.0 - 2.0 * (k % 2)).astype(np.float32)[:, None]
    return np.cos(ph).astype(np.float32), np.sin(ph).astype(np.float32), sgn


def _hyena_features(Lh):
    t = jnp.linspace(0.0, 1.0, Lh, dtype=F32)[:, None]
    w = (2.0 * math.pi / Lh) * jnp.arange(Lh, dtype=F32)[:, None]
    bands = jnp.linspace(1e-4, HY_BANDS - 1, HY_BANDS, dtype=F32)
    z = jnp.concatenate([t, jnp.cos(w * bands), -jnp.sin(w * bands)], axis=-1)
    return jnp.pad(z, ((0, 0), (0, 128 - z.shape[1])))


def _hy_filter_kernel(z_ref, w1_ref, b1_ref, w2_ref, b2_ref, fr_ref, w3f_ref, w3b_ref, dl_ref, c_ref, s_ref, sgn_ref,
                      hc_ref, hs_ref, hn_ref, *, Lh):
    hi = lax.Precision.HIGHEST
    z = z_ref[...]
    h = jnp.sin(fr_ref[0:1, :] * (jnp.dot(z, w1_ref[...], preferred_element_type=F32, precision=hi) + b1_ref[...]))
    h = jnp.sin(fr_ref[1:2, :] * (jnp.dot(h, w2_ref[...], preferred_element_type=F32, precision=hi) + b2_ref[...]))
    dec = jnp.exp(-(z[:, 0:1] * dl_ref[...]))
    hf = jnp.dot(h, w3f_ref[...], preferred_element_type=F32, precision=hi) * dec
    hb = jnp.dot(h, w3b_ref[...], preferred_element_type=F32, precision=hi) * dec
    nrm = jnp.sum(jnp.abs(hf), axis=0, keepdims=True) + jnp.sum(jnp.abs(hb), axis=0, keepdims=True)
    row = lax.broadcasted_iota(jnp.int32, (Lh, 1), 0)
    hf = hf / nrm
    hb = jnp.where(row == 0, 0.0, hb / nrm)
    wk = jnp.where(row == 0, 0.5 / Lh, 1.0 / Lh)

    def project(m_ref, x):
        xh = x.astype(BF16)
        xl = (x - xh.astype(F32)).astype(BF16)
        return (jnp.dot(m_ref[...], xh, preferred_element_type=F32)
                + jnp.dot(m_ref[...], xl, preferred_element_type=F32))

    am = hf + hb
    hc_ref[0] = project(c_ref, am) * wk
    hs_ref[0] = project(s_ref, hf - hb) * wk
    hn_ref[0] = jnp.sum(am * sgn_ref[...], axis=0, keepdims=True) * (0.5 / Lh)


def hyena_spectrum(Lh, w1, b1, w2, b2, w3, fr, cmat, smat, sgn):
    z = _hyena_features(Lh)
    w1p = jnp.pad(w1, ((0, 128 - w1.shape[0]), (0, 0)))
    deltas = jnp.linspace(HY_MIN_DECAY, HY_MAX_DECAY, HY_CH, dtype=F32)[None]
    nct = HY_CH // HY_CT
    full = lambda *shape: pl.BlockSpec(shape, lambda o, c: (0,) * len(shape))
    return pl.pallas_call(
        functools.partial(_hy_filter_kernel, Lh=Lh),
        grid=(HY_ORDER, nct),
        in_specs=[full(Lh, 128), full(128, 64), full(1, 64), full(64, 64), full(1, 64), full(2, 64),
                  pl.BlockSpec((64, HY_CT), lambda o, c: (0, o * 2 * nct + c)),
                  pl.BlockSpec((64, HY_CT), lambda o, c: (0, o * 2 * nct + nct + c)),
                  pl.BlockSpec((1, HY_CT), lambda o, c: (0, c)),
                  full(Lh, Lh), full(Lh, Lh), full(Lh, 1)],
        out_specs=[pl.BlockSpec((1, Lh, HY_CT), lambda o, c: (o, 0, c)),
                   pl.BlockSpec((1, Lh, HY_CT), lambda o, c: (o, 0, c)),
                   pl.BlockSpec((1, 1, HY_CT), lambda o, c: (o, 0, c))],
        out_shape=[jax.ShapeDtypeStruct((HY_ORDER, Lh, HY_CH), F32),
                   jax.ShapeDtypeStruct((HY_ORDER, Lh, HY_CH), F32),
                   jax.ShapeDtypeStruct((HY_ORDER, 1, HY_CH), F32)],
        compiler_params=_cparams(("arbitrary", "arbitrary")),
    )(z, w1p, b1[None], w2, b2[None], fr, w3, w3, deltas, cmat, smat, sgn)


def _hyena_seq(row0, Lh, x1_ref, x2_ref, y_ref, cw_refs, cb_refs, bias_ref, c_ref, s_ref, hc_ref, hs_ref, hn_ref, *,
               scratch):
    r = lax.broadcasted_iota(jnp.int32, (Lh, 1), 0)

    def sconv(u_ref, w_ref, b_ref):
        u = u_ref[0, row0:row0 + Lh, :].astype(F32)
        up = jnp.where(r == 0, 0.0, pltpu.roll(u, 1, 0))
        dn = jnp.where(r == Lh - 1, 0.0, pltpu.roll(u, Lh - 1, 0))
        return up * w_ref[0:1, :] + u * w_ref[1:2, :] + dn * w_ref[2:3, :] + b_ref[...]

    gate_refs = (x1_ref, x2_ref)
    ys_ref, yb_ref, cv_ref = scratch
    rows = slice(0, Lh)
    ys_ref[rows, :] = sconv(y_ref, cw_refs[2], cb_refs[2])
    sgn = (1 - 2 * (r & 1)).astype(F32)
    tk = min(HY_TK, Lh)
    for o in range(HY_ORDER):
        y = ys_ref[rows, :]
        yb_ref[rows, :] = y.astype(BF16)
        cv_ref[rows, :] = sgn * (jnp.sum(y * sgn, axis=0, keepdims=True) * hn_ref[o])
        for m in range(Lh // tk):
            fs = slice(m * tk, (m + 1) * tk)
            yc = jnp.dot(c_ref[fs, :], yb_ref[rows, :], preferred_element_type=F32)
            ysn = jnp.dot(s_ref[fs, :], yb_ref[rows, :], preferred_element_type=F32)
            hc = hc_ref[o, fs, :]
            hs = hs_ref[o, fs, :]
            pc = (yc * hc - ysn * hs).astype(BF16)
            ps = (yc * hs + ysn * hc).astype(BF16)
            cv_ref[rows, :] += (jnp.dot(c_ref[:, fs], pc, preferred_element_type=F32)
                                + jnp.dot(s_ref[:, fs], ps, preferred_element_type=F32))
        ys_ref[rows, :] = (sconv(gate_refs[o], cw_refs[o], cb_refs[o])
                           * (cv_ref[rows, :] + ys_ref[rows, :] * bias_ref[o:o + 1, :]))
    return ys_ref[rows, :]


def _hyena_kernel(x1_ref, x2_ref, y_ref, w1_ref, w2_ref, w3_ref, b1_ref, b2_ref, b3_ref, bias_ref, *rest, with_ctx):
    nmat = 10 if with_ctx else 5
    o_ref = rest[nmat]
    scratch = rest[nmat + 1:]
    seqs = [(LC, S, rest[0:5])]
    if with_ctx:
        seqs.append((0, LC, rest[5:10]))
    for row0, Lh, mats in seqs:
        y = _hyena_seq(row0, Lh, x1_ref, x2_ref, y_ref, (w1_ref, w2_ref, w3_ref), (b1_ref, b2_ref, b3_ref),
                       bias_ref, *mats, scratch=scratch)
        out0 = row0 if with_ctx else 0
        o_ref[0, out0:out0 + Lh, :] = y.astype(o_ref.dtype)


def hyena_mixer(u, conv_w, conv_b, w1, b1, w2, b2, w3, fr, bias, with_ctx):
    nct = HY_CH // HY_CT
    hc0 = U_OFF["hy"] // HY_CT
    once = pl.Buffered(1)
    consts = []
    specs = []
    for Lh in ((S, LC) if with_ctx else (S,)):
        cm, sm, sgn = _dft_consts(Lh)
        cm, sm, sgn = jnp.asarray(cm, BF16), jnp.asarray(sm, BF16), jnp.asarray(sgn)
        hc, hs, hn = hyena_spectrum(Lh, w1, b1, w2, b2, w3, fr, cm, sm, sgn)
        consts += [cm, sm, hc, hs, hn]
        specs += [pl.BlockSpec((Lh, Lh), lambda c, b: (0, 0), pipeline_mode=once),
                  pl.BlockSpec((Lh, Lh), lambda c, b: (0, 0), pipeline_mode=once),
                  pl.BlockSpec((HY_ORDER, Lh, HY_CT), lambda c, b: (0, 0, c), pipeline_mode=once),
                  pl.BlockSpec((HY_ORDER, Lh, HY_CT), lambda c, b: (0, 0, c), pipeline_mode=once),
                  pl.BlockSpec((HY_ORDER, 1, HY_CT), lambda c, b: (0, 0, c), pipeline_mode=once)]
    rows_out = L if with_ctx else S
    ublk = lambda j: pl.BlockSpec((1, L, HY_CT), lambda c, b: (b, 0, hc0 + j * nct + c))
    wblk = lambda j: pl.BlockSpec((3, HY_CT), lambda c, b: (0, j * nct + c))
    bblk = lambda j: pl.BlockSpec((1, HY_CT), lambda c, b: (0, j * nct + c))
    return pl.pallas_call(
        functools.partial(_hyena_kernel, with_ctx=with_ctx),
        grid=(nct, B),
        in_specs=[ublk(0), ublk(1), ublk(2), wblk(0), wblk(1), wblk(2), bblk(0), bblk(1), bblk(2),
                  pl.BlockSpec((HY_ORDER, HY_CT), lambda c, b: (0, c))] + specs,
        out_specs=pl.BlockSpec((1, rows_out, HY_CT), lambda c, b: (b, 0, c)),
        out_shape=jax.ShapeDtypeStruct((B, rows_out, HY_CH), BF16),
        scratch_shapes=[pltpu.VMEM((S, HY_CT), F32), pltpu.VMEM((S, HY_CT), BF16), pltpu.VMEM((S, HY_CT), F32)],
        compiler_params=_cparams(("arbitrary", "arbitrary")),
    )(u, u, u, conv_w, conv_w, conv_w, conv_b[None], conv_b[None], conv_b[None], bias, *consts)


def _permute_w_in(w):
    parts = []
    for n in U_ORDER:
        o, wd = REF_COLS[n]
        p = w[:, o:o + wd]
        if wd < 128:
            p = jnp.pad(p, ((0, 0), (0, 128 - wd)))
        parts.append(p)
    parts.append(jnp.zeros((w.shape[0], U_W - U_USED), w.dtype))
    return jnp.concatenate(parts, axis=1).astype(BF16)


def kernel(x, c, ctx, c_ctx, norm1_w, norm2_w, ada_w, ada_b, w_in, w_out, gla_gate_w, gla_gate_b, gla_norm_w,
           swa_q_norm_w, swa_k_norm_w, swa_sink, hyena_conv_w, hyena_conv_b, hyena_ffn_w1, hyena_ffn_b1,
           hyena_ffn_w2, hyena_ffn_b2, hyena_ffn_w3, hyena_ffn_freq, hyena_bias, diff_q_norm_w, diff_k_norm_w,
           diff_lambda, diff_subln_w, router_w, router_bias, expert_w_gate, expert_w_up, expert_w_down):
    assert x.shape == (B, S, D) and ctx.shape == (B, LC, D)
    cc = jnp.zeros((16, D), F32).at[:B].set(c).at[B].set(c_ctx)
    mods = ada_mod(cc, ada_w, ada_b)
    cos, sin = rope_tables128()
    rw_pad = jnp.pad(router_w, ((0, 0), (0, 128 - N_EXPERTS)))
    xall = jnp.concatenate([ctx, x], axis=1)

    for l in range(DEPTH):
        with_ctx = l < DEPTH - 1
        lam_init = 0.8 - 0.6 * math.exp(-0.3 * l)
        ml = mods[l, :B].reshape(B, 1, 6 * D)
        mc = mods[l, B:B + 1]
        u = in_proj(xall, norm1_w[l][None], ml, mc, _permute_w_in(w_in[l]))
        mix_a = gla_mixer(u, gla_gate_w[l], gla_gate_b[l], gla_norm_w[l], with_ctx)
        mix_b = swa_mixer(u, swa_sink[l], swa_q_norm_w[l], swa_k_norm_w[l], cos, sin, with_ctx)
        mix_y = hyena_mixer(u, hyena_conv_w[l], hyena_conv_b[l], hyena_ffn_w1[l], hyena_ffn_b1[l],
                            hyena_ffn_w2[l], hyena_ffn_b2[l], hyena_ffn_w3[l], hyena_ffn_freq[l],
                            hyena_bias[l], with_ctx)
        mix_d = diff_mixer(u, diff_q_norm_w[l], diff_k_norm_w[l], cos, sin, diff_lambda[l], diff_subln_w[l],
                           lam_init, with_ctx)
        xall, f, logits = out_proj(xall, (mix_a, mix_b, mix_y, mix_d), w_out[l].astype(BF16), norm2_w[l][None],
                                   ml, mc, rw_pad, with_ctx)
        T = B * xall.shape[1]
        src, wts, dest, tile_e, tile_v, tile_n = route(logits.reshape(T, 128)[:, :N_EXPERTS], router_bias)
        xs = gather_rows(src, f.reshape(T * TOK_SUB, 128), src.shape[0])
        ys = moe_experts(tile_e, tile_v, tile_n, xs, expert_w_gate, expert_w_up, expert_w_down, l)
        xall = moe_combine(dest, wts, xall, ml, mc, ys, with_ctx)
    return xall
```

```python
import functools
import math

import numpy as np
import jax
import jax.numpy as jnp
from jax import lax
from jax.experimental import pallas as pl
from jax.experimental.pallas import tpu as pltpu

F32 = jnp.float32
BF16 = jnp.bfloat16

D = 2048
B = 8
S = 2048
LC = 256
L = LC + S
DEPTH = 2
GRID_W = 64
HEAD_DIM = 64
ROPE_THETA = 10000.0
EPS = 1e-6
GROUP_W = 512

GLA_DV = 64
GLA_DK = 32
GLA_HEADS = 8
GLA_RANK = 16
GLA_TEMP = 16.0
GLA_CHUNK = 16
SWA_HEADS = 8
SWA_KV = 2
SWA_WINDOW = 128
HY_CH = 512
HY_ORDER = 2
HY_BANDS = 16
HY_TARGET = 1e-2
HY_MIN_DECAY = math.log(1.0 / HY_TARGET) / 1.5
HY_MAX_DECAY = math.log(1.0 / HY_TARGET) / 0.3
DIFF_HEADS = 4
N_EXPERTS = 16
N_GROUPS = 4
EPG = 4
TOP_K = 2
D_EXPERT = 1024

REF_COLS = dict(gla_q=(0, 256), gla_k=(256, 256), gla_v=(512, 512), gla_g=(1024, 512), gla_a=(1536, 32),
                swa_q=(1568, 512), swa_k=(2080, 128), swa_v=(2208, 128), hy=(2336, 1536),
                diff_q=(3872, 512), diff_k=(4384, 512), diff_v=(4896, 512))
U_ORDER = ("hy", "gla_v", "gla_g", "swa_q", "diff_q", "diff_k", "diff_v", "gla_q", "gla_k", "swa_k", "swa_v", "gla_a")
U_OFF = {}
_o = 0
for _n in U_ORDER:
    U_OFF[_n] = _o
    _o += max(REF_COLS[_n][1], 128)
U_USED = _o
TN_IN = 512
U_W = -(-U_USED // TN_IN) * TN_IN

VMEM_LIMIT = 56 * 1024 * 1024


def _cparams(sem):
    return pltpu.CompilerParams(dimension_semantics=sem, vmem_limit_bytes=VMEM_LIMIT)


def _ada_kernel(c_ref, w_ref, b_ref, o_ref):
    c = c_ref[...]
    a = c / (1.0 + jnp.exp(-c))
    o_ref[0] = jnp.dot(a, w_ref[0], preferred_element_type=F32, precision=lax.Precision.HIGHEST) + b_ref[0]


def ada_mod(cc, ada_w, ada_b):
    tn = 1024
    return pl.pallas_call(
        _ada_kernel,
        grid=(DEPTH, 6 * D // tn),
        in_specs=[pl.BlockSpec((16, D), lambda l, j: (0, 0)),
                  pl.BlockSpec((1, D, tn), lambda l, j: (l, 0, j)),
                  pl.BlockSpec((1, 1, tn), lambda l, j: (l, 0, j))],
        out_specs=pl.BlockSpec((1, 16, tn), lambda l, j: (l, 0, j)),
        out_shape=jax.ShapeDtypeStruct((DEPTH, 16, 6 * D), F32),
        compiler_params=_cparams(("parallel", "parallel")),
    )(cc, ada_w, ada_b.reshape(DEPTH, 1, 6 * D))


TM_IN = 1152
CH_IN = 32


def _inproj_kernel(x_ref, nw_ref, scl_ref, shl_ref, scc_ref, shc_ref, w_ref, o_ref, h_ref):
    i = pl.program_id(1)

    @pl.when(pl.program_id(2) == 0)
    def _():
        def chunk(c, carry):
            r0 = pl.multiple_of(c * CH_IN, CH_IN)
            x = x_ref[0, pl.ds(r0, CH_IN), :]
            ms = jnp.mean(x * x, axis=-1, keepdims=True)
            y = x * lax.rsqrt(ms + EPS) * nw_ref[...]
            row = i * TM_IN + r0 + lax.broadcasted_iota(jnp.int32, (CH_IN, 1), 0)
            is_ctx = row < LC
            sc = jnp.where(is_ctx, scc_ref[...], scl_ref[0])
            sh = jnp.where(is_ctx, shc_ref[...], shl_ref[0])
            h_ref[pl.ds(r0, CH_IN), :] = (y * (1.0 + sc) + sh).astype(BF16)
            return carry

        lax.fori_loop(0, TM_IN // CH_IN, chunk, 0)

    o_ref[0] = jnp.dot(h_ref[...], w_ref[...], preferred_element_type=F32).astype(o_ref.dtype)


def in_proj(xall, nw, ml, mc, w_in_p):
    return pl.pallas_call(
        _inproj_kernel,
        grid=(B, L // TM_IN, U_W // TN_IN),
        in_specs=[pl.BlockSpec((1, TM_IN, D), lambda b, i, j: (b, i, 0)),
                  pl.BlockSpec((1, D), lambda b, i, j: (0, 0)),
                  pl.BlockSpec((1, 1, D), lambda b, i, j: (b, 0, 1)),
                  pl.BlockSpec((1, 1, D), lambda b, i, j: (b, 0, 0)),
                  pl.BlockSpec((1, D), lambda b, i, j: (0, 1)),
                  pl.BlockSpec((1, D), lambda b, i, j: (0, 0)),
                  pl.BlockSpec((D, TN_IN), lambda b, i, j: (0, j))],
        out_specs=pl.BlockSpec((1, TM_IN, TN_IN), lambda b, i, j: (b, i, j)),
        out_shape=jax.ShapeDtypeStruct((B, L, U_W), BF16),
        scratch_shapes=[pltpu.VMEM((TM_IN, D), BF16)],
        compiler_params=_cparams(("parallel", "parallel", "arbitrary")),
    )(xall, nw, ml, ml, mc, mc, w_in_p)


def _norm_rope(x, w, cos, sin):
    R, W = x.shape
    lane = lax.broadcasted_iota(jnp.int32, (R, 128), 1)
    left = lane < HEAD_DIM
    first = (lane & 31) < 16
    outs = []
    for c in range(W // 128):
        xc = x[:, c * 128:(c + 1) * 128]
        sq = xc * xc
        sl = jnp.sum(jnp.where(left, sq, 0.0), axis=-1, keepdims=True)
        sr = jnp.sum(jnp.where(left, 0.0, sq), axis=-1, keepdims=True)
        inv = jnp.where(left, lax.rsqrt(sl * (1.0 / HEAD_DIM) + EPS), lax.rsqrt(sr * (1.0 / HEAD_DIM) + EPS))
        y = xc * inv * w
        rot = jnp.where(first, pltpu.roll(y, 112, 1), pltpu.roll(y, 16, 1))
        outs.append(y * cos + rot * sin)
    return outs[0] if len(outs) == 1 else jnp.concatenate(outs, axis=-1)


def rope_tables128():
    n = HEAD_DIM // 4
    rows = S // GRID_W
    row_ids = jnp.repeat(jnp.arange(rows), GRID_W).astype(F32)
    col_ids = jnp.tile(jnp.arange(GRID_W), rows).astype(F32)
    inv = ROPE_THETA ** (-jnp.arange(n, dtype=F32) / n)
    ar = row_ids[:, None] * inv
    ac = col_ids[:, None] * inv
    cos64 = jnp.concatenate([jnp.cos(ar), jnp.cos(ar), jnp.cos(ac), jnp.cos(ac)], axis=-1)
    sin64 = jnp.concatenate([-jnp.sin(ar), jnp.sin(ar), -jnp.sin(ac), jnp.sin(ac)], axis=-1)
    cos = jnp.concatenate([jnp.ones((LC, 64), F32), cos64], axis=0)
    sin = jnp.concatenate([jnp.zeros((LC, 64), F32), sin64], axis=0)
    return jnp.tile(cos, (1, 2)), jnp.tile(sin, (1, 2))


def _row_max(s):
    m = s[:, 0:128]
    for c in range(1, s.shape[1] // 128):
        m = jnp.maximum(m, s[:, c * 128:(c + 1) * 128])
    return jnp.max(m, axis=-1, keepdims=True)


def _row_sum(s):
    a = s[:, 0:128]
    for c in range(1, s.shape[1] // 128):
        a = a + s[:, c * 128:(c + 1) * 128]
    return jnp.sum(a, axis=-1, keepdims=True)


SWA_BAND = 3 * SWA_WINDOW


def _swa_kernel(sink_ref, q_ref, k_ref, v_ref, qw_ref, kw_ref, cos_ref, sin_ref, o_ref, kn_ref, *, n_off):
    n = pl.program_id(1) + n_off

    @pl.when(pl.program_id(1) == 0)
    def _():
        kn_ref[...] = _norm_rope(k_ref[0].astype(F32), kw_ref[...], cos_ref[...], sin_ref[...]).astype(BF16)

    r0 = pl.multiple_of(n * 128, 128)
    q = _norm_rope(q_ref[0].astype(F32), qw_ref[...], cos_ref[pl.ds(r0, 128), :], sin_ref[pl.ds(r0, 128), :])
    q = (q * (HEAD_DIM ** -0.5)).astype(BF16)

    nb = n - LC // 128
    is_lat = nb >= 0
    lstart = jnp.clip((nb - 1) * 128, 0, S - SWA_BAND)
    start = pl.multiple_of(LC + lstart, 128)
    kk = jnp.concatenate([kn_ref[0:LC, :], kn_ref[pl.ds(start, SWA_BAND), :]], axis=0)
    vv = jnp.concatenate([v_ref[0, 0:LC, :], v_ref[0, pl.ds(start, SWA_BAND), :]], axis=0)
    nk = LC + SWA_BAND
    row = lax.broadcasted_iota(jnp.int32, (512, nk), 0) & 127
    col = lax.broadcasted_iota(jnp.int32, (512, nk), 1)
    qpos = nb * 128 + row
    kpos = lstart + col - LC
    valid = (col < LC) | ((jnp.abs(qpos - kpos) <= SWA_WINDOW) & is_lat)
    rowi = lax.broadcasted_iota(jnp.int32, (512, 1), 0)
    outs = [None] * SWA_HEADS
    for hk in range(SWA_KV):
        qs = jnp.concatenate([q[:, (hk * 4 + g) * 64:(hk * 4 + g + 1) * 64] for g in range(4)], axis=0)
        s = lax.dot_general(qs, kk[:, hk * 64:(hk + 1) * 64], (((1,), (1,)), ((), ())),
                            preferred_element_type=F32)
        s = jnp.where(valid, s, -jnp.inf)
        sk = jnp.where(rowi < 128, sink_ref[hk * 4],
                       jnp.where(rowi < 256, sink_ref[hk * 4 + 1],
                                 jnp.where(rowi < 384, sink_ref[hk * 4 + 2], sink_ref[hk * 4 + 3])))
        m = jnp.maximum(_row_max(s), sk)
        e = jnp.exp(s - m)
        den = _row_sum(e) + jnp.exp(sk - m)
        o = jnp.dot(e.astype(BF16), vv[:, hk * 64:(hk + 1) * 64], preferred_element_type=F32) / den
        for g in range(4):
            outs[hk * 4 + g] = o[g * 128:(g + 1) * 128]
    o_ref[0] = jnp.concatenate(outs, axis=-1).astype(o_ref.dtype)


def swa_mixer(u, sink, qw, kw, cos, sin, with_ctx):
    n_off = 0 if with_ctx else LC // 128
    nblk = L // 128 - n_off
    qc, kc, vc = U_OFF["swa_q"] // 512, U_OFF["swa_k"] // 128, U_OFF["swa_v"] // 128
    return pl.pallas_call(
        functools.partial(_swa_kernel, n_off=n_off),
        grid_spec=pltpu.PrefetchScalarGridSpec(
            num_scalar_prefetch=1,
            grid=(B, nblk),
            in_specs=[pl.BlockSpec((1, 128, 512), lambda b, n, s: (b, n + n_off, qc)),
                      pl.BlockSpec((1, L, 128), lambda b, n, s: (b, 0, kc)),
                      pl.BlockSpec((1, L, 128), lambda b, n, s: (b, 0, vc)),
                      pl.BlockSpec((1, 128), lambda b, n, s: (0, 0)),
                      pl.BlockSpec((1, 128), lambda b, n, s: (0, 0)),
                      pl.BlockSpec((L, 128), lambda b, n, s: (0, 0)),
                      pl.BlockSpec((L, 128), lambda b, n, s: (0, 0))],
            out_specs=pl.BlockSpec((1, 128, 512), lambda b, n, s: (b, n, 0)),
            scratch_shapes=[pltpu.VMEM((L, 128), BF16)]),
        out_shape=jax.ShapeDtypeStruct((B, nblk * 128, GROUP_W), BF16),
        compiler_params=_cparams(("parallel", "arbitrary")),
    )(sink, u, u, u, jnp.tile(qw, 2)[None], jnp.tile(kw, 2)[None], cos, sin)


TQ_DIFF = 256
KC_DIFF = 768


def _diff_kernel(q_ref, k_ref, v_ref, qw_ref, kw_ref, cos_ref, sin_ref, dl_ref, sw_ref, o_ref, kn_ref, vx_ref,
                 *, j_off, lam_init):
    j = pl.program_id(2) + j_off

    @pl.when(pl.program_id(2) == 0)
    def _():
        kn_ref[...] = _norm_rope(k_ref[0].astype(F32), kw_ref[...], cos_ref[...], sin_ref[...]).astype(BF16)
        vx_ref[:, 0:128] = v_ref[0]
        vx_ref[:, 128:256] = jnp.ones((L, 128), BF16)

    dl = dl_ref[...]
    lam = (jnp.exp(jnp.sum(dl[0:1] * dl[1:2], axis=-1, keepdims=True))
           - jnp.exp(jnp.sum(dl[2:3] * dl[3:4], axis=-1, keepdims=True)) + lam_init)
    r0 = pl.multiple_of(j * TQ_DIFF, TQ_DIFF)
    q = _norm_rope(q_ref[0].astype(F32), qw_ref[...], cos_ref[pl.ds(r0, TQ_DIFF), :], sin_ref[pl.ds(r0, TQ_DIFF), :])
    q = q * (HEAD_DIM ** -0.5)
    lane = lax.broadcasted_iota(jnp.int32, (TQ_DIFF, 128), 1)
    qms = [jnp.where(lane < 64, q, 0.0).astype(BF16), jnp.where(lane < 64, 0.0, q).astype(BF16)]

    def attend(nk):
        os_ = []
        kc = min(KC_DIFF, nk)
        for m in range(2):
            mrun = acc = None
            for c0 in range(0, nk, kc):
                s = lax.dot_general(qms[m], kn_ref[c0:c0 + kc, :], (((1,), (1,)), ((), ())),
                                    preferred_element_type=F32)
                mnew = _row_max(s) if mrun is None else jnp.maximum(mrun, _row_max(s))
                e = jnp.exp((s - mnew).astype(BF16))
                oe = jnp.dot(e, vx_ref[c0:c0 + kc, :], preferred_element_type=F32)
                acc = oe if acc is None else acc * jnp.exp(mrun - mnew) + oe
                mrun = mnew
            os_.append(acc[:, 0:128] / acc[:, 128:129])
        o = os_[0] - lam * os_[1]
        ms = jnp.mean(o * o, axis=-1, keepdims=True)
        o = o * lax.rsqrt(ms + EPS) * sw_ref[...] * (1.0 - lam_init)
        o_ref[0] = o.astype(o_ref.dtype)

    if j_off == 0:
        @pl.when(j == 0)
        def _():
            attend(LC)

    @pl.when(j > 0)
    def _():
        attend(L)


def diff_mixer(u, qw, kw, cos, sin, dlam, subw, lam_init, with_ctx):
    j_off = 0 if with_ctx else LC // TQ_DIFF
    nblk = L // TQ_DIFF - j_off
    qc, kc, vc = U_OFF["diff_q"] // 128, U_OFF["diff_k"] // 128, U_OFF["diff_v"] // 128
    return pl.pallas_call(
        functools.partial(_diff_kernel, j_off=j_off, lam_init=lam_init),
        grid=(B, DIFF_HEADS, nblk),
        in_specs=[pl.BlockSpec((1, TQ_DIFF, 128), lambda b, h, j: (b, j + j_off, qc + h)),
                  pl.BlockSpec((1, L, 128), lambda b, h, j: (b, 0, kc + h)),
                  pl.BlockSpec((1, L, 128), lambda b, h, j: (b, 0, vc + h)),
                  pl.BlockSpec((1, 128), lambda b, h, j: (0, 0)),
                  pl.BlockSpec((1, 128), lambda b, h, j: (0, 0)),
                  pl.BlockSpec((L, 128), lambda b, h, j: (0, 0)),
                  pl.BlockSpec((L, 128), lambda b, h, j: (0, 0)),
                  pl.BlockSpec((4, 64), lambda b, h, j: (0, 0)),
                  pl.BlockSpec((1, 128), lambda b, h, j: (0, 0))],
        out_specs=pl.BlockSpec((1, TQ_DIFF, 128), lambda b, h, j: (b, j, h)),
        out_shape=jax.ShapeDtypeStruct((B, nblk * TQ_DIFF, GROUP_W), BF16),
        scratch_shapes=[pltpu.VMEM((L, 128), BF16), pltpu.VMEM((L, 256), BF16)],
        compiler_params=_cparams(("parallel", "parallel", "arbitrary")),
    )(u, u, u, jnp.tile(qw, 2)[None], jnp.tile(kw, 2)[None], cos, sin, dlam, subw[None])


TM_OUT = 256
TOK_SUB = D // 128


def _store_token_rows(ref, val):
    R = val.shape[0]
    for s in range(TOK_SUB):
        ref[pl.ds(s, R, stride=TOK_SUB), :] = val[:, s * 128:(s + 1) * 128]


def _load_token_rows(ref, R):
    return jnp.concatenate([ref[pl.ds(s, R, stride=TOK_SUB), :] for s in range(TOK_SUB)], axis=-1)


def _outproj_kernel(x_ref, a_ref, b_ref, y_ref, d_ref, w_ref, nw_ref, ml_ref, mc_ref, rwh_ref, rwl_ref,
                    xo_ref, f_ref, lg_ref, *, i_off):
    is_ctx = (pl.program_id(1) + i_off) == 0

    def mod(k):
        return jnp.where(is_ctx, mc_ref[:, k * D:(k + 1) * D], ml_ref[0, :, k * D:(k + 1) * D])

    acc = jnp.dot(a_ref[0], w_ref[0:512, :], preferred_element_type=F32)
    acc += jnp.dot(b_ref[0], w_ref[512:1024, :], preferred_element_type=F32)
    acc += jnp.dot(y_ref[0], w_ref[1024:1536, :], preferred_element_type=F32)
    acc += jnp.dot(d_ref[0], w_ref[1536:2048, :], preferred_element_type=F32)
    xn = x_ref[0] + mod(2) * acc
    xo_ref[0] = xn
    ms = jnp.mean(xn * xn, axis=-1, keepdims=True)
    f = (xn * lax.rsqrt(ms + EPS) * nw_ref[...]) * (1.0 + mod(4)) + mod(3)
    _store_token_rows(f_ref.at[0], f)
    fh = f.astype(BF16)
    fl = (f - fh.astype(F32)).astype(BF16)
    lg_ref[0] = (jnp.dot(fh, rwh_ref[...], preferred_element_type=F32)
                 + jnp.dot(fl, rwh_ref[...], preferred_element_type=F32)
                 + jnp.dot(fh, rwl_ref[...], preferred_element_type=F32))


def out_proj(xall, mixes, w_out_b, nw, ml, mc, rw_pad, with_ctx):
    i_off = 0 if with_ctx else LC // TM_OUT
    nblk = L // TM_OUT - i_off
    rows = lambda b, i: (b, i + i_off, 0)
    rwh = rw_pad.astype(BF16)
    rwl = (rw_pad - rwh.astype(F32)).astype(BF16)
    return pl.pallas_call(
        functools.partial(_outproj_kernel, i_off=i_off),
        grid=(B, nblk),
        in_specs=[pl.BlockSpec((1, TM_OUT, D), rows)]
                 + [pl.BlockSpec((1, TM_OUT, GROUP_W), lambda b, i: (b, i, 0))] * 4
                 + [pl.BlockSpec((D, D), lambda b, i: (0, 0)),
                    pl.BlockSpec((1, D), lambda b, i: (0, 0)),
                    pl.BlockSpec((1, 1, 6 * D), lambda b, i: (b, 0, 0)),
                    pl.BlockSpec((1, 6 * D), lambda b, i: (0, 0)),
                    pl.BlockSpec((D, 128), lambda b, i: (0, 0)),
                    pl.BlockSpec((D, 128), lambda b, i: (0, 0))],
        out_specs=[pl.BlockSpec((1, TM_OUT, D), lambda b, i: (b, i, 0)),
                   pl.BlockSpec((1, TM_OUT * TOK_SUB, 128), lambda b, i: (b, i, 0)),
                   pl.BlockSpec((1, TM_OUT, 128), lambda b, i: (b, i, 0))],
        out_shape=[jax.ShapeDtypeStruct((B, nblk * TM_OUT, D), F32),
                   jax.ShapeDtypeStruct((B, nblk * TM_OUT * TOK_SUB, 128), F32),
                   jax.ShapeDtypeStruct((B, nblk * TM_OUT, 128), F32)],
        compiler_params=_cparams(("parallel", "parallel")),
    )(xall, *mixes, w_out_b, nw, ml, mc, rwh, rwl)


TM_MOE = 256


def _token_copy(src_hbm, idx, buf, r, sem):
    return pltpu.make_async_copy(src_hbm.at[pl.ds(idx * TOK_SUB, TOK_SUB)], buf.at[pl.ds(r * TOK_SUB, TOK_SUB)], sem)


def _moe_kernel(te_ref, tv_ref, tn_ref, src_ref, f_hbm, wg_hbm, wu_hbm, wd_hbm, o_ref, gbuf, gsem, sg, su, sd,
                wg_b, wu_b, wd_b, sem, *, layer):
    t = pl.program_id(0)
    e = te_ref[t]

    def issue(step, slot):
        def body(r, carry):
            _token_copy(f_hbm, src_ref[step * TM_MOE + r], gbuf.at[slot], r, gsem.at[slot]).start()
            return carry

        lax.fori_loop(0, TM_MOE, body, 0, unroll=8)

    @pl.when(t == 0)
    def _():
        issue(0, 0)

    @pl.when(t + 1 < pl.num_programs(0))
    def _():
        issue(t + 1, (t + 1) % 2)

    def copies(ex):
        return (pltpu.make_async_copy(wg_hbm.at[layer, ex], sg, sem.at[0]),
                pltpu.make_async_copy(wu_hbm.at[layer, ex], su, sem.at[1]),
                pltpu.make_async_copy(wd_hbm.at[layer, ex], sd, sem.at[2]))

    @pl.when(t == 0)
    def _():
        for cp in copies(e):
            cp.start()

    @pl.when((t == 0) | (e != te_ref[jnp.maximum(t - 1, 0)]))
    def _():
        for cp in copies(e):
            cp.wait()
        wg_b[...] = sg[...].astype(BF16)
        wu_b[...] = su[...].astype(BF16)
        wd_b[...] = sd[...].astype(BF16)
        nxt = tn_ref[t]

        @pl.when(nxt >= 0)
        def _():
            for cp in copies(nxt):
                cp.start()

    slot = t % 2

    def drain(r, carry):
        _token_copy(f_hbm, 0, gbuf.at[slot], r, gsem.at[slot]).wait()
        return carry

    lax.fori_loop(0, TM_MOE, drain, 0, unroll=8)

    @pl.when(tv_ref[t] == 0)
    def _():
        o_ref[...] = jnp.zeros_like(o_ref)

    @pl.when(tv_ref[t] > 0)
    def _():
        x = _load_token_rows(gbuf.at[slot], TM_MOE).astype(BF16)
        g = jnp.dot(x, wg_b[...], preferred_element_type=F32)
        u = jnp.dot(x, wu_b[...], preferred_element_type=F32)
        h = (g / (1.0 + jnp.exp(-g)) * u).astype(BF16)
        _store_token_rows(o_ref, jnp.dot(h, wd_b[...], preferred_element_type=F32))


def moe_experts(tile_e, tile_v, tile_n, src, f_tok, wg, wu, wd, layer):
    nrows = src.shape[0]
    return pl.pallas_call(
        functools.partial(_moe_kernel, layer=layer),
        grid_spec=pltpu.PrefetchScalarGridSpec(
            num_scalar_prefetch=4,
            grid=(nrows // TM_MOE,),
            in_specs=[pl.BlockSpec(memory_space=pl.ANY)] * 4,
            out_specs=pl.BlockSpec((TM_MOE * TOK_SUB, 128), lambda t, te, tv, tn, sr: (t, 0)),
            scratch_shapes=[pltpu.VMEM((2, TM_MOE * TOK_SUB, 128), F32), pltpu.SemaphoreType.DMA((2,)),
                            pltpu.VMEM((D, D_EXPERT), F32), pltpu.VMEM((D, D_EXPERT), F32),
                            pltpu.VMEM((D_EXPERT, D), F32),
                            pltpu.VMEM((D, D_EXPERT), BF16), pltpu.VMEM((D, D_EXPERT), BF16),
                            pltpu.VMEM((D_EXPERT, D), BF16),
                            pltpu.SemaphoreType.DMA((3,))]),
        out_shape=jax.ShapeDtypeStruct((nrows * TOK_SUB, 128), F32),
        compiler_params=_cparams(("arbitrary",)),
    )(tile_e, tile_v, tile_n, src, f_tok, wg, wu, wd)


R_COMB = 256


def _combine_kernel(pos_ref, x_ref, w_ref, ml_ref, mc_ref, y_hbm, o_ref, buf, sem, *, i_off):
    nblk = pl.num_programs(1)
    step = pl.program_id(0) * nblk + pl.program_id(1)
    is_ctx = (pl.program_id(1) + i_off) == 0

    def issue(st, slot):
        base = st * (R_COMB * TOP_K)

        def body(r, carry):
            for k in range(TOP_K):
                _token_copy(y_hbm, pos_ref[base + TOP_K * r + k], buf.at[slot, k], r, sem.at[slot, k]).start()
            return carry

        lax.fori_loop(0, R_COMB, body, 0, unroll=4)

    @pl.when(step == 0)
    def _():
        issue(0, 0)

    @pl.when(step + 1 < pl.num_programs(0) * nblk)
    def _():
        issue(step + 1, (step + 1) % 2)

    slot = step % 2

    def drain(r, carry):
        for k in range(TOP_K):
            _token_copy(y_hbm, 0, buf.at[slot, k], r, sem.at[slot, k]).wait()
        return carry

    lax.fori_loop(0, R_COMB, drain, 0, unroll=4)
    gf = jnp.where(is_ctx, mc_ref[:, 5 * D:6 * D], ml_ref[0, :, 5 * D:6 * D])
    w0 = w_ref[0, :, 0:1]
    w1 = w_ref[0, :, 1:2]
    for s in range(TOK_SUB):
        cs = slice(s * 128, (s + 1) * 128)
        y = (w0 * buf[slot, 0, pl.ds(s, R_COMB, stride=TOK_SUB), :]
             + w1 * buf[slot, 1, pl.ds(s, R_COMB, stride=TOK_SUB), :])
        o_ref[0, :, cs] = x_ref[0, :, cs] + gf[:, cs] * y


def moe_combine(pos, wts, xall, ml, mc, y_sorted, with_ctx):
    i_off = 0 if with_ctx else LC // R_COMB
    rows_out = xall.shape[1]
    nblk = rows_out // R_COMB
    return pl.pallas_call(
        functools.partial(_combine_kernel, i_off=i_off),
        grid_spec=pltpu.PrefetchScalarGridSpec(
            num_scalar_prefetch=1,
            grid=(B, nblk),
            in_specs=[pl.BlockSpec((1, R_COMB, D), lambda b, i, p: (b, i, 0)),
                      pl.BlockSpec((1, R_COMB, TOP_K), lambda b, i, p: (b, i, 0)),
                      pl.BlockSpec((1, 1, 6 * D), lambda b, i, p: (b, 0, 0)),
                      pl.BlockSpec((1, 6 * D), lambda b, i, p: (0, 0)),
                      pl.BlockSpec(memory_space=pl.ANY)],
            out_specs=pl.BlockSpec((1, R_COMB, D), lambda b, i, p: (b, i, 0)),
            scratch_shapes=[pltpu.VMEM((2, TOP_K, R_COMB * TOK_SUB, 128), F32),
                            pltpu.SemaphoreType.DMA((2, TOP_K))]),
        out_shape=jax.ShapeDtypeStruct((B, rows_out, D), F32),
        compiler_params=_cparams(("arbitrary", "arbitrary")),
    )(pos, xall, wts.reshape(B, rows_out, TOP_K), ml, mc, y_sorted)


def route(logits, router_bias):
    T = logits.shape[0]
    lt = logits.T
    ex = jnp.exp(lt - jnp.max(lt, axis=0, keepdims=True))
    probs = ex / jnp.sum(ex, axis=0, keepdims=True)
    sel = (probs + router_bias.astype(F32)[:, None]).reshape(N_GROUPS, EPG, T)
    group = jnp.argmax(jnp.max(sel, axis=1), axis=0).astype(jnp.int32)
    sel_g = jnp.take_along_axis(sel, group[None, None, :], axis=0)[0]
    wid = jnp.arange(EPG, dtype=jnp.int32)[:, None]
    i0 = jnp.argmax(sel_g, axis=0).astype(jnp.int32)
    i1 = jnp.argmax(jnp.where(wid == i0[None, :], -jnp.inf, sel_g), axis=0).astype(jnp.int32)
    e0 = group * EPG + i0
    e1 = group * EPG + i1
    p0 = jnp.take_along_axis(probs, e0[None, :], axis=0)[0]
    p1 = jnp.take_along_axis(probs, e1[None, :], axis=0)[0]
    wts = jnp.stack([p0 / (p0 + p1), p1 / (p0 + p1)], axis=1)
    eid = jnp.arange(N_EXPERTS, dtype=jnp.int32)
    hot = (eid[:, None] == e0[None, :]).astype(jnp.int32) + (eid[:, None] == e1[None, :]).astype(jnp.int32)
    csum = jnp.cumsum(hot, axis=1)
    before = csum - hot
    r0 = jnp.take_along_axis(before, e0[None, :], axis=0)[0]
    r1 = jnp.take_along_axis(before, e1[None, :], axis=0)[0]
    counts = csum[:, -1]
    ptiles = (counts + TM_MOE - 1) // TM_MOE
    tend = jnp.cumsum(ptiles)
    tstart = tend - ptiles
    dest = jnp.stack([tstart[e0] * TM_MOE + r0, tstart[e1] * TM_MOE + r1], axis=1).reshape(-1).astype(jnp.int32)
    ntiles = -(-(TOP_K * T) // TM_MOE) + N_EXPERTS
    nrows = ntiles * TM_MOE
    tid = jnp.arange(ntiles, dtype=jnp.int32)
    tile_v = (tid < tend[-1]).astype(jnp.int32)
    tile_e = jnp.searchsorted(tend, jnp.minimum(tid, tend[-1] - 1), side="right").astype(jnp.int32)
    eid = jnp.arange(N_EXPERTS, dtype=jnp.int32)
    later = (eid[None, :] > eid[:, None]) & (ptiles[None, :] > 0)
    nxt = jnp.min(jnp.where(later, eid[None, :], N_EXPERTS), axis=1)
    tile_n = jnp.where(nxt < N_EXPERTS, nxt, -1)[tile_e].astype(jnp.int32)
    src = jnp.zeros((nrows,), jnp.int32).at[dest].set(jnp.repeat(jnp.arange(T, dtype=jnp.int32), TOP_K))
    return src, wts, dest, tile_e, tile_v, tile_n


GLA_QW = GLA_HEADS * GLA_DK
N_BLK = L // GLA_CHUNK
N_BLK_C = LC // GLA_CHUNK


def _log_sigmoid(z):
    return jnp.minimum(z, 0.0) - jnp.log1p(jnp.exp(-jnp.abs(z)))


def _head_rms(x, w):
    R, W = x.shape
    lane = lax.broadcasted_iota(jnp.int32, (R, 128), 1)
    left = lane < 64
    outs = []
    for c in range(W // 128):
        xc = x[:, c * 128:(c + 1) * 128]
        sq = xc * xc
        sl = jnp.sum(jnp.where(left, sq, 0.0), axis=-1, keepdims=True)
        sr = jnp.sum(jnp.where(left, 0.0, sq), axis=-1, keepdims=True)
        inv = jnp.where(left, lax.rsqrt(sl * (1.0 / 64) + EPS), lax.rsqrt(sr * (1.0 / 64) + EPS))
        outs.append(xc * inv * w)
    return jnp.concatenate(outs, axis=-1)


def _gla_kernel(q_ref, k_ref, v_ref, g_ref, a_ref, wf_ref, wb_ref, gb_ref, nw_ref, ind_ref, bdm_ref, o_ref,
                lf_ref, lb_ref, acc_ref, sf_ref, sb_ref, *, row_off):
    hi = lax.Precision.HIGHEST
    C = GLA_CHUNK
    a = a_ref[0].astype(F32)
    rin = lax.broadcasted_iota(jnp.int32, (L, 1), 0) & (C - 1)

    def block_cumsum(x, forward):
        for sh in (1, 2, 4, 8):
            if forward:
                x = x + jnp.where(rin >= sh, pltpu.roll(x, sh, 0), 0.0)
            else:
                x = x + jnp.where(rin < C - sh, pltpu.roll(x, L - sh, 0), 0.0)
        return x

    lf_ref[...] = block_cumsum(_log_sigmoid(jnp.dot(a, wf_ref[...], preferred_element_type=F32, precision=hi)
                                            + gb_ref[0:1, :]) * (1.0 / GLA_TEMP), True)
    lb_ref[...] = block_cumsum(_log_sigmoid(jnp.dot(a, wb_ref[...], preferred_element_type=F32, precision=hi)
                                            + gb_ref[1:2, :]) * (1.0 / GLA_TEMP), False)
    acc_ref[...] = jnp.zeros_like(acc_ref)
    sf_ref[...] = jnp.zeros_like(sf_ref)
    sb_ref[...] = jnp.zeros_like(sb_ref)

    rowi = lax.broadcasted_iota(jnp.int32, (C, GLA_QW), 0)

    def block(r0, b_ref, s_ref, forward):
        q = q_ref[0, pl.ds(r0, C), :].astype(F32) * (GLA_DK ** -0.5)
        k = k_ref[0, pl.ds(r0, C), :].astype(F32)
        v = v_ref[0, pl.ds(r0, C), :].astype(F32)
        b = b_ref[pl.ds(r0, C), :]
        btot = b[C - 1:C, :] if forward else b[0:1, :]
        st = s_ref[...]
        o = lax.dot_general((q * jnp.exp(b)).astype(BF16), st.astype(BF16), (((1,), (1,)), ((), ())),
                            preferred_element_type=F32)
        ps = []
        for s in range(C):
            seen = (rowi >= s) if forward else (rowi <= s)
            e = jnp.exp(jnp.where(seen, b - b[s:s + 1, :], -jnp.inf))
            ps.append(q * k[s:s + 1, :] * e)
        pm = jnp.concatenate(ps, axis=0).astype(BF16)
        rm = jnp.dot(pm, ind_ref[...], preferred_element_type=F32)
        for s in range(C):
            o = o + rm[s * C:(s + 1) * C, :] * v[s:s + 1, :]
        acc_ref[pl.ds(r0, C), :] += o
        kd = (k * jnp.exp(btot - b)).astype(BF16)
        upd = lax.dot_general(v.astype(BF16), kd, (((0,), (0,)), ((), ())), preferred_element_type=F32)
        s_ref[...] = st * jnp.exp(btot) + upd * bdm_ref[...]

    def body(i, carry):
        block(pl.multiple_of(i * C, C), lf_ref, sf_ref, True)
        jb = jnp.where(i < N_BLK_C, N_BLK_C - 1 - i, N_BLK + N_BLK_C - 1 - i)
        block(pl.multiple_of(jb * C, C), lb_ref, sb_ref, False)
        return carry

    lax.fori_loop(0, N_BLK, body, 0)
    o = _head_rms(acc_ref[row_off:, :], nw_ref[...])
    g = g_ref[0, row_off:, :].astype(F32)
    o_ref[0] = (o * (g / (1.0 + jnp.exp(-g)))).astype(o_ref.dtype)


def gla_mixer(u, gate_w, gate_b, norm_w, with_ctx):
    row_off = 0 if with_ctx else LC
    qc, kc = U_OFF["gla_q"] // 256, U_OFF["gla_k"] // 256
    vc, gc, ac = U_OFF["gla_v"] // 512, U_OFF["gla_g"] // 512, U_OFF["gla_a"] // 128
    wf = jnp.zeros((128, GLA_QW), F32).at[0:GLA_RANK].set(gate_w[0])
    wb = jnp.zeros((128, GLA_QW), F32).at[GLA_RANK:2 * GLA_RANK].set(gate_w[1])
    hd = np.arange(GLA_QW)[:, None] // GLA_DK == np.arange(GROUP_W)[None, :] // GLA_DV
    ind = jnp.asarray(hd, BF16)
    bdm = jnp.asarray(hd.T, F32)
    full = lambda *shape: pl.BlockSpec(shape, lambda b: (0,) * len(shape))
    return pl.pallas_call(
        functools.partial(_gla_kernel, row_off=row_off),
        grid=(B,),
        in_specs=[pl.BlockSpec((1, L, 256), lambda b: (b, 0, qc)),
                  pl.BlockSpec((1, L, 256), lambda b: (b, 0, kc)),
                  pl.BlockSpec((1, L, 512), lambda b: (b, 0, vc)),
                  pl.BlockSpec((1, L, 512), lambda b: (b, 0, gc)),
                  pl.BlockSpec((1, L, 128), lambda b: (b, 0, ac)),
                  full(128, GLA_QW), full(128, GLA_QW), full(2, GLA_QW), full(1, 128),
                  full(GLA_QW, GROUP_W), full(GROUP_W, GLA_QW)],
        out_specs=pl.BlockSpec((1, L - row_off, GROUP_W), lambda b: (b, 0, 0)),
        out_shape=jax.ShapeDtypeStruct((B, L - row_off, GROUP_W), BF16),
        scratch_shapes=[pltpu.VMEM((L, GLA_QW), F32), pltpu.VMEM((L, GLA_QW), F32), pltpu.VMEM((L, GROUP_W), F32),
                        pltpu.VMEM((GROUP_W, GLA_QW), F32), pltpu.VMEM((GROUP_W, GLA_QW), F32)],
        compiler_params=_cparams(("parallel",)),
    )(u, u, u, u, u, wf, wb, gate_b, jnp.tile(norm_w, 2)[None], ind, bdm)


HY_CT = 256
HY_TK = 512


@functools.lru_cache(maxsize=None)
def _dft_consts(Lh):
    k = np.arange(Lh, dtype=np.int64)
    ph = (np.outer(k, k) % (2 * Lh)).astype(np.float64) * (np.pi / Lh)
    sgn = (1.0 - 2.0 * (k % 2)).astype(np.float32)[:, None]
    return np.cos(ph).astype(np.float32), np.sin(ph).astype(np.float32), sgn


def _hyena_features(Lh):
    t = jnp.linspace(0.0, 1.0, Lh, dtype=F32)[:, None]
    w = (2.0 * math.pi / Lh) * jnp.arange(Lh, dtype=F32)[:, None]
    bands = jnp.linspace(1e-4, HY_BANDS - 1, HY_BANDS, dtype=F32)
    z = jnp.concatenate([t, jnp.cos(w * bands), -jnp.sin(w * bands)], axis=-1)
    return jnp.pad(z, ((0, 0), (0, 128 - z.shape[1])))


def _hy_filter_kernel(z_ref, w1_ref, b1_ref, w2_ref, b2_ref, fr_ref, w3f_ref, w3b_ref, dl_ref, c_ref, s_ref, sgn_ref,
                      hc_ref, hs_ref, hn_ref, *, Lh):
    hi = lax.Precision.HIGHEST
    z = z_ref[...]
    h = jnp.sin(fr_ref[0:1, :] * (jnp.dot(z, w1_ref[...], preferred_element_type=F32, precision=hi) + b1_ref[...]))
    h = jnp.sin(fr_ref[1:2, :] * (jnp.dot(h, w2_ref[...], preferred_element_type=F32, precision=hi) + b2_ref[...]))
    dec = jnp.exp(-(z[:, 0:1] * dl_ref[...]))
    hf = jnp.dot(h, w3f_ref[...], preferred_element_type=F32, precision=hi) * dec
    hb = jnp.dot(h, w3b_ref[...], preferred_element_type=F32, precision=hi) * dec
    nrm = jnp.sum(jnp.abs(hf), axis=0, keepdims=True) + jnp.sum(jnp.abs(hb), axis=0, keepdims=True)
    row = lax.broadcasted_iota(jnp.int32, (Lh, 1), 0)
    hf = hf / nrm
    hb = jnp.where(row == 0, 0.0, hb / nrm)
    wk = jnp.where(row == 0, 0.5 / Lh, 1.0 / Lh)

    def project(m_ref, x):
        xh = x.astype(BF16)
        xl = (x - xh.astype(F32)).astype(BF16)
        return (jnp.dot(m_ref[...], xh, preferred_element_type=F32)
                + jnp.dot(m_ref[...], xl, preferred_element_type=F32))

    am = hf + hb
    hc_ref[0] = project(c_ref, am) * wk
    hs_ref[0] = project(s_ref, hf - hb) * wk
    hn_ref[0] = jnp.sum(am * sgn_ref[...], axis=0, keepdims=True) * (0.5 / Lh)


def hyena_spectrum(Lh, w1, b1, w2, b2, w3, fr, cmat, smat, sgn):
    z = _hyena_features(Lh)
    w1p = jnp.pad(w1, ((0, 128 - w1.shape[0]), (0, 0)))
    deltas = jnp.linspace(HY_MIN_DECAY, HY_MAX_DECAY, HY_CH, dtype=F32)[None]
    nct = HY_CH // HY_CT
    full = lambda *shape: pl.BlockSpec(shape, lambda o, c: (0,) * len(shape))
    return pl.pallas_call(
        functools.partial(_hy_filter_kernel, Lh=Lh),
        grid=(HY_ORDER, nct),
        in_specs=[full(Lh, 128), full(128, 64), full(1, 64), full(64, 64), full(1, 64), full(2, 64),
                  pl.BlockSpec((64, HY_CT), lambda o, c: (0, o * 2 * nct + c)),
                  pl.BlockSpec((64, HY_CT), lambda o, c: (0, o * 2 * nct + nct + c)),
                  pl.BlockSpec((1, HY_CT), lambda o, c: (0, c)),
                  full(Lh, Lh), full(Lh, Lh), full(Lh, 1)],
        out_specs=[pl.BlockSpec((1, Lh, HY_CT), lambda o, c: (o, 0, c)),
                   pl.BlockSpec((1, Lh, HY_CT), lambda o, c: (o, 0, c)),
                   pl.BlockSpec((1, 1, HY_CT), lambda o, c: (o, 0, c))],
        out_shape=[jax.ShapeDtypeStruct((HY_ORDER, Lh, HY_CH), F32),
                   jax.ShapeDtypeStruct((HY_ORDER, Lh, HY_CH), F32),
                   jax.ShapeDtypeStruct((HY_ORDER, 1, HY_CH), F32)],
        compiler_params=_cparams(("arbitrary", "arbitrary")),
    )(z, w1p, b1[None], w2, b2[None], fr, w3, w3, deltas, cmat, smat, sgn)


def _hyena_seq(row0, Lh, x1_ref, x2_ref, y_ref, cw_refs, cb_refs, bias_ref, c_ref, s_ref, hc_ref, hs_ref, hn_ref, *,
               scratch):
    r = lax.broadcasted_iota(jnp.int32, (Lh, 1), 0)

    def sconv(u_ref, w_ref, b_ref):
        u = u_ref[0, row0:row0 + Lh, :].astype(F32)
        up = jnp.where(r == 0, 0.0, pltpu.roll(u, 1, 0))
        dn = jnp.where(r == Lh - 1, 0.0, pltpu.roll(u, Lh - 1, 0))
        return up * w_ref[0:1, :] + u * w_ref[1:2, :] + dn * w_ref[2:3, :] + b_ref[...]

    gate_refs = (x1_ref, x2_ref)
    ys_ref, yb_ref, cv_ref = scratch
    rows = slice(0, Lh)
    ys_ref[rows, :] = sconv(y_ref, cw_refs[2], cb_refs[2])
    sgn = (1 - 2 * (r & 1)).astype(F32)
    tk = min(HY_TK, Lh)
    for o in range(HY_ORDER):
        y = ys_ref[rows, :]
        yb_ref[rows, :] = y.astype(BF16)
        cv_ref[rows, :] = sgn * (jnp.sum(y * sgn, axis=0, keepdims=True) * hn_ref[o])
        for m in range(Lh // tk):
            fs = slice(m * tk, (m + 1) * tk)
            yc = jnp.dot(c_ref[fs, :], yb_ref[rows, :], preferred_element_type=F32)
            ysn = jnp.dot(s_ref[fs, :], yb_ref[rows, :], preferred_element_type=F32)
            hc = hc_ref[o, fs, :]
            hs = hs_ref[o, fs, :]
            pc = (yc * hc - ysn * hs).astype(BF16)
            ps = (yc * hs + ysn * hc).astype(BF16)
            cv_ref[rows, :] += (jnp.dot(c_ref[:, fs], pc, preferred_element_type=F32)
                                + jnp.dot(s_ref[:, fs], ps, preferred_element_type=F32))
        ys_ref[rows, :] = (sconv(gate_refs[o], cw_refs[o], cb_refs[o])
                           * (cv_ref[rows, :] + ys_ref[rows, :] * bias_ref[o:o + 1, :]))
    return ys_ref[rows, :]


def _hyena_kernel(x1_ref, x2_ref, y_ref, w1_ref, w2_ref, w3_ref, b1_ref, b2_ref, b3_ref, bias_ref, *rest, with_ctx):
    nmat = 10 if with_ctx else 5
    o_ref = rest[nmat]
    scratch = rest[nmat + 1:]
    seqs = [(LC, S, rest[0:5])]
    if with_ctx:
        seqs.append((0, LC, rest[5:10]))
    for row0, Lh, mats in seqs:
        y = _hyena_seq(row0, Lh, x1_ref, x2_ref, y_ref, (w1_ref, w2_ref, w3_ref), (b1_ref, b2_ref, b3_ref),
                       bias_ref, *mats, scratch=scratch)
        out0 = row0 if with_ctx else 0
        o_ref[0, out0:out0 + Lh, :] = y.astype(o_ref.dtype)


def hyena_mixer(u, conv_w, conv_b, w1, b1, w2, b2, w3, fr, bias, with_ctx):
    nct = HY_CH // HY_CT
    hc0 = U_OFF["hy"] // HY_CT
    once = pl.Buffered(1)
    consts = []
    specs = []
    for Lh in ((S, LC) if with_ctx else (S,)):
        cm, sm, sgn = _dft_consts(Lh)
        cm, sm, sgn = jnp.asarray(cm, BF16), jnp.asarray(sm, BF16), jnp.asarray(sgn)
        hc, hs, hn = hyena_spectrum(Lh, w1, b1, w2, b2, w3, fr, cm, sm, sgn)
        consts += [cm, sm, hc, hs, hn]
        specs += [pl.BlockSpec((Lh, Lh), lambda c, b: (0, 0), pipeline_mode=once),
                  pl.BlockSpec((Lh, Lh), lambda c, b: (0, 0), pipeline_mode=once),
                  pl.BlockSpec((HY_ORDER, Lh, HY_CT), lambda c, b: (0, 0, c), pipeline_mode=once),
                  pl.BlockSpec((HY_ORDER, Lh, HY_CT), lambda c, b: (0, 0, c), pipeline_mode=once),
                  pl.BlockSpec((HY_ORDER, 1, HY_CT), lambda c, b: (0, 0, c), pipeline_mode=once)]
    rows_out = L if with_ctx else S
    ublk = lambda j: pl.BlockSpec((1, L, HY_CT), lambda c, b: (b, 0, hc0 + j * nct + c))
    wblk = lambda j: pl.BlockSpec((3, HY_CT), lambda c, b: (0, j * nct + c))
    bblk = lambda j: pl.BlockSpec((1, HY_CT), lambda c, b: (0, j * nct + c))
    return pl.pallas_call(
        functools.partial(_hyena_kernel, with_ctx=with_ctx),
        grid=(nct, B),
        in_specs=[ublk(0), ublk(1), ublk(2), wblk(0), wblk(1), wblk(2), bblk(0), bblk(1), bblk(2),
                  pl.BlockSpec((HY_ORDER, HY_CT), lambda c, b: (0, c))] + specs,
        out_specs=pl.BlockSpec((1, rows_out, HY_CT), lambda c, b: (b, 0, c)),
        out_shape=jax.ShapeDtypeStruct((B, rows_out, HY_CH), BF16),
        scratch_shapes=[pltpu.VMEM((S, HY_CT), F32), pltpu.VMEM((S, HY_CT), BF16), pltpu.VMEM((S, HY_CT), F32)],
        compiler_params=_cparams(("arbitrary", "arbitrary")),
    )(u, u, u, conv_w, conv_w, conv_w, conv_b[None], conv_b[None], conv_b[None], bias, *consts)


def _permute_w_in(w):
    parts = []
    for n in U_ORDER:
        o, wd = REF_COLS[n]
        p = w[:, o:o + wd]
        if wd < 128:
            p = jnp.pad(p, ((0, 0), (0, 128 - wd)))
        parts.append(p)
    parts.append(jnp.zeros((w.shape[0], U_W - U_USED), w.dtype))
    return jnp.concatenate(parts, axis=1).astype(BF16)


def kernel(x, c, ctx, c_ctx, norm1_w, norm2_w, ada_w, ada_b, w_in, w_out, gla_gate_w, gla_gate_b, gla_norm_w,
           swa_q_norm_w, swa_k_norm_w, swa_sink, hyena_conv_w, hyena_conv_b, hyena_ffn_w1, hyena_ffn_b1,
           hyena_ffn_w2, hyena_ffn_b2, hyena_ffn_w3, hyena_ffn_freq, hyena_bias, diff_q_norm_w, diff_k_norm_w,
           diff_lambda, diff_subln_w, router_w, router_bias, expert_w_gate, expert_w_up, expert_w_down):
    assert x.shape == (B, S, D) and ctx.shape == (B, LC, D)
    cc = jnp.zeros((16, D), F32).at[:B].set(c).at[B].set(c_ctx)
    mods = ada_mod(cc, ada_w, ada_b)
    cos, sin = rope_tables128()
    rw_pad = jnp.pad(router_w, ((0, 0), (0, 128 - N_EXPERTS)))
    xall = jnp.concatenate([ctx, x], axis=1)

    for l in range(DEPTH):
        with_ctx = l < DEPTH - 1
        lam_init = 0.8 - 0.6 * math.exp(-0.3 * l)
        ml = mods[l, :B].reshape(B, 1, 6 * D)
        mc = mods[l, B:B + 1]
        u = in_proj(xall, norm1_w[l][None], ml, mc, _permute_w_in(w_in[l]))
        mix_a = gla_mixer(u, gla_gate_w[l], gla_gate_b[l], gla_norm_w[l], with_ctx)
        mix_b = swa_mixer(u, swa_sink[l], swa_q_norm_w[l], swa_k_norm_w[l], cos, sin, with_ctx)
        mix_y = hyena_mixer(u, hyena_conv_w[l], hyena_conv_b[l], hyena_ffn_w1[l], hyena_ffn_b1[l],
                            hyena_ffn_w2[l], hyena_ffn_b2[l], hyena_ffn_w3[l], hyena_ffn_freq[l],
                            hyena_bias[l], with_ctx)
        mix_d = diff_mixer(u, diff_q_norm_w[l], diff_k_norm_w[l], cos, sin, diff_lambda[l], diff_subln_w[l],
                           lam_init, with_ctx)
        xall, f, logits = out_proj(xall, (mix_a, mix_b, mix_y, mix_d), w_out[l].astype(BF16), norm2_w[l][None],
                                   ml, mc, rw_pad, with_ctx)
        T = B * xall.shape[1]
        src, wts, dest, tile_e, tile_v, tile_n = route(logits.reshape(T, 128)[:, :N_EXPERTS], router_bias)
        ys = moe_experts(tile_e, tile_v, tile_n, src, f.reshape(T * TOK_SUB, 128), expert_w_gate, expert_w_up,
                         expert_w_down, l)
        xall = moe_combine(dest, wts, xall, ml, mc, ys, with_ctx)
    return xall
```

```python
import functools
import math

import numpy as np
import jax
import jax.numpy as jnp
from jax import lax
from jax.experimental import pallas as pl
from jax.experimental.pallas import tpu as pltpu

F32 = jnp.float32
BF16 = jnp.bfloat16

D = 2048
B = 8
S = 2048
LC = 256
L = LC + S
DEPTH = 2
GRID_W = 64
HEAD_DIM = 64
ROPE_THETA = 10000.0
EPS = 1e-6
GROUP_W = 512

GLA_DV = 64
GLA_DK = 32
GLA_HEADS = 8
GLA_RANK = 16
GLA_TEMP = 16.0
GLA_CHUNK = 16
SWA_HEADS = 8
SWA_KV = 2
SWA_WINDOW = 128
HY_CH = 512
HY_ORDER = 2
HY_BANDS = 16
HY_TARGET = 1e-2
HY_MIN_DECAY = math.log(1.0 / HY_TARGET) / 1.5
HY_MAX_DECAY = math.log(1.0 / HY_TARGET) / 0.3
DIFF_HEADS = 4
N_EXPERTS = 16
N_GROUPS = 4
EPG = 4
TOP_K = 2
D_EXPERT = 1024

REF_COLS = dict(gla_q=(0, 256), gla_k=(256, 256), gla_v=(512, 512), gla_g=(1024, 512), gla_a=(1536, 32),
                swa_q=(1568, 512), swa_k=(2080, 128), swa_v=(2208, 128), hy=(2336, 1536),
                diff_q=(3872, 512), diff_k=(4384, 512), diff_v=(4896, 512))
U_ORDER = ("hy", "gla_v", "gla_g", "swa_q", "diff_q", "diff_k", "diff_v", "gla_q", "gla_k", "swa_k", "swa_v", "gla_a")
U_OFF = {}
_o = 0
for _n in U_ORDER:
    U_OFF[_n] = _o
    _o += max(REF_COLS[_n][1], 128)
U_USED = _o
TN_IN = 512
U_W = -(-U_USED // TN_IN) * TN_IN

VMEM_LIMIT = 56 * 1024 * 1024


def _cparams(sem):
    return pltpu.CompilerParams(dimension_semantics=sem, vmem_limit_bytes=VMEM_LIMIT)


def _ada_kernel(c_ref, w_ref, b_ref, o_ref):
    c = c_ref[...]
    a = c / (1.0 + jnp.exp(-c))
    o_ref[0] = jnp.dot(a, w_ref[0], preferred_element_type=F32, precision=lax.Precision.HIGHEST) + b_ref[0]


def ada_mod(cc, ada_w, ada_b):
    tn = 1024
    return pl.pallas_call(
        _ada_kernel,
        grid=(DEPTH, 6 * D // tn),
        in_specs=[pl.BlockSpec((16, D), lambda l, j: (0, 0)),
                  pl.BlockSpec((1, D, tn), lambda l, j: (l, 0, j)),
                  pl.BlockSpec((1, 1, tn), lambda l, j: (l, 0, j))],
        out_specs=pl.BlockSpec((1, 16, tn), lambda l, j: (l, 0, j)),
        out_shape=jax.ShapeDtypeStruct((DEPTH, 16, 6 * D), F32),
        compiler_params=_cparams(("parallel", "parallel")),
    )(cc, ada_w, ada_b.reshape(DEPTH, 1, 6 * D))


TM_IN = 1152
CH_IN = 32


def _inproj_kernel(x_ref, nw_ref, scl_ref, shl_ref, scc_ref, shc_ref, w_ref, o_ref, h_ref):
    i = pl.program_id(1)

    @pl.when(pl.program_id(2) == 0)
    def _():
        def chunk(c, carry):
            r0 = pl.multiple_of(c * CH_IN, CH_IN)
            x = x_ref[0, pl.ds(r0, CH_IN), :]
            ms = jnp.mean(x * x, axis=-1, keepdims=True)
            y = x * lax.rsqrt(ms + EPS) * nw_ref[...]
            row = i * TM_IN + r0 + lax.broadcasted_iota(jnp.int32, (CH_IN, 1), 0)
            is_ctx = row < LC
            sc = jnp.where(is_ctx, scc_ref[...], scl_ref[0])
            sh = jnp.where(is_ctx, shc_ref[...], shl_ref[0])
            h_ref[pl.ds(r0, CH_IN), :] = (y * (1.0 + sc) + sh).astype(BF16)
            return carry

        lax.fori_loop(0, TM_IN // CH_IN, chunk, 0)

    o_ref[0] = jnp.dot(h_ref[...], w_ref[...], preferred_element_type=F32).astype(o_ref.dtype)


def in_proj(xall, nw, ml, mc, w_in_p):
    return pl.pallas_call(
        _inproj_kernel,
        grid=(B, L // TM_IN, U_W // TN_IN),
        in_specs=[pl.BlockSpec((1, TM_IN, D), lambda b, i, j: (b, i, 0)),
                  pl.BlockSpec((1, D), lambda b, i, j: (0, 0)),
                  pl.BlockSpec((1, 1, D), lambda b, i, j: (b, 0, 1)),
                  pl.BlockSpec((1, 1, D), lambda b, i, j: (b, 0, 0)),
                  pl.BlockSpec((1, D), lambda b, i, j: (0, 1)),
                  pl.BlockSpec((1, D), lambda b, i, j: (0, 0)),
                  pl.BlockSpec((D, TN_IN), lambda b, i, j: (0, j))],
        out_specs=pl.BlockSpec((1, TM_IN, TN_IN), lambda b, i, j: (b, i, j)),
        out_shape=jax.ShapeDtypeStruct((B, L, U_W), BF16),
        scratch_shapes=[pltpu.VMEM((TM_IN, D), BF16)],
        compiler_params=_cparams(("parallel", "parallel", "arbitrary")),
    )(xall, nw, ml, ml, mc, mc, w_in_p)


def _norm_rope(x, w, cos, sin):
    R, W = x.shape
    lane = lax.broadcasted_iota(jnp.int32, (R, 128), 1)
    left = lane < HEAD_DIM
    first = (lane & 31) < 16
    outs = []
    for c in range(W // 128):
        xc = x[:, c * 128:(c + 1) * 128]
        sq = xc * xc
        sl = jnp.sum(jnp.where(left, sq, 0.0), axis=-1, keepdims=True)
        sr = jnp.sum(jnp.where(left, 0.0, sq), axis=-1, keepdims=True)
        inv = jnp.where(left, lax.rsqrt(sl * (1.0 / HEAD_DIM) + EPS), lax.rsqrt(sr * (1.0 / HEAD_DIM) + EPS))
        y = xc * inv * w
        rot = jnp.where(first, pltpu.roll(y, 112, 1), pltpu.roll(y, 16, 1))
        outs.append(y * cos + rot * sin)
    return outs[0] if len(outs) == 1 else jnp.concatenate(outs, axis=-1)


def rope_tables128():
    n = HEAD_DIM // 4
    rows = S // GRID_W
    row_ids = jnp.repeat(jnp.arange(rows), GRID_W).astype(F32)
    col_ids = jnp.tile(jnp.arange(GRID_W), rows).astype(F32)
    inv = ROPE_THETA ** (-jnp.arange(n, dtype=F32) / n)
    ar = row_ids[:, None] * inv
    ac = col_ids[:, None] * inv
    cos64 = jnp.concatenate([jnp.cos(ar), jnp.cos(ar), jnp.cos(ac), jnp.cos(ac)], axis=-1)
    sin64 = jnp.concatenate([-jnp.sin(ar), jnp.sin(ar), -jnp.sin(ac), jnp.sin(ac)], axis=-1)
    cos = jnp.concatenate([jnp.ones((LC, 64), F32), cos64], axis=0)
    sin = jnp.concatenate([jnp.zeros((LC, 64), F32), sin64], axis=0)
    return jnp.tile(cos, (1, 2)), jnp.tile(sin, (1, 2))


def _row_max(s):
    m = s[:, 0:128]
    for c in range(1, s.shape[1] // 128):
        m = jnp.maximum(m, s[:, c * 128:(c + 1) * 128])
    return jnp.max(m, axis=-1, keepdims=True)


def _row_sum(s):
    a = s[:, 0:128]
    for c in range(1, s.shape[1] // 128):
        a = a + s[:, c * 128:(c + 1) * 128]
    return jnp.sum(a, axis=-1, keepdims=True)


SWA_BAND = 3 * SWA_WINDOW


def _swa_kernel(sink_ref, q_ref, k_ref, v_ref, qw_ref, kw_ref, cos_ref, sin_ref, o_ref, kn_ref, *, n_off):
    n = pl.program_id(1) + n_off

    @pl.when(pl.program_id(1) == 0)
    def _():
        kn_ref[...] = _norm_rope(k_ref[0].astype(F32), kw_ref[...], cos_ref[...], sin_ref[...]).astype(BF16)

    r0 = pl.multiple_of(n * 128, 128)
    q = _norm_rope(q_ref[0].astype(F32), qw_ref[...], cos_ref[pl.ds(r0, 128), :], sin_ref[pl.ds(r0, 128), :])
    q = (q * (HEAD_DIM ** -0.5)).astype(BF16)

    nb = n - LC // 128
    is_lat = nb >= 0
    lstart = jnp.clip((nb - 1) * 128, 0, S - SWA_BAND)
    start = pl.multiple_of(LC + lstart, 128)
    kk = jnp.concatenate([kn_ref[0:LC, :], kn_ref[pl.ds(start, SWA_BAND), :]], axis=0)
    vv = jnp.concatenate([v_ref[0, 0:LC, :], v_ref[0, pl.ds(start, SWA_BAND), :]], axis=0)
    nk = LC + SWA_BAND
    row = lax.broadcasted_iota(jnp.int32, (512, nk), 0) & 127
    col = lax.broadcasted_iota(jnp.int32, (512, nk), 1)
    qpos = nb * 128 + row
    kpos = lstart + col - LC
    valid = (col < LC) | ((jnp.abs(qpos - kpos) <= SWA_WINDOW) & is_lat)
    rowi = lax.broadcasted_iota(jnp.int32, (512, 1), 0)
    outs = [None] * SWA_HEADS
    for hk in range(SWA_KV):
        qs = jnp.concatenate([q[:, (hk * 4 + g) * 64:(hk * 4 + g + 1) * 64] for g in range(4)], axis=0)
        s = lax.dot_general(qs, kk[:, hk * 64:(hk + 1) * 64], (((1,), (1,)), ((), ())),
                            preferred_element_type=F32)
        s = jnp.where(valid, s, -jnp.inf)
        sk = jnp.where(rowi < 128, sink_ref[hk * 4],
                       jnp.where(rowi < 256, sink_ref[hk * 4 + 1],
                                 jnp.where(rowi < 384, sink_ref[hk * 4 + 2], sink_ref[hk * 4 + 3])))
        m = jnp.maximum(_row_max(s), sk)
        e = jnp.exp(s - m)
        den = _row_sum(e) + jnp.exp(sk - m)
        o = jnp.dot(e.astype(BF16), vv[:, hk * 64:(hk + 1) * 64], preferred_element_type=F32) / den
        for g in range(4):
            outs[hk * 4 + g] = o[g * 128:(g + 1) * 128]
    o_ref[0] = jnp.concatenate(outs, axis=-1).astype(o_ref.dtype)


def swa_mixer(u, sink, qw, kw, cos, sin, with_ctx):
    n_off = 0 if with_ctx else LC // 128
    nblk = L // 128 - n_off
    qc, kc, vc = U_OFF["swa_q"] // 512, U_OFF["swa_k"] // 128, U_OFF["swa_v"] // 128
    return pl.pallas_call(
        functools.partial(_swa_kernel, n_off=n_off),
        grid_spec=pltpu.PrefetchScalarGridSpec(
            num_scalar_prefetch=1,
            grid=(B, nblk),
            in_specs=[pl.BlockSpec((1, 128, 512), lambda b, n, s: (b, n + n_off, qc)),
                      pl.BlockSpec((1, L, 128), lambda b, n, s: (b, 0, kc)),
                      pl.BlockSpec((1, L, 128), lambda b, n, s: (b, 0, vc)),
                      pl.BlockSpec((1, 128), lambda b, n, s: (0, 0)),
                      pl.BlockSpec((1, 128), lambda b, n, s: (0, 0)),
                      pl.BlockSpec((L, 128), lambda b, n, s: (0, 0)),
                      pl.BlockSpec((L, 128), lambda b, n, s: (0, 0))],
            out_specs=pl.BlockSpec((1, 128, 512), lambda b, n, s: (b, n, 0)),
            scratch_shapes=[pltpu.VMEM((L, 128), BF16)]),
        out_shape=jax.ShapeDtypeStruct((B, nblk * 128, GROUP_W), BF16),
        compiler_params=_cparams(("parallel", "arbitrary")),
    )(sink, u, u, u, jnp.tile(qw, 2)[None], jnp.tile(kw, 2)[None], cos, sin)


TQ_DIFF = 256
KC_DIFF = 768


def _diff_kernel(q_ref, k_ref, v_ref, qw_ref, kw_ref, cos_ref, sin_ref, dl_ref, sw_ref, o_ref, kn_ref, vx_ref,
                 *, j_off, lam_init):
    j = pl.program_id(2) + j_off

    @pl.when(pl.program_id(2) == 0)
    def _():
        kn_ref[...] = _norm_rope(k_ref[0].astype(F32), kw_ref[...], cos_ref[...], sin_ref[...]).astype(BF16)
        vx_ref[:, 0:128] = v_ref[0]
        vx_ref[:, 128:256] = jnp.ones((L, 128), BF16)

    dl = dl_ref[...]
    lam = (jnp.exp(jnp.sum(dl[0:1] * dl[1:2], axis=-1, keepdims=True))
           - jnp.exp(jnp.sum(dl[2:3] * dl[3:4], axis=-1, keepdims=True)) + lam_init)
    r0 = pl.multiple_of(j * TQ_DIFF, TQ_DIFF)
    q = _norm_rope(q_ref[0].astype(F32), qw_ref[...], cos_ref[pl.ds(r0, TQ_DIFF), :], sin_ref[pl.ds(r0, TQ_DIFF), :])
    q = q * (HEAD_DIM ** -0.5)
    lane = lax.broadcasted_iota(jnp.int32, (TQ_DIFF, 128), 1)
    qms = [jnp.where(lane < 64, q, 0.0).astype(BF16), jnp.where(lane < 64, 0.0, q).astype(BF16)]

    def attend(nk):
        os_ = []
        kc = min(KC_DIFF, nk)
        for m in range(2):
            mrun = acc = None
            for c0 in range(0, nk, kc):
                s = lax.dot_general(qms[m], kn_ref[c0:c0 + kc, :], (((1,), (1,)), ((), ())),
                                    preferred_element_type=F32)
                mnew = _row_max(s) if mrun is None else jnp.maximum(mrun, _row_max(s))
                e = jnp.exp((s - mnew).astype(BF16))
                oe = jnp.dot(e, vx_ref[c0:c0 + kc, :], preferred_element_type=F32)
                acc = oe if acc is None else acc * jnp.exp(mrun - mnew) + oe
                mrun = mnew
            os_.append(acc[:, 0:128] / acc[:, 128:129])
        o = os_[0] - lam * os_[1]
        ms = jnp.mean(o * o, axis=-1, keepdims=True)
        o = o * lax.rsqrt(ms + EPS) * sw_ref[...] * (1.0 - lam_init)
        o_ref[0] = o.astype(o_ref.dtype)

    if j_off == 0:
        @pl.when(j == 0)
        def _():
            attend(LC)

    @pl.when(j > 0)
    def _():
        attend(L)


def diff_mixer(u, qw, kw, cos, sin, dlam, subw, lam_init, with_ctx):
    j_off = 0 if with_ctx else LC // TQ_DIFF
    nblk = L // TQ_DIFF - j_off
    qc, kc, vc = U_OFF["diff_q"] // 128, U_OFF["diff_k"] // 128, U_OFF["diff_v"] // 128
    return pl.pallas_call(
        functools.partial(_diff_kernel, j_off=j_off, lam_init=lam_init),
        grid=(B, DIFF_HEADS, nblk),
        in_specs=[pl.BlockSpec((1, TQ_DIFF, 128), lambda b, h, j: (b, j + j_off, qc + h)),
                  pl.BlockSpec((1, L, 128), lambda b, h, j: (b, 0, kc + h)),
                  pl.BlockSpec((1, L, 128), lambda b, h, j: (b, 0, vc + h)),
                  pl.BlockSpec((1, 128), lambda b, h, j: (0, 0)),
                  pl.BlockSpec((1, 128), lambda b, h, j: (0, 0)),
                  pl.BlockSpec((L, 128), lambda b, h, j: (0, 0)),
                  pl.BlockSpec((L, 128), lambda b, h, j: (0, 0)),
                  pl.BlockSpec((4, 64), lambda b, h, j: (0, 0)),
                  pl.BlockSpec((1, 128), lambda b, h, j: (0, 0))],
        out_specs=pl.BlockSpec((1, TQ_DIFF, 128), lambda b, h, j: (b, j, h)),
        out_shape=jax.ShapeDtypeStruct((B, nblk * TQ_DIFF, GROUP_W), BF16),
        scratch_shapes=[pltpu.VMEM((L, 128), BF16), pltpu.VMEM((L, 256), BF16)],
        compiler_params=_cparams(("parallel", "parallel", "arbitrary")),
    )(u, u, u, jnp.tile(qw, 2)[None], jnp.tile(kw, 2)[None], cos, sin, dlam, subw[None])


TM_OUT = 256
TOK_SUB = D // 128


def _store_token_rows(ref, val):
    R = val.shape[0]
    for s in range(TOK_SUB):
        ref[pl.ds(s, R, stride=TOK_SUB), :] = val[:, s * 128:(s + 1) * 128]


def _load_token_rows(ref, R):
    return jnp.concatenate([ref[pl.ds(s, R, stride=TOK_SUB), :] for s in range(TOK_SUB)], axis=-1)


def _outproj_kernel(x_ref, a_ref, b_ref, y_ref, d_ref, w_ref, nw_ref, ml_ref, mc_ref, rwh_ref, rwl_ref,
                    xo_ref, f_ref, lg_ref, *, i_off):
    is_ctx = (pl.program_id(1) + i_off) == 0

    def mod(k):
        return jnp.where(is_ctx, mc_ref[:, k * D:(k + 1) * D], ml_ref[0, :, k * D:(k + 1) * D])

    acc = jnp.dot(a_ref[0], w_ref[0:512, :], preferred_element_type=F32)
    acc += jnp.dot(b_ref[0], w_ref[512:1024, :], preferred_element_type=F32)
    acc += jnp.dot(y_ref[0], w_ref[1024:1536, :], preferred_element_type=F32)
    acc += jnp.dot(d_ref[0], w_ref[1536:2048, :], preferred_element_type=F32)
    xn = x_ref[0] + mod(2) * acc
    xo_ref[0] = xn
    ms = jnp.mean(xn * xn, axis=-1, keepdims=True)
    f = (xn * lax.rsqrt(ms + EPS) * nw_ref[...]) * (1.0 + mod(4)) + mod(3)
    _store_token_rows(f_ref.at[0], f)
    fh = f.astype(BF16)
    fl = (f - fh.astype(F32)).astype(BF16)
    lg_ref[0] = (jnp.dot(fh, rwh_ref[...], preferred_element_type=F32)
                 + jnp.dot(fl, rwh_ref[...], preferred_element_type=F32)
                 + jnp.dot(fh, rwl_ref[...], preferred_element_type=F32))


def out_proj(xall, mixes, w_out_b, nw, ml, mc, rw_pad, with_ctx):
    i_off = 0 if with_ctx else LC // TM_OUT
    nblk = L // TM_OUT - i_off
    rows = lambda b, i: (b, i + i_off, 0)
    rwh = rw_pad.astype(BF16)
    rwl = (rw_pad - rwh.astype(F32)).astype(BF16)
    return pl.pallas_call(
        functools.partial(_outproj_kernel, i_off=i_off),
        grid=(B, nblk),
        in_specs=[pl.BlockSpec((1, TM_OUT, D), rows)]
                 + [pl.BlockSpec((1, TM_OUT, GROUP_W), lambda b, i: (b, i, 0))] * 4
                 + [pl.BlockSpec((D, D), lambda b, i: (0, 0)),
                    pl.BlockSpec((1, D), lambda b, i: (0, 0)),
                    pl.BlockSpec((1, 1, 6 * D), lambda b, i: (b, 0, 0)),
                    pl.BlockSpec((1, 6 * D), lambda b, i: (0, 0)),
                    pl.BlockSpec((D, 128), lambda b, i: (0, 0)),
                    pl.BlockSpec((D, 128), lambda b, i: (0, 0))],
        out_specs=[pl.BlockSpec((1, TM_OUT, D), lambda b, i: (b, i, 0)),
                   pl.BlockSpec((1, TM_OUT * TOK_SUB, 128), lambda b, i: (b, i, 0)),
                   pl.BlockSpec((1, TM_OUT, 128), lambda b, i: (b, i, 0))],
        out_shape=[jax.ShapeDtypeStruct((B, nblk * TM_OUT, D), F32),
                   jax.ShapeDtypeStruct((B, nblk * TM_OUT * TOK_SUB, 128), F32),
                   jax.ShapeDtypeStruct((B, nblk * TM_OUT, 128), F32)],
        compiler_params=_cparams(("parallel", "parallel")),
    )(xall, *mixes, w_out_b, nw, ml, mc, rwh, rwl)


TM_MOE = 256


def _token_copy(src_hbm, idx, buf, r, sem):
    return pltpu.make_async_copy(src_hbm.at[pl.ds(idx * TOK_SUB, TOK_SUB)], buf.at[pl.ds(r * TOK_SUB, TOK_SUB)], sem)


def _moe_kernel(te_ref, tv_ref, tn_ref, src_ref, f_hbm, wg_hbm, wu_hbm, wd_hbm, o_ref, gbuf, gsem, sg, su, sd,
                wg_b, wu_b, wd_b, sem, *, layer):
    t = pl.program_id(0)
    e = te_ref[t]

    def issue(step, slot):
        def body(r, carry):
            _token_copy(f_hbm, src_ref[step * TM_MOE + r], gbuf.at[slot], r, gsem.at[slot]).start()
            return carry

        lax.fori_loop(0, TM_MOE, body, 0, unroll=8)

    @pl.when(t == 0)
    def _():
        issue(0, 0)

    @pl.when(t + 1 < pl.num_programs(0))
    def _():
        issue(t + 1, (t + 1) % 2)

    def copies(ex):
        return (pltpu.make_async_copy(wg_hbm.at[layer, ex], sg, sem.at[0]),
                pltpu.make_async_copy(wu_hbm.at[layer, ex], su, sem.at[1]),
                pltpu.make_async_copy(wd_hbm.at[layer, ex], sd, sem.at[2]))

    @pl.when(t == 0)
    def _():
        for cp in copies(e):
            cp.start()

    @pl.when((t == 0) | (e != te_ref[jnp.maximum(t - 1, 0)]))
    def _():
        for cp in copies(e):
            cp.wait()
        wg_b[...] = sg[...].astype(BF16)
        wu_b[...] = su[...].astype(BF16)
        wd_b[...] = sd[...].astype(BF16)
        nxt = tn_ref[t]

        @pl.when(nxt >= 0)
        def _():
            for cp in copies(nxt):
                cp.start()

    slot = t % 2

    def drain(r, carry):
        _token_copy(f_hbm, 0, gbuf.at[slot], r, gsem.at[slot]).wait()
        return carry

    lax.fori_loop(0, TM_MOE, drain, 0, unroll=8)

    @pl.when(tv_ref[t] == 0)
    def _():
        o_ref[...] = jnp.zeros_like(o_ref)

    @pl.when(tv_ref[t] > 0)
    def _():
        x = _load_token_rows(gbuf.at[slot], TM_MOE).astype(BF16)
        g = jnp.dot(x, wg_b[...], preferred_element_type=F32)
        u = jnp.dot(x, wu_b[...], preferred_element_type=F32)
        h = (g / (1.0 + jnp.exp(-g)) * u).astype(BF16)
        _store_token_rows(o_ref, jnp.dot(h, wd_b[...], preferred_element_type=F32))


def moe_experts(tile_e, tile_v, tile_n, src, f_tok, wg, wu, wd, layer):
    nrows = src.shape[0]
    return pl.pallas_call(
        functools.partial(_moe_kernel, layer=layer),
        grid_spec=pltpu.PrefetchScalarGridSpec(
            num_scalar_prefetch=4,
            grid=(nrows // TM_MOE,),
            in_specs=[pl.BlockSpec(memory_space=pl.ANY)] * 4,
            out_specs=pl.BlockSpec((TM_MOE * TOK_SUB, 128), lambda t, te, tv, tn, sr: (t, 0)),
            scratch_shapes=[pltpu.VMEM((2, TM_MOE * TOK_SUB, 128), F32), pltpu.SemaphoreType.DMA((2,)),
                            pltpu.VMEM((D, D_EXPERT), F32), pltpu.VMEM((D, D_EXPERT), F32),
                            pltpu.VMEM((D_EXPERT, D), F32),
                            pltpu.VMEM((D, D_EXPERT), BF16), pltpu.VMEM((D, D_EXPERT), BF16),
                            pltpu.VMEM((D_EXPERT, D), BF16),
                            pltpu.SemaphoreType.DMA((3,))]),
        out_shape=jax.ShapeDtypeStruct((nrows * TOK_SUB, 128), F32),
        compiler_params=_cparams(("arbitrary",)),
    )(tile_e, tile_v, tile_n, src, f_tok, wg, wu, wd)


R_COMB = 256


def _combine_kernel(pos_ref, x_ref, w_ref, ml_ref, mc_ref, y_hbm, o_ref, buf, sem, *, i_off):
    nblk = pl.num_programs(1)
    step = pl.program_id(0) * nblk + pl.program_id(1)
    is_ctx = (pl.program_id(1) + i_off) == 0

    def issue(st, slot):
        base = st * (R_COMB * TOP_K)

        def body(r, carry):
            for k in range(TOP_K):
                _token_copy(y_hbm, pos_ref[base + TOP_K * r + k], buf.at[slot, k], r, sem.at[slot, k]).start()
            return carry

        lax.fori_loop(0, R_COMB, body, 0, unroll=4)

    @pl.when(step == 0)
    def _():
        issue(0, 0)

    @pl.when(step + 1 < pl.num_programs(0) * nblk)
    def _():
        issue(step + 1, (step + 1) % 2)

    slot = step % 2

    def drain(r, carry):
        for k in range(TOP_K):
            _token_copy(y_hbm, 0, buf.at[slot, k], r, sem.at[slot, k]).wait()
        return carry

    lax.fori_loop(0, R_COMB, drain, 0, unroll=4)
    gf = jnp.where(is_ctx, mc_ref[:, 5 * D:6 * D], ml_ref[0, :, 5 * D:6 * D])
    w0 = w_ref[0, :, 0:1]
    w1 = w_ref[0, :, 1:2]
    for s in range(TOK_SUB):
        cs = slice(s * 128, (s + 1) * 128)
        y = (w0 * buf[slot, 0, pl.ds(s, R_COMB, stride=TOK_SUB), :]
             + w1 * buf[slot, 1, pl.ds(s, R_COMB, stride=TOK_SUB), :])
        o_ref[0, :, cs] = x_ref[0, :, cs] + gf[:, cs] * y


def moe_combine(pos, wts, xall, ml, mc, y_sorted, with_ctx):
    i_off = 0 if with_ctx else LC // R_COMB
    rows_out = xall.shape[1]
    nblk = rows_out // R_COMB
    return pl.pallas_call(
        functools.partial(_combine_kernel, i_off=i_off),
        grid_spec=pltpu.PrefetchScalarGridSpec(
            num_scalar_prefetch=1,
            grid=(B, nblk),
            in_specs=[pl.BlockSpec((1, R_COMB, D), lambda b, i, p: (b, i, 0)),
                      pl.BlockSpec((1, R_COMB, TOP_K), lambda b, i, p: (b, i, 0)),
                      pl.BlockSpec((1, 1, 6 * D), lambda b, i, p: (b, 0, 0)),
                      pl.BlockSpec((1, 6 * D), lambda b, i, p: (0, 0)),
                      pl.BlockSpec(memory_space=pl.ANY)],
            out_specs=pl.BlockSpec((1, R_COMB, D), lambda b, i, p: (b, i, 0)),
            scratch_shapes=[pltpu.VMEM((2, TOP_K, R_COMB * TOK_SUB, 128), F32),
                            pltpu.SemaphoreType.DMA((2, TOP_K))]),
        out_shape=jax.ShapeDtypeStruct((B, rows_out, D), F32),
        compiler_params=_cparams(("arbitrary", "arbitrary")),
    )(pos, xall, wts.reshape(B, rows_out, TOP_K), ml, mc, y_sorted)


CS_BLK = 256


def _token_cumsum(hot):
    E, T = hot.shape
    nb = T // CS_BLK
    tri = (jnp.arange(CS_BLK)[:, None] <= jnp.arange(CS_BLK)[None, :]).astype(F32)
    inner = jnp.einsum("ebk,kj->ebj", hot.reshape(E, nb, CS_BLK).astype(F32), tri,
                       precision=lax.Precision.HIGHEST)
    offs = jnp.cumsum(inner[:, :, -1], axis=1) - inner[:, :, -1]
    return (inner + offs[:, :, None]).astype(jnp.int32).reshape(E, T)


def route(logits, router_bias):
    T = logits.shape[0]
    lt = logits.T
    ex = jnp.exp(lt - jnp.max(lt, axis=0, keepdims=True))
    probs = ex / jnp.sum(ex, axis=0, keepdims=True)
    sel = (probs + router_bias.astype(F32)[:, None]).reshape(N_GROUPS, EPG, T)
    group = jnp.argmax(jnp.max(sel, axis=1), axis=0).astype(jnp.int32)
    sel_g = jnp.take_along_axis(sel, group[None, None, :], axis=0)[0]
    wid = jnp.arange(EPG, dtype=jnp.int32)[:, None]
    i0 = jnp.argmax(sel_g, axis=0).astype(jnp.int32)
    i1 = jnp.argmax(jnp.where(wid == i0[None, :], -jnp.inf, sel_g), axis=0).astype(jnp.int32)
    e0 = group * EPG + i0
    e1 = group * EPG + i1
    p0 = jnp.take_along_axis(probs, e0[None, :], axis=0)[0]
    p1 = jnp.take_along_axis(probs, e1[None, :], axis=0)[0]
    wts = jnp.stack([p0 / (p0 + p1), p1 / (p0 + p1)], axis=1)
    eid = jnp.arange(N_EXPERTS, dtype=jnp.int32)
    hot = (eid[:, None] == e0[None, :]).astype(jnp.int32) + (eid[:, None] == e1[None, :]).astype(jnp.int32)
    csum = _token_cumsum(hot)
    before = csum - hot
    r0 = jnp.take_along_axis(before, e0[None, :], axis=0)[0]
    r1 = jnp.take_along_axis(before, e1[None, :], axis=0)[0]
    counts = csum[:, -1]
    ptiles = (counts + TM_MOE - 1) // TM_MOE
    tend = jnp.cumsum(ptiles)
    tstart = tend - ptiles
    dest = jnp.stack([tstart[e0] * TM_MOE + r0, tstart[e1] * TM_MOE + r1], axis=1).reshape(-1).astype(jnp.int32)
    ntiles = -(-(TOP_K * T) // TM_MOE) + N_EXPERTS
    nrows = ntiles * TM_MOE
    tid = jnp.arange(ntiles, dtype=jnp.int32)
    tile_v = (tid < tend[-1]).astype(jnp.int32)
    tile_e = jnp.searchsorted(tend, jnp.minimum(tid, tend[-1] - 1), side="right").astype(jnp.int32)
    eid = jnp.arange(N_EXPERTS, dtype=jnp.int32)
    later = (eid[None, :] > eid[:, None]) & (ptiles[None, :] > 0)
    nxt = jnp.min(jnp.where(later, eid[None, :], N_EXPERTS), axis=1)
    tile_n = jnp.where(nxt < N_EXPERTS, nxt, -1)[tile_e].astype(jnp.int32)
    src = jnp.zeros((nrows,), jnp.int32).at[dest].set(jnp.repeat(jnp.arange(T, dtype=jnp.int32), TOP_K),
                                                      unique_indices=True)
    return src, wts, dest, tile_e, tile_v, tile_n


GLA_QW = GLA_HEADS * GLA_DK
N_BLK = L // GLA_CHUNK
N_BLK_C = LC // GLA_CHUNK


def _log_sigmoid(z):
    return jnp.minimum(z, 0.0) - jnp.log1p(jnp.exp(-jnp.abs(z)))


def _head_rms(x, w):
    R, W = x.shape
    lane = lax.broadcasted_iota(jnp.int32, (R, 128), 1)
    left = lane < 64
    outs = []
    for c in range(W // 128):
        xc = x[:, c * 128:(c + 1) * 128]
        sq = xc * xc
        sl = jnp.sum(jnp.where(left, sq, 0.0), axis=-1, keepdims=True)
        sr = jnp.sum(jnp.where(left, 0.0, sq), axis=-1, keepdims=True)
        inv = jnp.where(left, lax.rsqrt(sl * (1.0 / 64) + EPS), lax.rsqrt(sr * (1.0 / 64) + EPS))
        outs.append(xc * inv * w)
    return jnp.concatenate(outs, axis=-1)


def _gla_kernel(q_ref, k_ref, v_ref, g_ref, a_ref, wf_ref, wb_ref, gb_ref, nw_ref, ind_ref, bdm_ref, o_ref,
                lf_ref, lb_ref, acc_ref, sf_ref, sb_ref, *, row_off):
    hi = lax.Precision.HIGHEST
    C = GLA_CHUNK
    a = a_ref[0].astype(F32)
    rin = lax.broadcasted_iota(jnp.int32, (L, 1), 0) & (C - 1)

    def block_cumsum(x, forward):
        for sh in (1, 2, 4, 8):
            if forward:
                x = x + jnp.where(rin >= sh, pltpu.roll(x, sh, 0), 0.0)
            else:
                x = x + jnp.where(rin < C - sh, pltpu.roll(x, L - sh, 0), 0.0)
        return x

    lf_ref[...] = block_cumsum(_log_sigmoid(jnp.dot(a, wf_ref[...], preferred_element_type=F32, precision=hi)
                                            + gb_ref[0:1, :]) * (1.0 / GLA_TEMP), True)
    lb_ref[...] = block_cumsum(_log_sigmoid(jnp.dot(a, wb_ref[...], preferred_element_type=F32, precision=hi)
                                            + gb_ref[1:2, :]) * (1.0 / GLA_TEMP), False)
    acc_ref[...] = jnp.zeros_like(acc_ref)
    sf_ref[...] = jnp.zeros_like(sf_ref)
    sb_ref[...] = jnp.zeros_like(sb_ref)

    rowi = lax.broadcasted_iota(jnp.int32, (C, GLA_QW), 0)

    def block(r0, b_ref, s_ref, forward):
        q = q_ref[0, pl.ds(r0, C), :].astype(F32) * (GLA_DK ** -0.5)
        k = k_ref[0, pl.ds(r0, C), :].astype(F32)
        v = v_ref[0, pl.ds(r0, C), :].astype(F32)
        b = b_ref[pl.ds(r0, C), :]
        btot = b[C - 1:C, :] if forward else b[0:1, :]
        qe = (q * jnp.exp(b)).astype(BF16)
        o = jnp.concatenate(
            [lax.dot_general(qe[:, h * 128:(h + 1) * 128], s_ref[h].astype(BF16), (((1,), (1,)), ((), ())),
                             preferred_element_type=F32) for h in range(2)], axis=-1)
        ps = []
        for s in range(C):
            seen = (rowi >= s) if forward else (rowi <= s)
            e = jnp.exp(jnp.where(seen, b - b[s:s + 1, :], -jnp.inf))
            ps.append(q * k[s:s + 1, :] * e)
        pm = jnp.concatenate(ps, axis=0).astype(BF16)
        rm = jnp.dot(pm, ind_ref[...], preferred_element_type=F32)
        for s in range(C):
            o = o + rm[s * C:(s + 1) * C, :] * v[s:s + 1, :]
        acc_ref[pl.ds(r0, C), :] += o
        kd = (k * jnp.exp(btot - b)).astype(BF16)
        vb = v.astype(BF16)
        dec = jnp.exp(btot)
        for h in range(2):
            upd = lax.dot_general(vb[:, h * 256:(h + 1) * 256], kd[:, h * 128:(h + 1) * 128],
                                  (((0,), (0,)), ((), ())), preferred_element_type=F32)
            s_ref[h] = s_ref[h] * dec[:, h * 128:(h + 1) * 128] + upd * bdm_ref[...]

    def body(i, carry):
        block(pl.multiple_of(i * C, C), lf_ref, sf_ref, True)
        jb = jnp.where(i < N_BLK_C, N_BLK_C - 1 - i, N_BLK + N_BLK_C - 1 - i)
        block(pl.multiple_of(jb * C, C), lb_ref, sb_ref, False)
        return carry

    lax.fori_loop(0, N_BLK, body, 0)
    o = _head_rms(acc_ref[row_off:, :], nw_ref[...])
    g = g_ref[0, row_off:, :].astype(F32)
    o_ref[0] = (o * (g / (1.0 + jnp.exp(-g)))).astype(o_ref.dtype)


def gla_mixer(u, gate_w, gate_b, norm_w, with_ctx):
    row_off = 0 if with_ctx else LC
    qc, kc = U_OFF["gla_q"] // 256, U_OFF["gla_k"] // 256
    vc, gc, ac = U_OFF["gla_v"] // 512, U_OFF["gla_g"] // 512, U_OFF["gla_a"] // 128
    wf = jnp.zeros((128, GLA_QW), F32).at[0:GLA_RANK].set(gate_w[0])
    wb = jnp.zeros((128, GLA_QW), F32).at[GLA_RANK:2 * GLA_RANK].set(gate_w[1])
    hd = np.arange(GLA_QW)[:, None] // GLA_DK == np.arange(GROUP_W)[None, :] // GLA_DV
    ind = jnp.asarray(hd, BF16)
    bdm = jnp.asarray(hd.T[:GROUP_W // 2, :GLA_QW // 2], F32)
    full = lambda *shape: pl.BlockSpec(shape, lambda b: (0,) * len(shape))
    return pl.pallas_call(
        functools.partial(_gla_kernel, row_off=row_off),
        grid=(B,),
        in_specs=[pl.BlockSpec((1, L, 256), lambda b: (b, 0, qc)),
                  pl.BlockSpec((1, L, 256), lambda b: (b, 0, kc)),
                  pl.BlockSpec((1, L, 512), lambda b: (b, 0, vc)),
                  pl.BlockSpec((1, L, 512), lambda b: (b, 0, gc)),
                  pl.BlockSpec((1, L, 128), lambda b: (b, 0, ac)),
                  full(128, GLA_QW), full(128, GLA_QW), full(2, GLA_QW), full(1, 128),
                  full(GLA_QW, GROUP_W), full(GROUP_W // 2, GLA_QW // 2)],
        out_specs=pl.BlockSpec((1, L - row_off, GROUP_W), lambda b: (b, 0, 0)),
        out_shape=jax.ShapeDtypeStruct((B, L - row_off, GROUP_W), BF16),
        scratch_shapes=[pltpu.VMEM((L, GLA_QW), F32), pltpu.VMEM((L, GLA_QW), F32), pltpu.VMEM((L, GROUP_W), F32),
                        pltpu.VMEM((2, GROUP_W // 2, GLA_QW // 2), F32),
                        pltpu.VMEM((2, GROUP_W // 2, GLA_QW // 2), F32)],
        compiler_params=_cparams(("parallel",)),
    )(u, u, u, u, u, wf, wb, gate_b, jnp.tile(norm_w, 2)[None], ind, bdm)


HY_CT = 256
HY_TK = 512


@functools.lru_cache(maxsize=None)
def _dft_consts(Lh):
    k = np.arange(Lh, dtype=np.int64)
    ph = (np.outer(k, k) % (2 * Lh)).astype(np.float64) * (np.pi / Lh)
    sgn = (1.0 - 2.0 * (k % 2)).astype(np.float32)[:, None]
    return np.cos(ph).astype(np.float32), np.sin(ph).astype(np.float32), sgn


def _hyena_features(Lh):
    t = jnp.linspace(0.0, 1.0, Lh, dtype=F32)[:, None]
    w = (2.0 * math.pi / Lh) * jnp.arange(Lh, dtype=F32)[:, None]
    bands = jnp.linspace(1e-4, HY_BANDS - 1, HY_BANDS, dtype=F32)
    z = jnp.concatenate([t, jnp.cos(w * bands), -jnp.sin(w * bands)], axis=-1)
    return jnp.pad(z, ((0, 0), (0, 128 - z.shape[1])))


def _hy_filter_kernel(z_ref, w1_ref, b1_ref, w2_ref, b2_ref, fr_ref, w3f_ref, w3b_ref, dl_ref, c_ref, s_ref, sgn_ref,
                      hc_ref, hs_ref, hn_ref, *, Lh):
    hi = lax.Precision.HIGHEST
    z = z_ref[...]
    h = jnp.sin(fr_ref[0:1, :] * (jnp.dot(z, w1_ref[...], preferred_element_type=F32, precision=hi) + b1_ref[...]))
    h = jnp.sin(fr_ref[1:2, :] * (jnp.dot(h, w2_ref[...], preferred_element_type=F32, precision=hi) + b2_ref[...]))
    dec = jnp.exp(-(z[:, 0:1] * dl_ref[...]))
    hf = jnp.dot(h, w3f_ref[...], preferred_element_type=F32, precision=hi) * dec
    hb = jnp.dot(h, w3b_ref[...], preferred_element_type=F32, precision=hi) * dec
    nrm = jnp.sum(jnp.abs(hf), axis=0, keepdims=True) + jnp.sum(jnp.abs(hb), axis=0, keepdims=True)
    row = lax.broadcasted_iota(jnp.int32, (Lh, 1), 0)
    hf = hf / nrm
    hb = jnp.where(row == 0, 0.0, hb / nrm)
    wk = jnp.where(row == 0, 0.5 / Lh, 1.0 / Lh)

    def project(m_ref, x):
        xh = x.astype(BF16)
        xl = (x - xh.astype(F32)).astype(BF16)
        return (jnp.dot(m_ref[...], xh, preferred_element_type=F32)
                + jnp.dot(m_ref[...], xl, preferred_element_type=F32))

    am = hf + hb
    hc_ref[0] = project(c_ref, am) * wk
    hs_ref[0] = project(s_ref, hf - hb) * wk
    hn_ref[0] = jnp.sum(am * sgn_ref[...], axis=0, keepdims=True) * (0.5 / Lh)


def hyena_spectrum(Lh, w1, b1, w2, b2, w3, fr, cmat, smat, sgn):
    z = _hyena_features(Lh)
    w1p = jnp.pad(w1, ((0, 128 - w1.shape[0]), (0, 0)))
    deltas = jnp.linspace(HY_MIN_DECAY, HY_MAX_DECAY, HY_CH, dtype=F32)[None]
    nct = HY_CH // HY_CT
    full = lambda *shape: pl.BlockSpec(shape, lambda o, c: (0,) * len(shape))
    return pl.pallas_call(
        functools.partial(_hy_filter_kernel, Lh=Lh),
        grid=(HY_ORDER, nct),
        in_specs=[full(Lh, 128), full(128, 64), full(1, 64), full(64, 64), full(1, 64), full(2, 64),
                  pl.BlockSpec((64, HY_CT), lambda o, c: (0, o * 2 * nct + c)),
                  pl.BlockSpec((64, HY_CT), lambda o, c: (0, o * 2 * nct + nct + c)),
                  pl.BlockSpec((1, HY_CT), lambda o, c: (0, c)),
                  full(Lh, Lh), full(Lh, Lh), full(Lh, 1)],
        out_specs=[pl.BlockSpec((1, Lh, HY_CT), lambda o, c: (o, 0, c)),
                   pl.BlockSpec((1, Lh, HY_CT), lambda o, c: (o, 0, c)),
                   pl.BlockSpec((1, 1, HY_CT), lambda o, c: (o, 0, c))],
        out_shape=[jax.ShapeDtypeStruct((HY_ORDER, Lh, HY_CH), F32),
                   jax.ShapeDtypeStruct((HY_ORDER, Lh, HY_CH), F32),
                   jax.ShapeDtypeStruct((HY_ORDER, 1, HY_CH), F32)],
        compiler_params=_cparams(("arbitrary", "arbitrary")),
    )(z, w1p, b1[None], w2, b2[None], fr, w3, w3, deltas, cmat, smat, sgn)


def _hyena_seq(row0, Lh, x1_ref, x2_ref, y_ref, cw_refs, cb_refs, bias_ref, c_ref, s_ref, hc_ref, hs_ref, hn_ref, *,
               scratch):
    r = lax.broadcasted_iota(jnp.int32, (Lh, 1), 0)

    def sconv(u_ref, w_ref, b_ref):
        u = u_ref[0, row0:row0 + Lh, :].astype(F32)
        up = jnp.where(r == 0, 0.0, pltpu.roll(u, 1, 0))
        dn = jnp.where(r == Lh - 1, 0.0, pltpu.roll(u, Lh - 1, 0))
        return up * w_ref[0:1, :] + u * w_ref[1:2, :] + dn * w_ref[2:3, :] + b_ref[...]

    gate_refs = (x1_ref, x2_ref)
    ys_ref, yb_ref, cv_ref = scratch
    rows = slice(0, Lh)
    ys_ref[rows, :] = sconv(y_ref, cw_refs[2], cb_refs[2])
    sgn = (1 - 2 * (r & 1)).astype(F32)
    tk = min(HY_TK, Lh)
    for o in range(HY_ORDER):
        y = ys_ref[rows, :]
        yb_ref[rows, :] = y.astype(BF16)
        cv_ref[rows, :] = sgn * (jnp.sum(y * sgn, axis=0, keepdims=True) * hn_ref[o])
        for m in range(Lh // tk):
            fs = slice(m * tk, (m + 1) * tk)
            yc = jnp.dot(c_ref[fs, :], yb_ref[rows, :], preferred_element_type=F32)
            ysn = jnp.dot(s_ref[fs, :], yb_ref[rows, :], preferred_element_type=F32)
            hc = hc_ref[o, fs, :]
            hs = hs_ref[o, fs, :]
            pc = (yc * hc - ysn * hs).astype(BF16)
            ps = (yc * hs + ysn * hc).astype(BF16)
            cv_ref[rows, :] += (jnp.dot(c_ref[:, fs], pc, preferred_element_type=F32)
                                + jnp.dot(s_ref[:, fs], ps, preferred_element_type=F32))
        ys_ref[rows, :] = (sconv(gate_refs[o], cw_refs[o], cb_refs[o])
                           * (cv_ref[rows, :] + ys_ref[rows, :] * bias_ref[o:o + 1, :]))
    return ys_ref[rows, :]


def _hyena_kernel(x1_ref, x2_ref, y_ref, w1_ref, w2_ref, w3_ref, b1_ref, b2_ref, b3_ref, bias_ref, *rest, with_ctx):
    nmat = 10 if with_ctx else 5
    o_ref = rest[nmat]
    scratch = rest[nmat + 1:]
    seqs = [(LC, S, rest[0:5])]
    if with_ctx:
        seqs.append((0, LC, rest[5:10]))
    for row0, Lh, mats in seqs:
        y = _hyena_seq(row0, Lh, x1_ref, x2_ref, y_ref, (w1_ref, w2_ref, w3_ref), (b1_ref, b2_ref, b3_ref),
                       bias_ref, *mats, scratch=scratch)
        out0 = row0 if with_ctx else 0
        o_ref[0, out0:out0 + Lh, :] = y.astype(o_ref.dtype)


def hyena_mixer(u, conv_w, conv_b, w1, b1, w2, b2, w3, fr, bias, with_ctx):
    nct = HY_CH // HY_CT
    hc0 = U_OFF["hy"] // HY_CT
    once = pl.Buffered(1)
    consts = []
    specs = []
    for Lh in ((S, LC) if with_ctx else (S,)):
        cm, sm, sgn = _dft_consts(Lh)
        cm, sm, sgn = jnp.asarray(cm, BF16), jnp.asarray(sm, BF16), jnp.asarray(sgn)
        hc, hs, hn = hyena_spectrum(Lh, w1, b1, w2, b2, w3, fr, cm, sm, sgn)
        consts += [cm, sm, hc, hs, hn]
        specs += [pl.BlockSpec((Lh, Lh), lambda c, b: (0, 0), pipeline_mode=once),
                  pl.BlockSpec((Lh, Lh), lambda c, b: (0, 0), pipeline_mode=once),
                  pl.BlockSpec((HY_ORDER, Lh, HY_CT), lambda c, b: (0, 0, c), pipeline_mode=once),
                  pl.BlockSpec((HY_ORDER, Lh, HY_CT), lambda c, b: (0, 0, c), pipeline_mode=once),
                  pl.BlockSpec((HY_ORDER, 1, HY_CT), lambda c, b: (0, 0, c), pipeline_mode=once)]
    rows_out = L if with_ctx else S
    ublk = lambda j: pl.BlockSpec((1, L, HY_CT), lambda c, b: (b, 0, hc0 + j * nct + c))
    wblk = lambda j: pl.BlockSpec((3, HY_CT), lambda c, b: (0, j * nct + c))
    bblk = lambda j: pl.BlockSpec((1, HY_CT), lambda c, b: (0, j * nct + c))
    return pl.pallas_call(
        functools.partial(_hyena_kernel, with_ctx=with_ctx),
        grid=(nct, B),
        in_specs=[ublk(0), ublk(1), ublk(2), wblk(0), wblk(1), wblk(2), bblk(0), bblk(1), bblk(2),
                  pl.BlockSpec((HY_ORDER, HY_CT), lambda c, b: (0, c))] + specs,
        out_specs=pl.BlockSpec((1, rows_out, HY_CT), lambda c, b: (b, 0, c)),
        out_shape=jax.ShapeDtypeStruct((B, rows_out, HY_CH), BF16),
        scratch_shapes=[pltpu.VMEM((S, HY_CT), F32), pltpu.VMEM((S, HY_CT), BF16), pltpu.VMEM((S, HY_CT), F32)],
        compiler_params=_cparams(("arbitrary", "arbitrary")),
    )(u, u, u, conv_w, conv_w, conv_w, conv_b[None], conv_b[None], conv_b[None], bias, *consts)


def _permute_w_in(w):
    parts = []
    for n in U_ORDER:
        o, wd = REF_COLS[n]
        p = w[:, o:o + wd]
        if wd < 128:
            p = jnp.pad(p, ((0, 0), (0, 128 - wd)))
        parts.append(p)
    parts.append(jnp.zeros((w.shape[0], U_W - U_USED), w.dtype))
    return jnp.concatenate(parts, axis=1).astype(BF16)


def kernel(x, c, ctx, c_ctx, norm1_w, norm2_w, ada_w, ada_b, w_in, w_out, gla_gate_w, gla_gate_b, gla_norm_w,
           swa_q_norm_w, swa_k_norm_w, swa_sink, hyena_conv_w, hyena_conv_b, hyena_ffn_w1, hyena_ffn_b1,
           hyena_ffn_w2, hyena_ffn_b2, hyena_ffn_w3, hyena_ffn_freq, hyena_bias, diff_q_norm_w, diff_k_norm_w,
           diff_lambda, diff_subln_w, router_w, router_bias, expert_w_gate, expert_w_up, expert_w_down):
    assert x.shape == (B, S, D) and ctx.shape == (B, LC, D)
    cc = jnp.zeros((16, D), F32).at[:B].set(c).at[B].set(c_ctx)
    mods = ada_mod(cc, ada_w, ada_b)
    cos, sin = rope_tables128()
    rw_pad = jnp.pad(router_w, ((0, 0), (0, 128 - N_EXPERTS)))
    xall = jnp.concatenate([ctx, x], axis=1)

    for l in range(DEPTH):
        with_ctx = l < DEPTH - 1
        lam_init = 0.8 - 0.6 * math.exp(-0.3 * l)
        ml = mods[l, :B].reshape(B, 1, 6 * D)
        mc = mods[l, B:B + 1]
        u = in_proj(xall, norm1_w[l][None], ml, mc, _permute_w_in(w_in[l]))
        mix_a = gla_mixer(u, gla_gate_w[l], gla_gate_b[l], gla_norm_w[l], with_ctx)
        mix_b = swa_mixer(u, swa_sink[l], swa_q_norm_w[l], swa_k_norm_w[l], cos, sin, with_ctx)
        mix_y = hyena_mixer(u, hyena_conv_w[l], hyena_conv_b[l], hyena_ffn_w1[l], hyena_ffn_b1[l],
                            hyena_ffn_w2[l], hyena_ffn_b2[l], hyena_ffn_w3[l], hyena_ffn_freq[l],
                            hyena_bias[l], with_ctx)
        mix_d = diff_mixer(u, diff_q_norm_w[l], diff_k_norm_w[l], cos, sin, diff_lambda[l], diff_subln_w[l],
                           lam_init, with_ctx)
        xall, f, logits = out_proj(xall, (mix_a, mix_b, mix_y, mix_d), w_out[l].astype(BF16), norm2_w[l][None],
                                   ml, mc, rw_pad, with_ctx)
        T = B * xall.shape[1]
        src, wts, dest, tile_e, tile_v, tile_n = route(logits.reshape(T, 128)[:, :N_EXPERTS], router_bias)
        ys = moe_experts(tile_e, tile_v, tile_n, src, f.reshape(T * TOK_SUB, 128), expert_w_gate, expert_w_up,
                         expert_w_down, l)
        xall = moe_combine(dest, wts, xall, ml, mc, ys, with_ctx)
    return xall
```

```python
import functools
import math

import numpy as np
import jax
import jax.numpy as jnp
from jax import lax
from jax.experimental import pallas as pl
from jax.experimental.pallas import tpu as pltpu

F32 = jnp.float32
BF16 = jnp.bfloat16

D = 2048
B = 8
S = 2048
LC = 256
L = LC + S
DEPTH = 2
GRID_W = 64
HEAD_DIM = 64
ROPE_THETA = 10000.0
EPS = 1e-6
GROUP_W = 512

GLA_DV = 64
GLA_DK = 32
GLA_HEADS = 8
GLA_RANK = 16
GLA_TEMP = 16.0
GLA_CHUNK = 16
SWA_HEADS = 8
SWA_KV = 2
SWA_WINDOW = 128
HY_CH = 512
HY_ORDER = 2
HY_BANDS = 16
HY_TARGET = 1e-2
HY_MIN_DECAY = math.log(1.0 / HY_TARGET) / 1.5
HY_MAX_DECAY = math.log(1.0 / HY_TARGET) / 0.3
DIFF_HEADS = 4
N_EXPERTS = 16
N_GROUPS = 4
EPG = 4
TOP_K = 2
D_EXPERT = 1024

REF_COLS = dict(gla_q=(0, 256), gla_k=(256, 256), gla_v=(512, 512), gla_g=(1024, 512), gla_a=(1536, 32),
                swa_q=(1568, 512), swa_k=(2080, 128), swa_v=(2208, 128), hy=(2336, 1536),
                diff_q=(3872, 512), diff_k=(4384, 512), diff_v=(4896, 512))
U_ORDER = ("hy", "gla_v", "gla_g", "swa_q", "diff_q", "diff_k", "diff_v", "gla_q", "gla_k", "swa_k", "swa_v", "gla_a")
U_OFF = {}
_o = 0
for _n in U_ORDER:
    U_OFF[_n] = _o
    _o += max(REF_COLS[_n][1], 128)
U_USED = _o
TN_IN = 512
U_W = -(-U_USED // TN_IN) * TN_IN

VMEM_LIMIT = 56 * 1024 * 1024


def _cparams(sem):
    return pltpu.CompilerParams(dimension_semantics=sem, vmem_limit_bytes=VMEM_LIMIT)


def _ada_kernel(c_ref, w_ref, b_ref, o_ref):
    c = c_ref[...]
    a = c / (1.0 + jnp.exp(-c))
    o_ref[0] = jnp.dot(a, w_ref[0], preferred_element_type=F32, precision=lax.Precision.HIGHEST) + b_ref[0]


def ada_mod(cc, ada_w, ada_b):
    tn = 1024
    return pl.pallas_call(
        _ada_kernel,
        grid=(DEPTH, 6 * D // tn),
        in_specs=[pl.BlockSpec((16, D), lambda l, j: (0, 0)),
                  pl.BlockSpec((1, D, tn), lambda l, j: (l, 0, j)),
                  pl.BlockSpec((1, 1, tn), lambda l, j: (l, 0, j))],
        out_specs=pl.BlockSpec((1, 16, tn), lambda l, j: (l, 0, j)),
        out_shape=jax.ShapeDtypeStruct((DEPTH, 16, 6 * D), F32),
        compiler_params=_cparams(("parallel", "parallel")),
    )(cc, ada_w, ada_b.reshape(DEPTH, 1, 6 * D))


TM_IN = 1152
CH_IN = 32


def _inproj_kernel(x_ref, nw_ref, scl_ref, shl_ref, scc_ref, shc_ref, w_ref, o_ref, h_ref):
    i = pl.program_id(1)

    @pl.when(pl.program_id(2) == 0)
    def _():
        def chunk(c, carry, sc, sh):
            r0 = pl.multiple_of(c * CH_IN, CH_IN)
            x = x_ref[0, pl.ds(r0, CH_IN), :]
            ms = jnp.mean(x * x, axis=-1, keepdims=True)
            y = x * lax.rsqrt(ms + EPS) * nw_ref[...]
            h_ref[pl.ds(r0, CH_IN), :] = (y * (1.0 + sc) + sh).astype(BF16)
            return carry

        n_ctx = jnp.where(i == 0, LC // CH_IN, 0)
        lax.fori_loop(0, n_ctx, functools.partial(chunk, sc=scc_ref[...], sh=shc_ref[...]), 0)
        lax.fori_loop(n_ctx, TM_IN // CH_IN, functools.partial(chunk, sc=scl_ref[0], sh=shl_ref[0]), 0)

    o_ref[0] = jnp.dot(h_ref[...], w_ref[...], preferred_element_type=F32).astype(o_ref.dtype)


def in_proj(xall, nw, ml, mc, w_in_p):
    return pl.pallas_call(
        _inproj_kernel,
        grid=(B, L // TM_IN, U_W // TN_IN),
        in_specs=[pl.BlockSpec((1, TM_IN, D), lambda b, i, j: (b, i, 0)),
                  pl.BlockSpec((1, D), lambda b, i, j: (0, 0)),
                  pl.BlockSpec((1, 1, D), lambda b, i, j: (b, 0, 1)),
                  pl.BlockSpec((1, 1, D), lambda b, i, j: (b, 0, 0)),
                  pl.BlockSpec((1, D), lambda b, i, j: (0, 1)),
                  pl.BlockSpec((1, D), lambda b, i, j: (0, 0)),
                  pl.BlockSpec((D, TN_IN), lambda b, i, j: (0, j))],
        out_specs=pl.BlockSpec((1, TM_IN, TN_IN), lambda b, i, j: (b, i, j)),
        out_shape=jax.ShapeDtypeStruct((B, L, U_W), BF16),
        scratch_shapes=[pltpu.VMEM((TM_IN, D), BF16)],
        compiler_params=_cparams(("parallel", "parallel", "arbitrary")),
    )(xall, nw, ml, ml, mc, mc, w_in_p)


def _norm_rope(x, w, cos, sin):
    R, W = x.shape
    lane = lax.broadcasted_iota(jnp.int32, (R, 128), 1)
    left = lane < HEAD_DIM
    first = (lane & 31) < 16
    outs = []
    for c in range(W // 128):
        xc = x[:, c * 128:(c + 1) * 128]
        sq = xc * xc
        sl = jnp.sum(jnp.where(left, sq, 0.0), axis=-1, keepdims=True)
        sr = jnp.sum(jnp.where(left, 0.0, sq), axis=-1, keepdims=True)
        inv = jnp.where(left, lax.rsqrt(sl * (1.0 / HEAD_DIM) + EPS), lax.rsqrt(sr * (1.0 / HEAD_DIM) + EPS))
        y = xc * inv * w
        rot = jnp.where(first, pltpu.roll(y, 112, 1), pltpu.roll(y, 16, 1))
        outs.append(y * cos + rot * sin)
    return outs[0] if len(outs) == 1 else jnp.concatenate(outs, axis=-1)


def rope_tables128():
    n = HEAD_DIM // 4
    rows = S // GRID_W
    row_ids = jnp.repeat(jnp.arange(rows), GRID_W).astype(F32)
    col_ids = jnp.tile(jnp.arange(GRID_W), rows).astype(F32)
    inv = ROPE_THETA ** (-jnp.arange(n, dtype=F32) / n)
    ar = row_ids[:, None] * inv
    ac = col_ids[:, None] * inv
    cos64 = jnp.concatenate([jnp.cos(ar), jnp.cos(ar), jnp.cos(ac), jnp.cos(ac)], axis=-1)
    sin64 = jnp.concatenate([-jnp.sin(ar), jnp.sin(ar), -jnp.sin(ac), jnp.sin(ac)], axis=-1)
    cos = jnp.concatenate([jnp.ones((LC, 64), F32), cos64], axis=0)
    sin = jnp.concatenate([jnp.zeros((LC, 64), F32), sin64], axis=0)
    return jnp.tile(cos, (1, 2)), jnp.tile(sin, (1, 2))


def _row_max(s):
    m = s[:, 0:128]
    for c in range(1, s.shape[1] // 128):
        m = jnp.maximum(m, s[:, c * 128:(c + 1) * 128])
    return jnp.max(m, axis=-1, keepdims=True)


def _row_sum(s):
    a = s[:, 0:128]
    for c in range(1, s.shape[1] // 128):
        a = a + s[:, c * 128:(c + 1) * 128]
    return jnp.sum(a, axis=-1, keepdims=True)


SWA_BAND = 3 * SWA_WINDOW


def _swa_kernel(sink_ref, q_ref, k_ref, v_ref, qw_ref, kw_ref, cos_ref, sin_ref, o_ref, kn_ref, *, n_off):
    n = pl.program_id(1) + n_off

    @pl.when(pl.program_id(1) == 0)
    def _():
        kn_ref[...] = _norm_rope(k_ref[0].astype(F32), kw_ref[...], cos_ref[...], sin_ref[...]).astype(BF16)

    r0 = pl.multiple_of(n * 128, 128)
    q = _norm_rope(q_ref[0].astype(F32), qw_ref[...], cos_ref[pl.ds(r0, 128), :], sin_ref[pl.ds(r0, 128), :])
    q = (q * (HEAD_DIM ** -0.5)).astype(BF16)

    nb = n - LC // 128
    is_lat = nb >= 0
    lstart = jnp.clip((nb - 1) * 128, 0, S - SWA_BAND)
    start = pl.multiple_of(LC + lstart, 128)
    kk = jnp.concatenate([kn_ref[0:LC, :], kn_ref[pl.ds(start, SWA_BAND), :]], axis=0)
    vv = jnp.concatenate([v_ref[0, 0:LC, :], v_ref[0, pl.ds(start, SWA_BAND), :]], axis=0)
    nk = LC + SWA_BAND
    row = lax.broadcasted_iota(jnp.int32, (512, nk), 0) & 127
    col = lax.broadcasted_iota(jnp.int32, (512, nk), 1)
    qpos = nb * 128 + row
    kpos = lstart + col - LC
    valid = (col < LC) | ((jnp.abs(qpos - kpos) <= SWA_WINDOW) & is_lat)
    rowi = lax.broadcasted_iota(jnp.int32, (512, 1), 0)
    outs = [None] * SWA_HEADS
    for hk in range(SWA_KV):
        qs = jnp.concatenate([q[:, (hk * 4 + g) * 64:(hk * 4 + g + 1) * 64] for g in range(4)], axis=0)
        s = lax.dot_general(qs, kk[:, hk * 64:(hk + 1) * 64], (((1,), (1,)), ((), ())),
                            preferred_element_type=F32)
        s = jnp.where(valid, s, -jnp.inf)
        sk = jnp.where(rowi < 128, sink_ref[hk * 4],
                       jnp.where(rowi < 256, sink_ref[hk * 4 + 1],
                                 jnp.where(rowi < 384, sink_ref[hk * 4 + 2], sink_ref[hk * 4 + 3])))
        m = jnp.maximum(_row_max(s), sk)
        e = jnp.exp(s - m)
        den = _row_sum(e) + jnp.exp(sk - m)
        o = jnp.dot(e.astype(BF16), vv[:, hk * 64:(hk + 1) * 64], preferred_element_type=F32) / den
        for g in range(4):
            outs[hk * 4 + g] = o[g * 128:(g + 1) * 128]
    o_ref[0] = jnp.concatenate(outs, axis=-1).astype(o_ref.dtype)


def swa_mixer(u, sink, qw, kw, cos, sin, with_ctx):
    n_off = 0 if with_ctx else LC // 128
    nblk = L // 128 - n_off
    qc, kc, vc = U_OFF["swa_q"] // 512, U_OFF["swa_k"] // 128, U_OFF["swa_v"] // 128
    return pl.pallas_call(
        functools.partial(_swa_kernel, n_off=n_off),
        grid_spec=pltpu.PrefetchScalarGridSpec(
            num_scalar_prefetch=1,
            grid=(B, nblk),
            in_specs=[pl.BlockSpec((1, 128, 512), lambda b, n, s: (b, n + n_off, qc)),
                      pl.BlockSpec((1, L, 128), lambda b, n, s: (b, 0, kc)),
                      pl.BlockSpec((1, L, 128), lambda b, n, s: (b, 0, vc)),
                      pl.BlockSpec((1, 128), lambda b, n, s: (0, 0)),
                      pl.BlockSpec((1, 128), lambda b, n, s: (0, 0)),
                      pl.BlockSpec((L, 128), lambda b, n, s: (0, 0)),
                      pl.BlockSpec((L, 128), lambda b, n, s: (0, 0))],
            out_specs=pl.BlockSpec((1, 128, 512), lambda b, n, s: (b, n, 0)),
            scratch_shapes=[pltpu.VMEM((L, 128), BF16)]),
        out_shape=jax.ShapeDtypeStruct((B, nblk * 128, GROUP_W), BF16),
        compiler_params=_cparams(("parallel", "arbitrary")),
    )(sink, u, u, u, jnp.tile(qw, 2)[None], jnp.tile(kw, 2)[None], cos, sin)


TQ_DIFF = 256
KC_DIFF = 768


def _diff_kernel(q_ref, k_ref, v_ref, qw_ref, kw_ref, cos_ref, sin_ref, dl_ref, sw_ref, o_ref, kn_ref, vx_ref,
                 *, j_off, lam_init):
    j = pl.program_id(2) + j_off

    @pl.when(pl.program_id(2) == 0)
    def _():
        kn_ref[...] = _norm_rope(k_ref[0].astype(F32), kw_ref[...], cos_ref[...], sin_ref[...]).astype(BF16)
        vx_ref[:, 0:128] = v_ref[0]
        vx_ref[:, 128:256] = jnp.ones((L, 128), BF16)

    dl = dl_ref[...]
    lam = (jnp.exp(jnp.sum(dl[0:1] * dl[1:2], axis=-1, keepdims=True))
           - jnp.exp(jnp.sum(dl[2:3] * dl[3:4], axis=-1, keepdims=True)) + lam_init)
    r0 = pl.multiple_of(j * TQ_DIFF, TQ_DIFF)
    q = _norm_rope(q_ref[0].astype(F32), qw_ref[...], cos_ref[pl.ds(r0, TQ_DIFF), :], sin_ref[pl.ds(r0, TQ_DIFF), :])
    q = q * (HEAD_DIM ** -0.5)
    lane = lax.broadcasted_iota(jnp.int32, (TQ_DIFF, 128), 1)
    qms = [jnp.where(lane < 64, q, 0.0).astype(BF16), jnp.where(lane < 64, 0.0, q).astype(BF16)]

    def attend(nk):
        os_ = []
        kc = min(KC_DIFF, nk)
        for m in range(2):
            mrun = acc = None
            for c0 in range(0, nk, kc):
                s = lax.dot_general(qms[m], kn_ref[c0:c0 + kc, :], (((1,), (1,)), ((), ())),
                                    preferred_element_type=F32)
                mnew = _row_max(s) if mrun is None else jnp.maximum(mrun, _row_max(s))
                e = jnp.exp((s - mnew).astype(BF16))
                oe = jnp.dot(e, vx_ref[c0:c0 + kc, :], preferred_element_type=F32)
                acc = oe if acc is None else acc * jnp.exp(mrun - mnew) + oe
                mrun = mnew
            os_.append(acc[:, 0:128] / acc[:, 128:129])
        o = os_[0] - lam * os_[1]
        ms = jnp.mean(o * o, axis=-1, keepdims=True)
        o = o * lax.rsqrt(ms + EPS) * sw_ref[...] * (1.0 - lam_init)
        o_ref[0] = o.astype(o_ref.dtype)

    if j_off == 0:
        @pl.when(j == 0)
        def _():
            attend(LC)

    @pl.when(j > 0)
    def _():
        attend(L)


def diff_mixer(u, qw, kw, cos, sin, dlam, subw, lam_init, with_ctx):
    j_off = 0 if with_ctx else LC // TQ_DIFF
    nblk = L // TQ_DIFF - j_off
    qc, kc, vc = U_OFF["diff_q"] // 128, U_OFF["diff_k"] // 128, U_OFF["diff_v"] // 128
    return pl.pallas_call(
        functools.partial(_diff_kernel, j_off=j_off, lam_init=lam_init),
        grid=(B, DIFF_HEADS, nblk),
        in_specs=[pl.BlockSpec((1, TQ_DIFF, 128), lambda b, h, j: (b, j + j_off, qc + h)),
                  pl.BlockSpec((1, L, 128), lambda b, h, j: (b, 0, kc + h)),
                  pl.BlockSpec((1, L, 128), lambda b, h, j: (b, 0, vc + h)),
                  pl.BlockSpec((1, 128), lambda b, h, j: (0, 0)),
                  pl.BlockSpec((1, 128), lambda b, h, j: (0, 0)),
                  pl.BlockSpec((L, 128), lambda b, h, j: (0, 0)),
                  pl.BlockSpec((L, 128), lambda b, h, j: (0, 0)),
                  pl.BlockSpec((4, 64), lambda b, h, j: (0, 0)),
                  pl.BlockSpec((1, 128), lambda b, h, j: (0, 0))],
        out_specs=pl.BlockSpec((1, TQ_DIFF, 128), lambda b, h, j: (b, j, h)),
        out_shape=jax.ShapeDtypeStruct((B, nblk * TQ_DIFF, GROUP_W), BF16),
        scratch_shapes=[pltpu.VMEM((L, 128), BF16), pltpu.VMEM((L, 256), BF16)],
        compiler_params=_cparams(("parallel", "parallel", "arbitrary")),
    )(u, u, u, jnp.tile(qw, 2)[None], jnp.tile(kw, 2)[None], cos, sin, dlam, subw[None])


TM_OUT = 256
TOK_SUB = D // 128


def _store_token_rows(ref, val):
    R = val.shape[0]
    for s in range(TOK_SUB):
        ref[pl.ds(s, R, stride=TOK_SUB), :] = val[:, s * 128:(s + 1) * 128]


def _load_token_rows(ref, R):
    return jnp.concatenate([ref[pl.ds(s, R, stride=TOK_SUB), :] for s in range(TOK_SUB)], axis=-1)


def _outproj_kernel(x_ref, a_ref, b_ref, y_ref, d_ref, w_ref, nw_ref, ml_ref, mc_ref, rw_ref,
                    xo_ref, f_ref, lg_ref, *, i_off):
    is_ctx = (pl.program_id(1) + i_off) == 0

    def mod(k):
        return jnp.where(is_ctx, mc_ref[:, k * D:(k + 1) * D], ml_ref[0, :, k * D:(k + 1) * D])

    acc = jnp.dot(a_ref[0], w_ref[0:512, :], preferred_element_type=F32)
    acc += jnp.dot(b_ref[0], w_ref[512:1024, :], preferred_element_type=F32)
    acc += jnp.dot(y_ref[0], w_ref[1024:1536, :], preferred_element_type=F32)
    acc += jnp.dot(d_ref[0], w_ref[1536:2048, :], preferred_element_type=F32)
    xn = x_ref[0] + mod(2) * acc
    xo_ref[0] = xn
    ms = jnp.mean(xn * xn, axis=-1, keepdims=True)
    f = (xn * lax.rsqrt(ms + EPS) * nw_ref[...]) * (1.0 + mod(4)) + mod(3)
    _store_token_rows(f_ref.at[0], f)
    fh = f.astype(BF16)
    fl = (f - fh.astype(F32)).astype(BF16)
    ph = jnp.dot(fh, rw_ref[...], preferred_element_type=F32)
    pw = jnp.dot(fl, rw_ref[...], preferred_element_type=F32)
    lg_ref[0] = ph[:, 0:128] + ph[:, 128:256] + pw[:, 0:128]


def out_proj(xall, mixes, w_out_b, nw, ml, mc, rw_pad, with_ctx):
    i_off = 0 if with_ctx else LC // TM_OUT
    nblk = L // TM_OUT - i_off
    rows = lambda b, i: (b, i + i_off, 0)
    rwh = rw_pad.astype(BF16)
    rw2 = jnp.concatenate([rwh, (rw_pad - rwh.astype(F32)).astype(BF16)], axis=1)
    return pl.pallas_call(
        functools.partial(_outproj_kernel, i_off=i_off),
        grid=(B, nblk),
        in_specs=[pl.BlockSpec((1, TM_OUT, D), rows)]
                 + [pl.BlockSpec((1, TM_OUT, GROUP_W), lambda b, i: (b, i, 0))] * 4
                 + [pl.BlockSpec((D, D), lambda b, i: (0, 0)),
                    pl.BlockSpec((1, D), lambda b, i: (0, 0)),
                    pl.BlockSpec((1, 1, 6 * D), lambda b, i: (b, 0, 0)),
                    pl.BlockSpec((1, 6 * D), lambda b, i: (0, 0)),
                    pl.BlockSpec((D, 256), lambda b, i: (0, 0))],
        out_specs=[pl.BlockSpec((1, TM_OUT, D), lambda b, i: (b, i, 0)),
                   pl.BlockSpec((1, TM_OUT * TOK_SUB, 128), lambda b, i: (b, i, 0)),
                   pl.BlockSpec((1, TM_OUT, 128), lambda b, i: (b, i, 0))],
        out_shape=[jax.ShapeDtypeStruct((B, nblk * TM_OUT, D), F32),
                   jax.ShapeDtypeStruct((B, nblk * TM_OUT * TOK_SUB, 128), F32),
                   jax.ShapeDtypeStruct((B, nblk * TM_OUT, 128), F32)],
        compiler_params=_cparams(("parallel", "parallel")),
    )(xall, *mixes, w_out_b, nw, ml, mc, rw2)


TM_MOE = 256


def _token_copy(src_hbm, idx, buf, r, sem):
    return pltpu.make_async_copy(src_hbm.at[pl.ds(idx * TOK_SUB, TOK_SUB)], buf.at[pl.ds(r * TOK_SUB, TOK_SUB)], sem)


def _moe_kernel(te_ref, tv_ref, tn_ref, src_ref, f_hbm, wg_hbm, wu_hbm, wd_hbm, o_ref, gbuf, gsem, sg, su, sd,
                wg_b, wu_b, wd_b, sem, *, layer):
    t = pl.program_id(0)
    e = te_ref[t]

    def issue(step, slot):
        def body(r, carry):
            _token_copy(f_hbm, src_ref[step * TM_MOE + r], gbuf.at[slot], r, gsem.at[slot]).start()
            return carry

        lax.fori_loop(0, TM_MOE, body, 0, unroll=8)

    @pl.when(t == 0)
    def _():
        issue(0, 0)

    @pl.when(t + 1 < pl.num_programs(0))
    def _():
        issue(t + 1, (t + 1) % 2)

    def copies(ex):
        return (pltpu.make_async_copy(wg_hbm.at[layer, ex], sg, sem.at[0]),
                pltpu.make_async_copy(wu_hbm.at[layer, ex], su, sem.at[1]),
                pltpu.make_async_copy(wd_hbm.at[layer, ex], sd, sem.at[2]))

    @pl.when(t == 0)
    def _():
        for cp in copies(e):
            cp.start()

    @pl.when((t == 0) | (e != te_ref[jnp.maximum(t - 1, 0)]))
    def _():
        for cp in copies(e):
            cp.wait()
        wg_b[...] = sg[...].astype(BF16)
        wu_b[...] = su[...].astype(BF16)
        wd_b[...] = sd[...].astype(BF16)
        nxt = tn_ref[t]

        @pl.when(nxt >= 0)
        def _():
            for cp in copies(nxt):
                cp.start()

    slot = t % 2

    def drain(r, carry):
        _token_copy(f_hbm, 0, gbuf.at[slot], r, gsem.at[slot]).wait()
        return carry

    lax.fori_loop(0, TM_MOE, drain, 0, unroll=8)

    @pl.when(tv_ref[t] == 0)
    def _():
        o_ref[...] = jnp.zeros_like(o_ref)

    @pl.when(tv_ref[t] > 0)
    def _():
        x = _load_token_rows(gbuf.at[slot], TM_MOE).astype(BF16)
        g = jnp.dot(x, wg_b[...], preferred_element_type=F32)
        u = jnp.dot(x, wu_b[...], preferred_element_type=F32)
        h = (g / (1.0 + jnp.exp(-g)) * u).astype(BF16)
        _store_token_rows(o_ref, jnp.dot(h, wd_b[...], preferred_element_type=F32))


def moe_experts(tile_e, tile_v, tile_n, src, f_tok, wg, wu, wd, layer):
    nrows = src.shape[0]
    return pl.pallas_call(
        functools.partial(_moe_kernel, layer=layer),
        grid_spec=pltpu.PrefetchScalarGridSpec(
            num_scalar_prefetch=4,
            grid=(nrows // TM_MOE,),
            in_specs=[pl.BlockSpec(memory_space=pl.ANY)] * 4,
            out_specs=pl.BlockSpec((TM_MOE * TOK_SUB, 128), lambda t, te, tv, tn, sr: (t, 0)),
            scratch_shapes=[pltpu.VMEM((2, TM_MOE * TOK_SUB, 128), F32), pltpu.SemaphoreType.DMA((2,)),
                            pltpu.VMEM((D, D_EXPERT), F32), pltpu.VMEM((D, D_EXPERT), F32),
                            pltpu.VMEM((D_EXPERT, D), F32),
                            pltpu.VMEM((D, D_EXPERT), BF16), pltpu.VMEM((D, D_EXPERT), BF16),
                            pltpu.VMEM((D_EXPERT, D), BF16),
                            pltpu.SemaphoreType.DMA((3,))]),
        out_shape=jax.ShapeDtypeStruct((nrows * TOK_SUB, 128), F32),
        compiler_params=_cparams(("arbitrary",)),
    )(tile_e, tile_v, tile_n, src, f_tok, wg, wu, wd)


R_COMB = 256


def _combine_kernel(pos_ref, x_ref, w_ref, ml_ref, mc_ref, y_hbm, o_ref, buf, sem, *, i_off):
    nblk = pl.num_programs(1)
    step = pl.program_id(0) * nblk + pl.program_id(1)
    is_ctx = (pl.program_id(1) + i_off) == 0

    def issue(st, slot):
        base = st * (R_COMB * TOP_K)

        def body(r, carry):
            for k in range(TOP_K):
                _token_copy(y_hbm, pos_ref[base + TOP_K * r + k], buf.at[slot, k], r, sem.at[slot, k]).start()
            return carry

        lax.fori_loop(0, R_COMB, body, 0, unroll=4)

    @pl.when(step == 0)
    def _():
        issue(0, 0)

    @pl.when(step + 1 < pl.num_programs(0) * nblk)
    def _():
        issue(step + 1, (step + 1) % 2)

    slot = step % 2

    def drain(r, carry):
        for k in range(TOP_K):
            _token_copy(y_hbm, 0, buf.at[slot, k], r, sem.at[slot, k]).wait()
        return carry

    lax.fori_loop(0, R_COMB, drain, 0, unroll=4)
    gf = jnp.where(is_ctx, mc_ref[:, 5 * D:6 * D], ml_ref[0, :, 5 * D:6 * D])
    w0 = w_ref[0, :, 0:1]
    w1 = w_ref[0, :, 1:2]
    for s in range(TOK_SUB):
        cs = slice(s * 128, (s + 1) * 128)
        y = (w0 * buf[slot, 0, pl.ds(s, R_COMB, stride=TOK_SUB), :]
             + w1 * buf[slot, 1, pl.ds(s, R_COMB, stride=TOK_SUB), :])
        o_ref[0, :, cs] = x_ref[0, :, cs] + gf[:, cs] * y


def moe_combine(pos, wts, xall, ml, mc, y_sorted, with_ctx):
    i_off = 0 if with_ctx else LC // R_COMB
    rows_out = xall.shape[1]
    nblk = rows_out // R_COMB
    return pl.pallas_call(
        functools.partial(_combine_kernel, i_off=i_off),
        grid_spec=pltpu.PrefetchScalarGridSpec(
            num_scalar_prefetch=1,
            grid=(B, nblk),
            in_specs=[pl.BlockSpec((1, R_COMB, D), lambda b, i, p: (b, i, 0)),
                      pl.BlockSpec((1, R_COMB, TOP_K), lambda b, i, p: (b, i, 0)),
                      pl.BlockSpec((1, 1, 6 * D), lambda b, i, p: (b, 0, 0)),
                      pl.BlockSpec((1, 6 * D), lambda b, i, p: (0, 0)),
                      pl.BlockSpec(memory_space=pl.ANY)],
            out_specs=pl.BlockSpec((1, R_COMB, D), lambda b, i, p: (b, i, 0)),
            scratch_shapes=[pltpu.VMEM((2, TOP_K, R_COMB * TOK_SUB, 128), F32),
                            pltpu.SemaphoreType.DMA((2, TOP_K))]),
        out_shape=jax.ShapeDtypeStruct((B, rows_out, D), F32),
        compiler_params=_cparams(("arbitrary", "arbitrary")),
    )(pos, xall, wts.reshape(B, rows_out, TOP_K), ml, mc, y_sorted)


CS_BLK = 256


def _token_cumsum(hot):
    E, T = hot.shape
    nb = T // CS_BLK
    tri = (jnp.arange(CS_BLK)[:, None] <= jnp.arange(CS_BLK)[None, :]).astype(F32)
    inner = jnp.einsum("ebk,kj->ebj", hot.reshape(E, nb, CS_BLK).astype(F32), tri,
                       precision=lax.Precision.HIGHEST)
    offs = jnp.cumsum(inner[:, :, -1], axis=1) - inner[:, :, -1]
    return (inner + offs[:, :, None]).astype(jnp.int32).reshape(E, T)


def route(logits, router_bias):
    T = logits.shape[0]
    lt = logits.T
    ex = jnp.exp(lt - jnp.max(lt, axis=0, keepdims=True))
    probs = ex / jnp.sum(ex, axis=0, keepdims=True)
    sel = (probs + router_bias.astype(F32)[:, None]).reshape(N_GROUPS, EPG, T)
    group = jnp.argmax(jnp.max(sel, axis=1), axis=0).astype(jnp.int32)
    sel_g = jnp.take_along_axis(sel, group[None, None, :], axis=0)[0]
    wid = jnp.arange(EPG, dtype=jnp.int32)[:, None]
    i0 = jnp.argmax(sel_g, axis=0).astype(jnp.int32)
    i1 = jnp.argmax(jnp.where(wid == i0[None, :], -jnp.inf, sel_g), axis=0).astype(jnp.int32)
    e0 = group * EPG + i0
    e1 = group * EPG + i1
    p0 = jnp.take_along_axis(probs, e0[None, :], axis=0)[0]
    p1 = jnp.take_along_axis(probs, e1[None, :], axis=0)[0]
    wts = jnp.stack([p0 / (p0 + p1), p1 / (p0 + p1)], axis=1)
    eid = jnp.arange(N_EXPERTS, dtype=jnp.int32)
    hot = (eid[:, None] == e0[None, :]).astype(jnp.int32) + (eid[:, None] == e1[None, :]).astype(jnp.int32)
    csum = _token_cumsum(hot)
    before = csum - hot
    r0 = jnp.take_along_axis(before, e0[None, :], axis=0)[0]
    r1 = jnp.take_along_axis(before, e1[None, :], axis=0)[0]
    counts = csum[:, -1]
    ptiles = (counts + TM_MOE - 1) // TM_MOE
    tend = jnp.cumsum(ptiles)
    tstart = tend - ptiles
    dest = jnp.stack([tstart[e0] * TM_MOE + r0, tstart[e1] * TM_MOE + r1], axis=1).reshape(-1).astype(jnp.int32)
    ntiles = -(-(TOP_K * T) // TM_MOE) + N_EXPERTS
    nrows = ntiles * TM_MOE
    tid = jnp.arange(ntiles, dtype=jnp.int32)
    tile_v = (tid < tend[-1]).astype(jnp.int32)
    tile_e = jnp.sum(tend[None, :] <= jnp.minimum(tid, tend[-1] - 1)[:, None], axis=1).astype(jnp.int32)
    eid = jnp.arange(N_EXPERTS, dtype=jnp.int32)
    later = (eid[None, :] > eid[:, None]) & (ptiles[None, :] > 0)
    nxt = jnp.min(jnp.where(later, eid[None, :], N_EXPERTS), axis=1)
    tile_n = jnp.where(nxt < N_EXPERTS, nxt, -1)[tile_e].astype(jnp.int32)
    src = jnp.zeros((nrows,), jnp.int32).at[dest].set(jnp.repeat(jnp.arange(T, dtype=jnp.int32), TOP_K),
                                                      unique_indices=True)
    return src, wts, dest, tile_e, tile_v, tile_n


GLA_QW = GLA_HEADS * GLA_DK
N_BLK = L // GLA_CHUNK
N_BLK_C = LC // GLA_CHUNK


def _log_sigmoid(z):
    return jnp.minimum(z, 0.0) - jnp.log1p(jnp.exp(-jnp.abs(z)))


def _head_rms(x, w):
    R, W = x.shape
    lane = lax.broadcasted_iota(jnp.int32, (R, 128), 1)
    left = lane < 64
    outs = []
    for c in range(W // 128):
        xc = x[:, c * 128:(c + 1) * 128]
        sq = xc * xc
        sl = jnp.sum(jnp.where(left, sq, 0.0), axis=-1, keepdims=True)
        sr = jnp.sum(jnp.where(left, 0.0, sq), axis=-1, keepdims=True)
        inv = jnp.where(left, lax.rsqrt(sl * (1.0 / 64) + EPS), lax.rsqrt(sr * (1.0 / 64) + EPS))
        outs.append(xc * inv * w)
    return jnp.concatenate(outs, axis=-1)


def _gla_kernel(q_ref, k_ref, v_ref, g_ref, a_ref, wf_ref, wb_ref, gb_ref, nw_ref, ind_ref, bdm_ref, o_ref,
                lf_ref, lb_ref, acc_ref, sf_ref, sb_ref, *, row_off):
    hi = lax.Precision.HIGHEST
    C = GLA_CHUNK
    a = a_ref[0].astype(F32)
    rin = lax.broadcasted_iota(jnp.int32, (L, 1), 0) & (C - 1)

    def block_cumsum(x, forward):
        for sh in (1, 2, 4, 8):
            if forward:
                x = x + jnp.where(rin >= sh, pltpu.roll(x, sh, 0), 0.0)
            else:
                x = x + jnp.where(rin < C - sh, pltpu.roll(x, L - sh, 0), 0.0)
        return x

    lf_ref[...] = block_cumsum(_log_sigmoid(jnp.dot(a, wf_ref[...], preferred_element_type=F32, precision=hi)
                                            + gb_ref[0:1, :]) * (1.0 / GLA_TEMP), True)
    lb_ref[...] = block_cumsum(_log_sigmoid(jnp.dot(a, wb_ref[...], preferred_element_type=F32, precision=hi)
                                            + gb_ref[1:2, :]) * (1.0 / GLA_TEMP), False)
    acc_ref[...] = jnp.zeros_like(acc_ref)
    sf_ref[...] = jnp.zeros_like(sf_ref)
    sb_ref[...] = jnp.zeros_like(sb_ref)

    rowi = lax.broadcasted_iota(jnp.int32, (C, GLA_QW), 0)

    def block(r0, b_ref, s_ref, forward):
        q = q_ref[0, pl.ds(r0, C), :].astype(F32) * (GLA_DK ** -0.5)
        k = k_ref[0, pl.ds(r0, C), :].astype(F32)
        v = v_ref[0, pl.ds(r0, C), :].astype(F32)
        b = b_ref[pl.ds(r0, C), :]
        btot = b[C - 1:C, :] if forward else b[0:1, :]
        qe = (q * jnp.exp(b)).astype(BF16)
        o = jnp.concatenate(
            [lax.dot_general(qe[:, h * 128:(h + 1) * 128], s_ref[h].astype(BF16), (((1,), (1,)), ((), ())),
                             preferred_element_type=F32) for h in range(2)], axis=-1)
        ps = []
        for s in range(C):
            seen = (rowi >= s) if forward else (rowi <= s)
            e = jnp.exp(jnp.where(seen, b - b[s:s + 1, :], -jnp.inf))
            ps.append(q * k[s:s + 1, :] * e)
        pm = jnp.concatenate(ps, axis=0).astype(BF16)
        rm = jnp.dot(pm, ind_ref[...], preferred_element_type=F32)
        for s in range(C):
            o = o + rm[s * C:(s + 1) * C, :] * v[s:s + 1, :]
        acc_ref[pl.ds(r0, C), :] += o
        kd = (k * jnp.exp(btot - b)).astype(BF16)
        vb = v.astype(BF16)
        dec = jnp.exp(btot)
        for h in range(2):
            upd = lax.dot_general(vb[:, h * 256:(h + 1) * 256], kd[:, h * 128:(h + 1) * 128],
                                  (((0,), (0,)), ((), ())), preferred_element_type=F32)
            s_ref[h] = s_ref[h] * dec[:, h * 128:(h + 1) * 128] + upd * bdm_ref[...]

    def body(i, carry):
        block(pl.multiple_of(i * C, C), lf_ref, sf_ref, True)
        jb = jnp.where(i < N_BLK_C, N_BLK_C - 1 - i, N_BLK + N_BLK_C - 1 - i)
        block(pl.multiple_of(jb * C, C), lb_ref, sb_ref, False)
        return carry

    lax.fori_loop(0, N_BLK, body, 0)
    o = _head_rms(acc_ref[row_off:, :], nw_ref[...])
    g = g_ref[0, row_off:, :].astype(F32)
    o_ref[0] = (o * (g / (1.0 + jnp.exp(-g)))).astype(o_ref.dtype)


def gla_mixer(u, gate_w, gate_b, norm_w, with_ctx):
    row_off = 0 if with_ctx else LC
    qc, kc = U_OFF["gla_q"] // 256, U_OFF["gla_k"] // 256
    vc, gc, ac = U_OFF["gla_v"] // 512, U_OFF["gla_g"] // 512, U_OFF["gla_a"] // 128
    wf = jnp.zeros((128, GLA_QW), F32).at[0:GLA_RANK].set(gate_w[0])
    wb = jnp.zeros((128, GLA_QW), F32).at[GLA_RANK:2 * GLA_RANK].set(gate_w[1])
    hd = np.arange(GLA_QW)[:, None] // GLA_DK == np.arange(GROUP_W)[None, :] // GLA_DV
    ind = jnp.asarray(hd, BF16)
    bdm = jnp.asarray(hd.T[:GROUP_W // 2, :GLA_QW // 2], F32)
    full = lambda *shape: pl.BlockSpec(shape, lambda b: (0,) * len(shape))
    return pl.pallas_call(
        functools.partial(_gla_kernel, row_off=row_off),
        grid=(B,),
        in_specs=[pl.BlockSpec((1, L, 256), lambda b: (b, 0, qc)),
                  pl.BlockSpec((1, L, 256), lambda b: (b, 0, kc)),
                  pl.BlockSpec((1, L, 512), lambda b: (b, 0, vc)),
                  pl.BlockSpec((1, L, 512), lambda b: (b, 0, gc)),
                  pl.BlockSpec((1, L, 128), lambda b: (b, 0, ac)),
                  full(128, GLA_QW), full(128, GLA_QW), full(2, GLA_QW), full(1, 128),
                  full(GLA_QW, GROUP_W), full(GROUP_W // 2, GLA_QW // 2)],
        out_specs=pl.BlockSpec((1, L - row_off, GROUP_W), lambda b: (b, 0, 0)),
        out_shape=jax.ShapeDtypeStruct((B, L - row_off, GROUP_W), BF16),
        scratch_shapes=[pltpu.VMEM((L, GLA_QW), F32), pltpu.VMEM((L, GLA_QW), F32), pltpu.VMEM((L, GROUP_W), F32),
                        pltpu.VMEM((2, GROUP_W // 2, GLA_QW // 2), F32),
                        pltpu.VMEM((2, GROUP_W // 2, GLA_QW // 2), F32)],
        compiler_params=_cparams(("parallel",)),
    )(u, u, u, u, u, wf, wb, gate_b, jnp.tile(norm_w, 2)[None], ind, bdm)


HY_CT = 256
HY_TK = 512


@functools.lru_cache(maxsize=None)
def _dft_consts(Lh):
    k = np.arange(Lh, dtype=np.int64)
    ph = (np.outer(k, k) % (2 * Lh)).astype(np.float64) * (np.pi / Lh)
    sgn = (1.0 - 2.0 * (k % 2)).astype(np.float32)[:, None]
    return np.cos(ph).astype(np.float32), np.sin(ph).astype(np.float32), sgn


def _hyena_features(Lh):
    t = jnp.linspace(0.0, 1.0, Lh, dtype=F32)[:, None]
    w = (2.0 * math.pi / Lh) * jnp.arange(Lh, dtype=F32)[:, None]
    bands = jnp.linspace(1e-4, HY_BANDS - 1, HY_BANDS, dtype=F32)
    z = jnp.concatenate([t, jnp.cos(w * bands), -jnp.sin(w * bands)], axis=-1)
    return jnp.pad(z, ((0, 0), (0, 128 - z.shape[1])))


def _hy_filter_kernel(z_ref, w1_ref, b1_ref, w2_ref, b2_ref, fr_ref, w3f_ref, w3b_ref, dl_ref, c_ref, s_ref, sgn_ref,
                      hc_ref, hs_ref, hn_ref, *, Lh):
    hi = lax.Precision.HIGHEST
    z = z_ref[...]
    h = jnp.sin(fr_ref[0:1, :] * (jnp.dot(z, w1_ref[...], preferred_element_type=F32, precision=hi) + b1_ref[...]))
    h = jnp.sin(fr_ref[1:2, :] * (jnp.dot(h, w2_ref[...], preferred_element_type=F32, precision=hi) + b2_ref[...]))
    dec = jnp.exp(-(z[:, 0:1] * dl_ref[...]))
    hf = jnp.dot(h, w3f_ref[...], preferred_element_type=F32, precision=hi) * dec
    hb = jnp.dot(h, w3b_ref[...], preferred_element_type=F32, precision=hi) * dec
    nrm = jnp.sum(jnp.abs(hf), axis=0, keepdims=True) + jnp.sum(jnp.abs(hb), axis=0, keepdims=True)
    row = lax.broadcasted_iota(jnp.int32, (Lh, 1), 0)
    hf = hf / nrm
    hb = jnp.where(row == 0, 0.0, hb / nrm)
    wk = jnp.where(row == 0, 0.5 / Lh, 1.0 / Lh)

    def project(m_ref, x):
        xh = x.astype(BF16)
        xl = (x - xh.astype(F32)).astype(BF16)
        return (jnp.dot(m_ref[...], xh, preferred_element_type=F32)
                + jnp.dot(m_ref[...], xl, preferred_element_type=F32))

    am = hf + hb
    hc_ref[0] = project(c_ref, am) * wk
    hs_ref[0] = project(s_ref, hf - hb) * wk
    hn_ref[0] = jnp.sum(am * sgn_ref[...], axis=0, keepdims=True) * (0.5 / Lh)


def hyena_spectrum(Lh, w1, b1, w2, b2, w3, fr, cmat, smat, sgn):
    z = _hyena_features(Lh)
    w1p = jnp.pad(w1, ((0, 128 - w1.shape[0]), (0, 0)))
    deltas = jnp.linspace(HY_MIN_DECAY, HY_MAX_DECAY, HY_CH, dtype=F32)[None]
    nct = HY_CH // HY_CT
    full = lambda *shape: pl.BlockSpec(shape, lambda o, c: (0,) * len(shape))
    return pl.pallas_call(
        functools.partial(_hy_filter_kernel, Lh=Lh),
        grid=(HY_ORDER, nct),
        in_specs=[full(Lh, 128), full(128, 64), full(1, 64), full(64, 64), full(1, 64), full(2, 64),
                  pl.BlockSpec((64, HY_CT), lambda o, c: (0, o * 2 * nct + c)),
                  pl.BlockSpec((64, HY_CT), lambda o, c: (0, o * 2 * nct + nct + c)),
                  pl.BlockSpec((1, HY_CT), lambda o, c: (0, c)),
                  full(Lh, Lh), full(Lh, Lh), full(Lh, 1)],
        out_specs=[pl.BlockSpec((1, Lh, HY_CT), lambda o, c: (o, 0, c)),
                   pl.BlockSpec((1, Lh, HY_CT), lambda o, c: (o, 0, c)),
                   pl.BlockSpec((1, 1, HY_CT), lambda o, c: (o, 0, c))],
        out_shape=[jax.ShapeDtypeStruct((HY_ORDER, Lh, HY_CH), F32),
                   jax.ShapeDtypeStruct((HY_ORDER, Lh, HY_CH), F32),
                   jax.ShapeDtypeStruct((HY_ORDER, 1, HY_CH), F32)],
        compiler_params=_cparams(("arbitrary", "arbitrary")),
    )(z, w1p, b1[None], w2, b2[None], fr, w3, w3, deltas, cmat, smat, sgn)


def _hyena_seq(row0, Lh, x1_ref, x2_ref, y_ref, cw_refs, cb_refs, bias_ref, c_ref, s_ref, hc_ref, hs_ref, hn_ref, *,
               scratch):
    r = lax.broadcasted_iota(jnp.int32, (Lh, 1), 0)

    def sconv(u_ref, w_ref, b_ref):
        u = u_ref[0, row0:row0 + Lh, :].astype(F32)
        up = jnp.where(r == 0, 0.0, pltpu.roll(u, 1, 0))
        dn = jnp.where(r == Lh - 1, 0.0, pltpu.roll(u, Lh - 1, 0))
        return up * w_ref[0:1, :] + u * w_ref[1:2, :] + dn * w_ref[2:3, :] + b_ref[...]

    gate_refs = (x1_ref, x2_ref)
    ys_ref, yb_ref, cv_ref = scratch
    rows = slice(0, Lh)
    ys_ref[rows, :] = sconv(y_ref, cw_refs[2], cb_refs[2])
    sgn = (1 - 2 * (r & 1)).astype(F32)
    tk = min(HY_TK, Lh)
    for o in range(HY_ORDER):
        y = ys_ref[rows, :]
        yb_ref[rows, :] = y.astype(BF16)
        cv_ref[rows, :] = sgn * (jnp.sum(y * sgn, axis=0, keepdims=True) * hn_ref[o])
        for m in range(Lh // tk):
            fs = slice(m * tk, (m + 1) * tk)
            yc = jnp.dot(c_ref[fs, :], yb_ref[rows, :], preferred_element_type=F32)
            ysn = jnp.dot(s_ref[fs, :], yb_ref[rows, :], preferred_element_type=F32)
            hc = hc_ref[o, fs, :]
            hs = hs_ref[o, fs, :]
            pc = (yc * hc - ysn * hs).astype(BF16)
            ps = (yc * hs + ysn * hc).astype(BF16)
            cv_ref[rows, :] += (jnp.dot(c_ref[:, fs], pc, preferred_element_type=F32)
                                + jnp.dot(s_ref[:, fs], ps, preferred_element_type=F32))
        ys_ref[rows, :] = (sconv(gate_refs[o], cw_refs[o], cb_refs[o])
                           * (cv_ref[rows, :] + ys_ref[rows, :] * bias_ref[o:o + 1, :]))
    return ys_ref[rows, :]


def _hyena_kernel(x1_ref, x2_ref, y_ref, w1_ref, w2_ref, w3_ref, b1_ref, b2_ref, b3_ref, bias_ref, *rest, with_ctx):
    nmat = 10 if with_ctx else 5
    o_ref = rest[nmat]
    scratch = rest[nmat + 1:]
    seqs = [(LC, S, rest[0:5])]
    if with_ctx:
        seqs.append((0, LC, rest[5:10]))
    for row0, Lh, mats in seqs:
        y = _hyena_seq(row0, Lh, x1_ref, x2_ref, y_ref, (w1_ref, w2_ref, w3_ref), (b1_ref, b2_ref, b3_ref),
                       bias_ref, *mats, scratch=scratch)
        out0 = row0 if with_ctx else 0
        o_ref[0, out0:out0 + Lh, :] = y.astype(o_ref.dtype)


def hyena_mixer(u, conv_w, conv_b, w1, b1, w2, b2, w3, fr, bias, with_ctx):
    nct = HY_CH // HY_CT
    hc0 = U_OFF["hy"] // HY_CT
    once = pl.Buffered(1)
    consts = []
    specs = []
    for Lh in ((S, LC) if with_ctx else (S,)):
        cm, sm, sgn = _dft_consts(Lh)
        cm, sm, sgn = jnp.asarray(cm).astype(BF16), jnp.asarray(sm).astype(BF16), jnp.asarray(sgn)
        hc, hs, hn = hyena_spectrum(Lh, w1, b1, w2, b2, w3, fr, cm, sm, sgn)
        consts += [cm, sm, hc, hs, hn]
        specs += [pl.BlockSpec((Lh, Lh), lambda c, b: (0, 0), pipeline_mode=once),
                  pl.BlockSpec((Lh, Lh), lambda c, b: (0, 0), pipeline_mode=once),
                  pl.BlockSpec((HY_ORDER, Lh, HY_CT), lambda c, b: (0, 0, c), pipeline_mode=once),
                  pl.BlockSpec((HY_ORDER, Lh, HY_CT), lambda c, b: (0, 0, c), pipeline_mode=once),
                  pl.BlockSpec((HY_ORDER, 1, HY_CT), lambda c, b: (0, 0, c), pipeline_mode=once)]
    rows_out = L if with_ctx else S
    ublk = lambda j: pl.BlockSpec((1, L, HY_CT), lambda c, b: (b, 0, hc0 + j * nct + c))
    wblk = lambda j: pl.BlockSpec((3, HY_CT), lambda c, b: (0, j * nct + c))
    bblk = lambda j: pl.BlockSpec((1, HY_CT), lambda c, b: (0, j * nct + c))
    return pl.pallas_call(
        functools.partial(_hyena_kernel, with_ctx=with_ctx),
        grid=(nct, B),
        in_specs=[ublk(0), ublk(1), ublk(2), wblk(0), wblk(1), wblk(2), bblk(0), bblk(1), bblk(2),
                  pl.BlockSpec((HY_ORDER, HY_CT), lambda c, b: (0, c))] + specs,
        out_specs=pl.BlockSpec((1, rows_out, HY_CT), lambda c, b: (b, 0, c)),
        out_shape=jax.ShapeDtypeStruct((B, rows_out, HY_CH), BF16),
        scratch_shapes=[pltpu.VMEM((S, HY_CT), F32), pltpu.VMEM((S, HY_CT), BF16), pltpu.VMEM((S, HY_CT), F32)],
        compiler_params=_cparams(("arbitrary", "arbitrary")),
    )(u, u, u, conv_w, conv_w, conv_w, conv_b[None], conv_b[None], conv_b[None], bias, *consts)


def _permute_w_in(w):
    parts = []
    for n in U_ORDER:
        o, wd = REF_COLS[n]
        p = w[:, o:o + wd]
        if wd < 128:
            p = jnp.pad(p, ((0, 0), (0, 128 - wd)))
        parts.append(p)
    parts.append(jnp.zeros((w.shape[0], U_W - U_USED), w.dtype))
    return jnp.concatenate(parts, axis=1).astype(BF16)


def kernel(x, c, ctx, c_ctx, norm1_w, norm2_w, ada_w, ada_b, w_in, w_out, gla_gate_w, gla_gate_b, gla_norm_w,
           swa_q_norm_w, swa_k_norm_w, swa_sink, hyena_conv_w, hyena_conv_b, hyena_ffn_w1, hyena_ffn_b1,
           hyena_ffn_w2, hyena_ffn_b2, hyena_ffn_w3, hyena_ffn_freq, hyena_bias, diff_q_norm_w, diff_k_norm_w,
           diff_lambda, diff_subln_w, router_w, router_bias, expert_w_gate, expert_w_up, expert_w_down):
    assert x.shape == (B, S, D) and ctx.shape == (B, LC, D)
    cc = jnp.zeros((16, D), F32).at[:B].set(c).at[B].set(c_ctx)
    mods = ada_mod(cc, ada_w, ada_b)
    cos, sin = rope_tables128()
    rw_pad = jnp.pad(router_w, ((0, 0), (0, 128 - N_EXPERTS)))
    xall = jnp.concatenate([ctx, x], axis=1)

    for l in range(DEPTH):
        with_ctx = l < DEPTH - 1
        lam_init = 0.8 - 0.6 * math.exp(-0.3 * l)
        ml = mods[l, :B].reshape(B, 1, 6 * D)
        mc = mods[l, B:B + 1]
        u = in_proj(xall, norm1_w[l][None], ml, mc, _permute_w_in(w_in[l]))
        mix_a = gla_mixer(u, gla_gate_w[l], gla_gate_b[l], gla_norm_w[l], with_ctx)
        mix_b = swa_mixer(u, swa_sink[l], swa_q_norm_w[l], swa_k_norm_w[l], cos, sin, with_ctx)
        mix_y = hyena_mixer(u, hyena_conv_w[l], hyena_conv_b[l], hyena_ffn_w1[l], hyena_ffn_b1[l],
                            hyena_ffn_w2[l], hyena_ffn_b2[l], hyena_ffn_w3[l], hyena_ffn_freq[l],
                            hyena_bias[l], with_ctx)
        mix_d = diff_mixer(u, diff_q_norm_w[l], diff_k_norm_w[l], cos, sin, diff_lambda[l], diff_subln_w[l],
                           lam_init, with_ctx)
        xall, f, logits = out_proj(xall, (mix_a, mix_b, mix_y, mix_d), w_out[l].astype(BF16), norm2_w[l][None],
                                   ml, mc, rw_pad, with_ctx)
        T = B * xall.shape[1]
        src, wts, dest, tile_e, tile_v, tile_n = route(logits.reshape(T, 128)[:, :N_EXPERTS], router_bias)
        ys = moe_experts(tile_e, tile_v, tile_n, src, f.reshape(T * TOK_SUB, 128), expert_w_gate, expert_w_up,
                         expert_w_down, l)
        xall = moe_combine(dest, wts, xall, ml, mc, ys, with_ctx)
    return xall
```
